```python
import math
import jax, jax.numpy as jnp
from jax import lax
import numpy as np

D_MODEL = 1024
BATCH = 4
SEQ = 4096
DEPTH = 1

NSA_HEAD_DIM = 64
NSA_HEADS = D_MODEL // (2 * NSA_HEAD_DIM)
NSA_KV_GROUPS = 2
NSA_Q_PER_KV = NSA_HEADS // NSA_KV_GROUPS
NSA_WIDTH = NSA_HEADS * NSA_HEAD_DIM
NSA_KV_WIDTH = NSA_KV_GROUPS * NSA_HEAD_DIM
CMP_BLOCK = 32
CMP_STRIDE = 16
CMP_HIDDEN = 2 * NSA_HEAD_DIM
SEL_BLOCK = 64
SEL_TOP = 8
WINDOW = 512
Q_BLOCK = 128
FORCE_BONUS = 1.0e4
RET_HEAD_DIM = 128
RET_HEADS = (D_MODEL - NSA_WIDTH) // RET_HEAD_DIM
RET_WIDTH = RET_HEADS * RET_HEAD_DIM
RET_CHUNK = 128
ROPE_THETA = 10000.0
RMS_EPS = 1e-6
N_EXPERTS = 256
N_EXPERT_GROUPS = 8
TOPK_GROUPS = 4
TOP_K = 8
EXPERT_DIM = D_MODEL // 4
ROUTED_SCALE = 2.5
MOE_BLOCK = 128
IN_SIZES = (NSA_WIDTH,
            NSA_KV_WIDTH, NSA_KV_WIDTH,
            NSA_KV_WIDTH, NSA_KV_WIDTH,
            NSA_KV_WIDTH, NSA_KV_WIDTH,
            NSA_HEADS * 3,
            RET_WIDTH, RET_WIDTH, RET_WIDTH, RET_WIDTH)
IN_WIDTH = sum(IN_SIZES)
IN_SPLITS = tuple(int(v) for v in np.cumsum(IN_SIZES)[:-1])

kernel_name = 'hybrid_nsa_retention_moe_block'


def _rmsnorm(x, g, eps=RMS_EPS):
    xf = x.astype(jnp.float32)
    y = xf * lax.rsqrt(jnp.mean(xf * xf, axis=-1, keepdims=True) + eps)
    return (y * g.astype(jnp.float32)).astype(x.dtype)


def _heads(t, n):
    B, S, _ = t.shape
    return t.reshape(B, S, n, -1).transpose(0, 2, 1, 3)


def _rope(x, pos):
    d = x.shape[-1]
    inv = ROPE_THETA ** (-jnp.arange(0, d, 2, dtype=jnp.float32) / d)
    ang = pos.astype(jnp.float32)[:, None, :, None] * inv
    cos, sin = jnp.cos(ang), jnp.sin(ang)
    x1, x2 = jnp.split(x.astype(jnp.float32), 2, axis=-1)
    return jnp.concatenate([x1 * cos - x2 * sin, x1 * sin + x2 * cos], axis=-1).astype(x.dtype)


def _masked_softmax(s, mask):
    s = jnp.where(mask, s, -1e30)
    m = jnp.max(s, axis=-1, keepdims=True)
    e = jnp.where(mask, jnp.exp(s - m), 0.0)
    return e / jnp.maximum(jnp.sum(e, axis=-1, keepdims=True), 1e-20)


def _compress(k, pos_emb, w1, w2):
    B, G, S, dh = k.shape
    n_cmp = (S - CMP_BLOCK) // CMP_STRIDE + 1
    idx = np.arange(n_cmp)[:, None] * CMP_STRIDE + np.arange(CMP_BLOCK)[None, :]
    blocks = k[:, :, idx] + pos_emb
    flat = blocks.reshape(B, G, n_cmp, CMP_BLOCK * dh)
    return jax.nn.gelu(flat @ w1) @ w2


def _nsa(q, kc, vc, ks, vs, kw, vw, gates):
    B, H, S, dh = q.shape
    G, R = NSA_KV_GROUPS, NSA_Q_PER_KV
    n_q = S // Q_BLOCK
    n_sel = S // SEL_BLOCK
    n_top = min(SEL_TOP, n_sel)
    n_cmp = kc.shape[2]
    scale = dh ** -0.5
    cmp_start = np.arange(n_cmp) * CMP_STRIDE
    cmp_end = jnp.asarray(cmp_start + CMP_BLOCK - 1, jnp.int32)
    sel_start = np.arange(n_sel) * SEL_BLOCK
    overlap = jnp.asarray((cmp_start[:, None] < sel_start[None, :] + SEL_BLOCK)
                          & (cmp_start[:, None] + CMP_BLOCK > sel_start[None, :]), jnp.float32)
    kcf, vcf = kc.astype(jnp.float32), vc.astype(jnp.float32)
    ks_blocks = ks.reshape(B, G, n_sel, SEL_BLOCK, dh)
    vs_blocks = vs.reshape(B, G, n_sel, SEL_BLOCK, dh)
    pad = jnp.zeros((B, G, WINDOW, dh), kw.dtype)
    kw_p = jnp.concatenate([pad, kw], axis=2)
    vw_p = jnp.concatenate([pad, vw], axis=2)
    bi = jnp.arange(B)[:, None, None, None]
    gi = jnp.arange(G)[None, :, None, None]
    j = jnp.arange(n_sel)

    def one_block(qb):
        t0 = qb * Q_BLOCK
        t = t0 + jnp.arange(Q_BLOCK)
        qblk = lax.dynamic_slice_in_dim(q, t0, Q_BLOCK, axis=2).astype(jnp.float32)
        qblk = qblk.reshape(B, G, R, Q_BLOCK, dh) * scale
        s_c = jnp.einsum('bgrtd,bgcd->bgrtc', qblk, kcf)
        p_c = _masked_softmax(s_c, cmp_end[None, :] <= t[:, None])
        o_c = jnp.einsum('bgrtc,bgcd->bgrtd', p_c, vcf)
        imp = jnp.einsum('bgrtc,cj->bgtj', p_c, overlap)
        cur = t // SEL_BLOCK
        valid = j[None, :] * SEL_BLOCK <= t[:, None]
        forced = (j[None, :] == 0) | (j[None, :] == cur[:, None]) | (j[None, :] == cur[:, None] - 1)
        imp = jnp.where(valid, imp + FORCE_BONUS * forced, -1.0)
        _, sel = lax.top_k(imp, n_top)
        k_sel = ks_blocks[bi, gi, sel].reshape(B, G, Q_BLOCK, n_top * SEL_BLOCK, dh)
        v_sel = vs_blocks[bi, gi, sel].reshape(B, G, Q_BLOCK, n_top * SEL_BLOCK, dh)
        pos_sel = (sel[..., None] * SEL_BLOCK + jnp.arange(SEL_BLOCK)).reshape(B, G, Q_BLOCK, -1)
        s_s = jnp.einsum('bgrtd,bgtnd->bgrtn', qblk, k_sel.astype(jnp.float32))
        p_s = _masked_softmax(s_s, (pos_sel <= t[None, None, :, None])[:, :, None])
        o_s = jnp.einsum('bgrtn,bgtnd->bgrtd', p_s, v_sel.astype(jnp.float32))
        k_w = lax.dynamic_slice_in_dim(kw_p, t0, Q_BLOCK + WINDOW, axis=2).astype(jnp.float32)
        v_w = lax.dynamic_slice_in_dim(vw_p, t0, Q_BLOCK + WINDOW, axis=2).astype(jnp.float32)
        spos = t0 - WINDOW + jnp.arange(Q_BLOCK + WINDOW)
        m_w = (spos[None, :] <= t[:, None]) & (spos[None, :] > t[:, None] - WINDOW) & (spos[None, :] >= 0)
        s_w = jnp.einsum('bgrtd,bgsd->bgrts', qblk, k_w)
        o_w = jnp.einsum('bgrts,bgsd->bgrtd', _masked_softmax(s_w, m_w), v_w)
        g = lax.dynamic_slice_in_dim(gates, t0, Q_BLOCK, axis=1)
        g = g.transpose(0, 2, 1, 3).reshape(B, G, R, Q_BLOCK, 3)
        o = g[..., 0:1] * o_c + g[..., 1:2] * o_s + g[..., 2:3] * o_w
        return o.reshape(B, H, Q_BLOCK, dh)

    out = lax.map(one_block, jnp.arange(n_q))
    return out.transpose(1, 0, 3, 2, 4).reshape(B, S, H * dh)


def _retention(q, k, v, pos):
    B, H, S, dk = q.shape
    C = RET_CHUNK
    N = S // C
    q = _rope(q, pos).astype(jnp.float32)
    k = _rope(k, pos).astype(jnp.float32) * (dk ** -0.5)
    v = v.astype(jnp.float32)
    log_g = jnp.log(1.0 - 2.0 ** (-5.0 - jnp.arange(H, dtype=jnp.float32)))
    i = jnp.arange(C, dtype=jnp.float32)
    diff = i[:, None] - i[None, :]
    causal = diff >= 0
    dec_in = jnp.where(causal, jnp.exp(log_g[:, None, None] * jnp.where(causal, diff, 0.0)), 0.0)
    xi = jnp.exp(log_g[:, None] * (i + 1.0))
    zeta = jnp.exp(log_g[:, None] * (C - 1.0 - i))
    g_chunk = jnp.exp(log_g * C)
    qc = q.reshape(B, H, N, C, dk)
    kc = k.reshape(B, H, N, C, dk)
    vc = v.reshape(B, H, N, C, -1)
    scores = jnp.einsum('bhncd,bhnsd->bhncs', qc, kc) * dec_in[None, :, None]
    inner = jnp.einsum('bhncs,bhnse->bhnce', scores, vc)
    kv = jnp.einsum('bhnsd,bhnse->nbhde', kc * zeta[None, :, None, :, None], vc)

    def step(state, kv_n):
        return state * g_chunk[None, :, None, None] + kv_n, state

    _, state_prev = lax.scan(step, jnp.zeros(kv.shape[1:], jnp.float32), kv)
    cross = jnp.einsum('bhncd,nbhde->bhnce', qc * xi[None, :, None, :, None], state_prev)
    return (inner + cross).reshape(B, H, S, -1)


def _swiglu(h, wg, wu, wd):
    return (jax.nn.silu(h @ wg) * (h @ wu)) @ wd


def _moe(h2, w_router, router_bias, w_gate_e, w_up_e, w_down_e, w_gate_s, w_up_s, w_down_s):
    T, D = h2.shape
    s = jax.nn.sigmoid(h2.astype(jnp.float32) @ w_router.astype(jnp.float32))
    sb = s + router_bias.astype(jnp.float32)
    grp_score = lax.top_k(sb.reshape(T, N_EXPERT_GROUPS, -1), 2)[0].sum(-1)
    _, gidx = lax.top_k(grp_score, TOPK_GROUPS)
    gmask = jax.nn.one_hot(gidx, N_EXPERT_GROUPS, dtype=jnp.float32).sum(1) > 0
    emask = jnp.repeat(gmask, N_EXPERTS // N_EXPERT_GROUPS, axis=1)
    _, idx = lax.top_k(jnp.where(emask, sb, -1e30), TOP_K)
    w = jnp.take_along_axis(s, idx, axis=1)
    w = w / jnp.sum(w, axis=-1, keepdims=True) * ROUTED_SCALE
    A = T * TOP_K
    flat_e = idx.reshape(-1)
    flat_tok = jnp.repeat(jnp.arange(T, dtype=jnp.int32), TOP_K)
    order = jnp.argsort(flat_e)
    e_sorted, tok_sorted, w_sorted = flat_e[order], flat_tok[order], w.reshape(-1)[order]
    counts = jnp.bincount(flat_e, length=N_EXPERTS)
    padded = ((counts + MOE_BLOCK - 1) // MOE_BLOCK) * MOE_BLOCK
    ends_pad = jnp.cumsum(padded)
    starts_pad = ends_pad - padded
    starts = jnp.cumsum(counts) - counts
    dest = starts_pad[e_sorted] + (jnp.arange(A) - starts[e_sorted])
    n_blocks = -(-A // MOE_BLOCK) + N_EXPERTS
    P = n_blocks * MOE_BLOCK
    buf_tok = jnp.zeros((P,), jnp.int32).at[dest].set(tok_sorted)
    buf_w = jnp.zeros((P,), jnp.float32).at[dest].set(w_sorted)
    blk_e = jnp.minimum(jnp.searchsorted(ends_pad, jnp.arange(n_blocks) * MOE_BLOCK, side='right'),
                        N_EXPERTS - 1)

    def expert_step(acc, blk):
        tok, wt, e = blk
        yb = _swiglu(h2[tok], w_gate_e[e], w_up_e[e], w_down_e[e]).astype(jnp.float32)
        return acc.at[tok].add(yb * wt[:, None]), None

    routed, _ = lax.scan(expert_step, jnp.zeros((T, D), jnp.float32),
                         (buf_tok.reshape(n_blocks, MOE_BLOCK), buf_w.reshape(n_blocks, MOE_BLOCK), blk_e))
    shared = _swiglu(h2, w_gate_s, w_up_s, w_down_s).astype(jnp.float32)
    return (shared + routed).astype(h2.dtype)


def setup_inputs(seed: int = 0) -> dict:
    key = jax.random.key(seed)
    ks = jax.random.split(key, 26)
    f32 = jnp.float32
    nrm = lambda k, shape, fan: jax.random.normal(k, shape, f32) * (fan ** -0.5)
    gain = lambda k, shape: 1.0 + 0.02 * jax.random.normal(k, shape, f32)
    L, D, dh = DEPTH, D_MODEL, NSA_HEAD_DIM
    offset = jax.random.randint(ks[2], (BATCH, 1), 0, 2048, dtype=jnp.int32)
    return {
        'x': jax.random.normal(ks[0], (BATCH, SEQ, D), f32),
        'c': jax.random.normal(ks[1], (BATCH, D), f32),
        'positions': offset + jnp.arange(SEQ, dtype=jnp.int32)[None, :],
        'w_ada': 0.5 * nrm(ks[3], (L, D, 6 * D), D),
        'b_ada': 0.02 * jax.random.normal(ks[4], (L, 6 * D), f32),
        'g_norm_mix': gain(ks[5], (L, D)),
        'w_in': nrm(ks[6], (L, D, IN_WIDTH), D),
        'cmp_pos_k': 0.02 * jax.random.normal(ks[7], (L, CMP_BLOCK, dh), f32),
        'cmp_w1_k': nrm(ks[8], (L, CMP_BLOCK * dh, CMP_HIDDEN), CMP_BLOCK * dh),
        'cmp_w2_k': nrm(ks[9], (L, CMP_HIDDEN, dh), CMP_HIDDEN),
        'cmp_pos_v': 0.02 * jax.random.normal(ks[10], (L, CMP_BLOCK, dh), f32),
        'cmp_w1_v': nrm(ks[11], (L, CMP_BLOCK * dh, CMP_HIDDEN), CMP_BLOCK * dh),
        'cmp_w2_v': nrm(ks[12], (L, CMP_HIDDEN, dh), CMP_HIDDEN),
        'g_nsa_out': gain(ks[13], (L, NSA_WIDTH)),
        'g_ret_out': gain(ks[14], (L, RET_HEADS, RET_HEAD_DIM)),
        'w_out': nrm(ks[15], (L, D, D), D),
        'g_norm_ffn': gain(ks[16], (L, D)),
        'w_router': nrm(ks[17], (L, D, N_EXPERTS), D),
        'router_bias': 0.01 * jax.random.normal(ks[18], (L, N_EXPERTS), f32),
        'w_gate_e': nrm(ks[19], (L, N_EXPERTS, D, EXPERT_DIM), D),
        'w_up_e': nrm(ks[20], (L, N_EXPERTS, D, EXPERT_DIM), D),
        'w_down_e': nrm(ks[21], (L, N_EXPERTS, EXPERT_DIM, D), EXPERT_DIM),
        'w_gate_s': nrm(ks[22], (L, D, EXPERT_DIM), D),
        'w_up_s': nrm(ks[23], (L, D, EXPERT_DIM), D),
        'w_down_s': nrm(ks[24], (L, EXPERT_DIM, D), EXPERT_DIM),
        'g_norm_final': gain(ks[25], (D,)),
    }


def reference(x, c, positions, w_ada, b_ada, g_norm_mix, w_in, cmp_pos_k, cmp_w1_k, cmp_w2_k,
              cmp_pos_v, cmp_w1_v, cmp_w2_v, g_nsa_out, g_ret_out, w_out, g_norm_ffn,
              w_router, router_bias, w_gate_e, w_up_e, w_down_e, w_gate_s, w_up_s, w_down_s,
              g_norm_final):
    B, S, D = x.shape
    n_cmp = (S - CMP_BLOCK) // CMP_STRIDE + 1
    cmp_end = np.arange(n_cmp) * CMP_STRIDE + CMP_BLOCK - 1
    pos_cmp = positions[:, cmp_end]
    for l in range(DEPTH):
        mod = jax.nn.silu(c) @ w_ada[l] + b_ada[l]
        sh_a, sc_a, gt_a, sh_m, sc_m, gt_m = [m[:, None, :] for m in jnp.split(mod, 6, axis=-1)]
        h = _rmsnorm(x, g_norm_mix[l]) * (1.0 + sc_a) + sh_a
        proj = h @ w_in[l]
        (q_n, kc_n, vc_n, ks_n, vs_n, kw_n, vw_n, gate_n,
         q_r, k_r, v_r, gate_r) = jnp.split(proj, IN_SPLITS, axis=-1)
        q_nsa = _rope(_heads(q_n, NSA_HEADS), positions)
        k_cmp = _rope(_compress(_heads(kc_n, NSA_KV_GROUPS), cmp_pos_k[l], cmp_w1_k[l], cmp_w2_k[l]), pos_cmp)
        v_cmp = _compress(_heads(vc_n, NSA_KV_GROUPS), cmp_pos_v[l], cmp_w1_v[l], cmp_w2_v[l])
        k_sel = _rope(_heads(ks_n, NSA_KV_GROUPS), positions)
        k_win = _rope(_heads(kw_n, NSA_KV_GROUPS), positions)
        gates = jax.nn.sigmoid(gate_n.astype(jnp.float32)).reshape(B, S, NSA_HEADS, 3)
        o_nsa = _nsa(q_nsa, k_cmp, v_cmp, k_sel, _heads(vs_n, NSA_KV_GROUPS),
                     k_win, _heads(vw_n, NSA_KV_GROUPS), gates)
        o_nsa = _rmsnorm(o_nsa, g_nsa_out[l])
        o_ret = _retention(_heads(q_r, RET_HEADS), _heads(k_r, RET_HEADS), _heads(v_r, RET_HEADS), positions)
        o_ret = _rmsnorm(o_ret.transpose(0, 2, 1, 3), g_ret_out[l]).reshape(B, S, RET_WIDTH)
        o_ret = jax.nn.silu(gate_r.astype(jnp.float32)) * o_ret
        mix = jnp.concatenate([o_nsa, o_ret], axis=-1).astype(x.dtype) @ w_out[l]
        x = x + gt_a * mix
        h = _rmsnorm(x, g_norm_ffn[l]) * (1.0 + sc_m) + sh_m
        y = _moe(h.reshape(B * S, D), w_router[l], router_bias[l], w_gate_e[l], w_up_e[l], w_down_e[l],
                 w_gate_s[l], w_up_s[l], w_down_s[l])
        x = x + gt_m * y.reshape(B, S, D)
    return _rmsnorm(x, g_norm_final)
```

```python
import functools
import math

import numpy as np
import jax
import jax.numpy as jnp
from jax import lax
from jax.experimental import pallas as pl
from jax.experimental.pallas import tpu as pltpu

F32 = jnp.float32
BF16 = jnp.bfloat16
I32 = jnp.int32
HIGHEST = lax.Precision.HIGHEST

LANES = 128
NSA_HEAD_DIM = 64
NSA_HEADS = 8
NSA_GROUPS = 2
NSA_Q_PER_KV = NSA_HEADS // NSA_GROUPS
CMP_BLOCK = 32
CMP_STRIDE = 16
CMP_HIDDEN = 128
SEL_BLOCK = 64
SEL_TOP = 8
WINDOW = 512
Q_BLOCK = 128
FORCE_BONUS = 1.0e4
RET_HEADS = 4
RET_HEAD_DIM = 128
RET_CHUNK = 128
ROPE_THETA = 10000.0
RMS_EPS = 1e-6
N_EXPERTS = 256
N_EXPERT_GROUPS = 8
TOPK_GROUPS = 4
TOP_K = 8
ROUTED_SCALE = 2.5
MOE_TILE = 128
SEL_CHUNK = 256
NEG = -1e30
VMEM_LIMIT = 48 * 1024 * 1024


def _cparams(sem):
    return pltpu.CompilerParams(dimension_semantics=sem, vmem_limit_bytes=VMEM_LIMIT)


def _dot(a, b, **kw):
    return jnp.dot(a, b, preferred_element_type=F32, **kw)


def _dot_nt(a, b, **kw):
    return lax.dot_general(a, b, (((1,), (1,)), ((), ())), preferred_element_type=F32, **kw)


def _sigmoid(v):
    return 1.0 / (1.0 + jnp.exp(-v))


def _silu(v):
    return v * _sigmoid(v)


def _ada_kernel(c_ref, w_ref, b_ref, o_ref):
    o_ref[...] = _dot(_silu(c_ref[...]), w_ref[...], precision=HIGHEST) + b_ref[...]


def _ada(c, w, b):
    bsz, d = c.shape
    n = w.shape[1]
    tn = 1536
    cp = jnp.zeros((8, d), F32).at[:bsz].set(c)
    out = pl.pallas_call(
        _ada_kernel,
        grid=(n // tn,),
        in_specs=[pl.BlockSpec((8, d), lambda j: (0, 0)),
                  pl.BlockSpec((d, tn), lambda j: (0, j)),
                  pl.BlockSpec((1, tn), lambda j: (0, j))],
        out_specs=pl.BlockSpec((8, tn), lambda j: (0, j)),
        out_shape=jax.ShapeDtypeStruct((8, n), F32),
        compiler_params=_cparams(("arbitrary",)),
        name="ada",
    )(cp, w, b.reshape(1, n))
    return out[:bsz]


def _rope_table_kernel(pos_ref, inv_ref, sgn_ref, cos_ref, sin_ref):
    ang = pos_ref[...] * inv_ref[...]
    cos_ref[...] = jnp.cos(ang)
    sin_ref[...] = jnp.sin(ang) * sgn_ref[...]


def _rope_tables(posf):
    t = posf.shape[0]
    inv_n = ROPE_THETA ** (-jnp.arange(0, NSA_HEAD_DIM, 2, dtype=F32) / NSA_HEAD_DIM)
    inv_r = ROPE_THETA ** (-jnp.arange(0, RET_HEAD_DIM, 2, dtype=F32) / RET_HEAD_DIM)
    inv = jnp.concatenate([jnp.tile(inv_n, 4), jnp.tile(inv_r, 2)]).reshape(1, 2 * LANES)
    sgn_n = np.where((np.arange(LANES) % NSA_HEAD_DIM) < NSA_HEAD_DIM // 2, -1.0, 1.0)
    sgn_r = np.where(np.arange(LANES) < RET_HEAD_DIM // 2, -1.0, 1.0)
    sgn = jnp.asarray(np.concatenate([sgn_n, sgn_r]).reshape(1, 2 * LANES), F32)
    tm = min(t, 1024)
    return pl.pallas_call(
        _rope_table_kernel,
        grid=(t // tm,),
        in_specs=[pl.BlockSpec((tm, 1), lambda i: (i, 0)),
                  pl.BlockSpec((1, 2 * LANES), lambda i: (0, 0)),
                  pl.BlockSpec((1, 2 * LANES), lambda i: (0, 0))],
        out_specs=[pl.BlockSpec((tm, 2 * LANES), lambda i: (i, 0))] * 2,
        out_shape=[jax.ShapeDtypeStruct((t, 2 * LANES), F32)] * 2,
        compiler_params=_cparams(("arbitrary",)),
        name="rope_tables",
    )(posf, inv, sgn)


def _rope64(p, cos, sin, first_half):
    rot = jnp.where(first_half, pltpu.roll(p, 96, 1), pltpu.roll(p, 32, 1))
    return p * cos + rot * sin


def _rope128(p, cos, sin):
    return p * cos + pltpu.roll(p, 64, 1) * sin


_C_Q = 0
_C_KV = _C_Q + NSA_HEADS * LANES
_C_GATE = _C_KV + 6 * LANES
_C_RET = _C_GATE + LANES
_C_END = _C_RET + 4 * RET_HEADS * RET_HEAD_DIM


def _arrange_w_in(w_in):
    d = w_in.shape[0]
    nw = NSA_HEADS * NSA_HEAD_DIM
    q = w_in[:, :nw].reshape(d, NSA_HEADS, NSA_HEAD_DIM)
    z = jnp.zeros_like(q)
    grp = (jnp.arange(NSA_HEADS) // NSA_Q_PER_KV)[None, :, None]
    qpad = jnp.where(grp == 0, jnp.concatenate([q, z], -1), jnp.concatenate([z, q], -1))
    qpad = qpad.reshape(d, NSA_HEADS * LANES)
    kv = w_in[:, nw:nw + 6 * LANES]
    g0 = nw + 6 * LANES
    gate = jnp.pad(w_in[:, g0:g0 + 3 * NSA_HEADS], ((0, 0), (0, LANES - 3 * NSA_HEADS)))
    ret = w_in[:, g0 + 3 * NSA_HEADS:]
    return jnp.concatenate([qpad, kv, gate, ret], axis=1).astype(BF16)


def _inproj_kernel(x_ref, mod_ref, g_ref, w_ref, cos_ref, sin_ref,
                   q_ref, kc_ref, vc_ref, ksw_ref, gate_ref, qr_ref, kr_ref, vr_ref, gr_ref):
    x = x_ref[...]
    tm = x.shape[0]
    ms = jnp.mean(x * x, axis=-1, keepdims=True)
    y = x * lax.rsqrt(ms + RMS_EPS) * g_ref[...]
    h = y * (1.0 + mod_ref[0, 1:2, :]) + mod_ref[0, 0:1, :]
    hb = h.astype(BF16)
    cos_n, sin_n = cos_ref[:, 0:LANES], sin_ref[:, 0:LANES]
    cos_r, sin_r = cos_ref[:, LANES:], sin_ref[:, LANES:]
    lane = lax.broadcasted_iota(I32, (tm, LANES), 1)
    first_half = (lane % NSA_HEAD_DIM) < (NSA_HEAD_DIM // 2)
    scale_n = NSA_HEAD_DIM ** -0.5
    scale_r = RET_HEAD_DIM ** -0.5

    def proj(c0, n):
        return _dot(hb, w_ref[:, c0:c0 + n])

    for hh in range(NSA_HEADS):
        p = proj(_C_Q + hh * LANES, LANES)
        q_ref[:, hh * LANES:(hh + 1) * LANES] = (_rope64(p, cos_n, sin_n, first_half) * scale_n).astype(BF16)
    kv = proj(_C_KV, 6 * LANES)
    kc_ref[...] = kv[:, 0:LANES].astype(BF16)
    vc_ref[...] = kv[:, LANES:2 * LANES].astype(BF16)
    ksw_ref[:, 0:LANES] = _rope64(kv[:, 2 * LANES:3 * LANES], cos_n, sin_n, first_half).astype(BF16)
    ksw_ref[:, LANES:2 * LANES] = kv[:, 3 * LANES:4 * LANES].astype(BF16)
    ksw_ref[:, 2 * LANES:3 * LANES] = _rope64(kv[:, 4 * LANES:5 * LANES], cos_n, sin_n, first_half).astype(BF16)
    ksw_ref[:, 3 * LANES:4 * LANES] = kv[:, 5 * LANES:6 * LANES].astype(BF16)
    gate_ref[...] = _sigmoid(proj(_C_GATE, LANES))
    rw = RET_HEADS * RET_HEAD_DIM
    for hh in range(RET_HEADS):
        sl = slice(hh * LANES, (hh + 1) * LANES)
        pq = proj(_C_RET + hh * LANES, LANES)
        qr_ref[:, sl] = _rope128(pq, cos_r, sin_r).astype(BF16)
        pk = proj(_C_RET + rw + hh * LANES, LANES)
        kr_ref[:, sl] = (_rope128(pk, cos_r, sin_r) * scale_r).astype(BF16)
    vr_ref[...] = proj(_C_RET + 2 * rw, rw).astype(BF16)
    gr_ref[...] = proj(_C_RET + 3 * rw, rw)


def _inproj(x2, mod3, g_mix, w_in_p, cos_t, sin_t, seq):
    t, d = x2.shape
    tm = min(512, seq)
    rw = RET_HEADS * RET_HEAD_DIM
    row = lambda n: pl.BlockSpec((tm, n), lambda i: (i, 0))
    outs = [(NSA_HEADS * LANES, BF16), (LANES, BF16), (LANES, BF16), (4 * LANES, BF16), (LANES, F32),
            (rw, BF16), (rw, BF16), (rw, BF16), (rw, F32)]
    return pl.pallas_call(
        _inproj_kernel,
        grid=(t // tm,),
        in_specs=[row(d),
                  pl.BlockSpec((1, 6, d), lambda i: ((i * tm) // seq, 0, 0)),
                  pl.BlockSpec((1, d), lambda i: (0, 0)),
                  pl.BlockSpec((d, _C_END), lambda i: (0, 0)),
                  row(2 * LANES), row(2 * LANES)],
        out_specs=[row(n) for n, _ in outs],
        out_shape=[jax.ShapeDtypeStruct((t, n), dt) for n, dt in outs],
        compiler_params=_cparams(("arbitrary",)),
        name="inproj",
    )(x2, mod3, g_mix, w_in_p, cos_t, sin_t)


def _gelu_tanh(v):
    return v * (0.5 * (1.0 + jnp.tanh(math.sqrt(2.0 / math.pi) * (v + 0.044715 * (v * v * v)))))


def _compress_kernel(kc_ref, vc_ref, wkt_ref, wkb_ref, wk2_ref, pk_ref, wvt_ref, wvb_ref, wv2_ref, pv_ref,
                     cos_ref, sin_ref, ko_ref, vo_ref):
    def one(x_ref, wt_ref, wb_ref, w2_ref, p_ref):
        xx = x_ref[0]
        a = _dot(xx, wt_ref[...])
        b = _dot(xx, wb_ref[...])
        pb = p_ref[...].astype(BF16)
        bias = _dot(pb, wt_ref[...])[0:1] + _dot(pb, wb_ref[...])[1:2]
        hid = a + pltpu.roll(b, b.shape[0] - 1, 0) + bias
        return _dot(_gelu_tanh(hid).astype(BF16), w2_ref[...])

    k = one(kc_ref, wkt_ref, wkb_ref, wk2_ref, pk_ref)
    lane = lax.broadcasted_iota(I32, k.shape, 1)
    first_half = (lane % NSA_HEAD_DIM) < (NSA_HEAD_DIM // 2)
    ko_ref[0] = _rope64(k, cos_ref[0], sin_ref[0], first_half).astype(BF16)
    vo_ref[0] = one(vc_ref, wvt_ref, wvb_ref, wv2_ref, pv_ref).astype(BF16)


def _arrange_cmp_weights(pos, w1, w2):
    half = CMP_BLOCK // 2
    dh = NSA_HEAD_DIM
    w1r = w1.reshape(CMP_BLOCK, dh, CMP_HIDDEN)

    def block(wpart):
        z = jnp.zeros_like(wpart)
        g0 = jnp.concatenate([wpart, z], axis=-1)
        g1 = jnp.concatenate([z, wpart], axis=-1)
        return jnp.stack([g0, g1], axis=1).reshape(half * 2 * dh, 2 * CMP_HIDDEN)

    wt, wb = block(w1r[:half]), block(w1r[half:])
    z2 = jnp.zeros_like(w2)
    w2b = jnp.concatenate([jnp.concatenate([w2, z2], 1), jnp.concatenate([z2, w2], 1)], 0)
    ptop = jnp.tile(pos[:half], (1, 2)).reshape(1, -1)
    pbot = jnp.tile(pos[half:], (1, 2)).reshape(1, -1)
    prow = jnp.concatenate([ptop, pbot, jnp.zeros((6, ptop.shape[1]), F32)], 0)
    return wt.astype(BF16), wb.astype(BF16), w2b.astype(BF16), prow


def _compress(kc, vc, cos_c, sin_c, kparams, vparams, bsz, seq):
    nchunk = seq // CMP_STRIDE
    width = CMP_STRIDE * LANES
    kc16 = kc.reshape(bsz, nchunk, width)
    vc16 = vc.reshape(bsz, nchunk, width)
    full = lambda a: pl.BlockSpec(a.shape, lambda b: (0,) * a.ndim)
    per_b = lambda n: pl.BlockSpec((1, nchunk, n), lambda b: (b, 0, 0))
    return pl.pallas_call(
        _compress_kernel,
        grid=(bsz,),
        in_specs=[per_b(width), per_b(width)] + [full(a) for a in kparams] + [full(a) for a in vparams]
                 + [per_b(LANES), per_b(LANES)],
        out_specs=[per_b(LANES), per_b(LANES)],
        out_shape=[jax.ShapeDtypeStruct((bsz, nchunk, LANES), BF16)] * 2,
        compiler_params=_cparams(("arbitrary",)),
        name="compress",
    )(kc16, vc16, *kparams, *vparams, cos_c, sin_c)


def _online_update(carry, s, mask, v):
    m, l, acc = carry
    s = jnp.where(mask, s, NEG)
    m_new = jnp.maximum(m, jnp.max(s, axis=-1, keepdims=True))
    alpha = jnp.exp(m - m_new)
    e = jnp.where(mask, jnp.exp(s - m_new), 0.0)
    l = alpha * l + jnp.sum(e, axis=-1, keepdims=True)
    acc = alpha * acc + _dot(e.astype(BF16), v)
    return m_new, l, acc


def _nsa_kernel(q_ref, kc_ref, vc_ref, ks_ref, vs_ref, kw_ref, vw_ref, gate_ref, ovt_ref, exp_ref, g_ref,
                o_ref, *, seq):
    qb = pl.program_id(1)
    t0 = qb * Q_BLOCK
    rows = NSA_Q_PER_KV * Q_BLOCK
    n_cmp_pad = kc_ref.shape[1]
    n_sel = seq // SEL_BLOCK
    t_col = t0 + lax.broadcasted_iota(I32, (rows, 1), 0) % Q_BLOCK
    t_blk = t0 + lax.broadcasted_iota(I32, (Q_BLOCK, 1), 0)
    t_row = t0 + lax.broadcasted_iota(I32, (1, Q_BLOCK), 1)
    gates = gate_ref[...]
    lane = lax.broadcasted_iota(I32, (Q_BLOCK, LANES), 1)
    heads_out = []
    for g in range(NSA_GROUPS):
        q4 = jnp.concatenate([q_ref[:, (NSA_Q_PER_KV * g + r) * LANES:(NSA_Q_PER_KV * g + r + 1) * LANES]
                              for r in range(NSA_Q_PER_KV)], axis=0)
        s_c = _dot_nt(q4, kc_ref[0])
        cend = lax.broadcasted_iota(I32, (1, n_cmp_pad), 1) * CMP_STRIDE + (CMP_BLOCK - 1)
        mask_c = cend <= t_col
        s_c = jnp.where(mask_c, s_c, NEG)
        e_c = jnp.where(mask_c, jnp.exp(s_c - jnp.max(s_c, axis=-1, keepdims=True)), 0.0)
        p_c = e_c / jnp.maximum(jnp.sum(e_c, axis=-1, keepdims=True), 1e-20)
        o_c = _dot(p_c.astype(BF16), vc_ref[0])
        psum = p_c[0:Q_BLOCK]
        for r in range(1, NSA_Q_PER_KV):
            psum = psum + p_c[r * Q_BLOCK:(r + 1) * Q_BLOCK]
        imp_t = _dot_nt(ovt_ref[...], psum, precision=HIGHEST)
        nsp = imp_t.shape[0]
        jrow = lax.broadcasted_iota(I32, (nsp, 1), 0)
        cur = t_row // SEL_BLOCK
        valid = (jrow * SEL_BLOCK <= t_row) & (jrow < n_sel)
        forced = (jrow == 0) | (jrow == cur) | (jrow == cur - 1)
        val = jnp.where(valid, imp_t + jnp.where(forced, FORCE_BONUS, 0.0), -1.0)
        val = jnp.where(jrow < n_sel, val, -jnp.inf)
        sel_t = jnp.zeros((nsp, Q_BLOCK), F32)
        for _ in range(min(SEL_TOP, n_sel)):
            mx = jnp.max(val, axis=0, keepdims=True)
            jmin = jnp.min(jnp.where(val == mx, jrow, nsp), axis=0, keepdims=True)
            hit = jrow == jmin
            sel_t = jnp.where(hit, 1.0, sel_t)
            val = jnp.where(hit, -jnp.inf, val)
        if nsp < LANES:
            sel_t = jnp.concatenate([sel_t, jnp.zeros((LANES - nsp, Q_BLOCK), F32)], axis=0)
        sel_b = sel_t.T.astype(BF16)
        n_chunks = (t0 + Q_BLOCK + SEL_CHUNK - 1) // SEL_CHUNK

        def sel_step(c, carry):
            k0 = pl.multiple_of(c * SEL_CHUNK, SEL_CHUNK)
            s = _dot_nt(q4, ks_ref[pl.ds(k0, SEL_CHUNK), :])
            picked = _dot(sel_b, exp_ref[c])
            kpos = k0 + lax.broadcasted_iota(I32, (1, SEL_CHUNK), 1)
            m1 = (picked > 0.5) & (kpos <= t_blk)
            mask = jnp.concatenate([m1] * NSA_Q_PER_KV, axis=0)
            return _online_update(carry, s, mask, vs_ref[pl.ds(k0, SEL_CHUNK), :])

        init = (jnp.full((rows, 1), NEG, F32), jnp.zeros((rows, 1), F32), jnp.zeros((rows, LANES), F32))
        _, l_s, acc_s = lax.fori_loop(0, n_chunks, sel_step, init)
        o_s = acc_s / jnp.maximum(l_s, 1e-20)
        carry = init
        for w in range(WINDOW // Q_BLOCK + 1):
            wb = qb - WINDOW // Q_BLOCK + w
            k0 = pl.multiple_of(jnp.maximum(wb, 0) * Q_BLOCK, Q_BLOCK)
            s = _dot_nt(q4, kw_ref[pl.ds(k0, Q_BLOCK), :])
            spos = wb * Q_BLOCK + lax.broadcasted_iota(I32, (1, Q_BLOCK), 1)
            mask = (spos <= t_col) & (spos > t_col - WINDOW) & (spos >= 0)
            carry = _online_update(carry, s, mask, vw_ref[pl.ds(k0, Q_BLOCK), :])
        o_w = carry[2] / jnp.maximum(carry[1], 1e-20)
        for r in range(NSA_Q_PER_KV):
            hh = NSA_Q_PER_KV * g + r
            rs = slice(r * Q_BLOCK, (r + 1) * Q_BLOCK)
            o = (gates[:, 3 * hh:3 * hh + 1] * o_c[rs] + gates[:, 3 * hh + 1:3 * hh + 2] * o_s[rs]
                 + gates[:, 3 * hh + 2:3 * hh + 3] * o_w[rs])
            if (hh % 2) != g:
                o = pltpu.roll(o, NSA_HEAD_DIM, 1)
            heads_out.append(o)
    blocks = [jnp.where(lane < NSA_HEAD_DIM, heads_out[2 * i], heads_out[2 * i + 1])
              for i in range(NSA_HEADS // 2)]
    ss = sum(jnp.sum(b * b, axis=-1, keepdims=True) for b in blocks)
    inv = lax.rsqrt(ss / (NSA_HEADS * NSA_HEAD_DIM) + RMS_EPS)
    for i, b in enumerate(blocks):
        sl = slice(i * LANES, (i + 1) * LANES)
        o_ref[:, sl] = (b * inv * g_ref[:, sl]).astype(BF16)


def _nsa(q, kcmp, vcmp, ksw, gates, g_nsa, bsz, seq):
    t = bsz * seq
    nq = seq // Q_BLOCK
    n_cmp_pad = seq // CMP_STRIDE
    n_cmp = (seq - CMP_BLOCK) // CMP_STRIDE + 1
    n_sel = seq // SEL_BLOCK
    nsp = max(n_sel, 8)
    cs = np.arange(n_cmp_pad) * CMP_STRIDE
    ss = np.arange(nsp) * SEL_BLOCK
    ov = ((cs[None, :] < ss[:, None] + SEL_BLOCK) & (cs[None, :] + CMP_BLOCK > ss[:, None])
          & (np.arange(n_cmp_pad)[None, :] < n_cmp) & (np.arange(nsp)[:, None] < n_sel))
    ovt = jnp.asarray(ov, F32)
    nch = seq // SEL_CHUNK
    kp = np.arange(seq).reshape(nch, 1, SEL_CHUNK)
    expand = jnp.asarray(kp // SEL_BLOCK == np.arange(LANES).reshape(1, LANES, 1), BF16)
    seqcol = lambda c: pl.BlockSpec((seq, LANES), lambda b, i: (b, c))
    return pl.pallas_call(
        functools.partial(_nsa_kernel, seq=seq),
        grid=(bsz, nq),
        in_specs=[pl.BlockSpec((Q_BLOCK, NSA_HEADS * LANES), lambda b, i: (b * nq + i, 0)),
                  pl.BlockSpec((1, n_cmp_pad, LANES), lambda b, i: (b, 0, 0)),
                  pl.BlockSpec((1, n_cmp_pad, LANES), lambda b, i: (b, 0, 0)),
                  seqcol(0), seqcol(1), seqcol(2), seqcol(3),
                  pl.BlockSpec((Q_BLOCK, LANES), lambda b, i: (b * nq + i, 0)),
                  pl.BlockSpec(ovt.shape, lambda b, i: (0, 0)),
                  pl.BlockSpec(expand.shape, lambda b, i: (0, 0, 0)),
                  pl.BlockSpec((1, NSA_HEADS * NSA_HEAD_DIM), lambda b, i: (0, 0))],
        out_specs=pl.BlockSpec((Q_BLOCK, NSA_HEADS * NSA_HEAD_DIM), lambda b, i: (b * nq + i, 0)),
        out_shape=jax.ShapeDtypeStruct((t, NSA_HEADS * NSA_HEAD_DIM), BF16),
        compiler_params=_cparams(("arbitrary", "arbitrary")),
        name="nsa",
    )(q, kcmp, vcmp, ksw, ksw, ksw, ksw, gates, ovt, expand, g_nsa)


def _ret_kernel(q_ref, k_ref, v_ref, gr_ref, dec_ref, xi_ref, zeta_ref, gch_ref, g_ref, o_ref, st_ref):
    @pl.when(pl.program_id(1) == 0)
    def _():
        st_ref[...] = jnp.zeros_like(st_ref)

    for hh in range(RET_HEADS):
        sl = slice(hh * LANES, (hh + 1) * LANES)
        q, k, v = q_ref[:, sl], k_ref[:, sl], v_ref[:, sl]
        sc = _dot_nt(q, k) * dec_ref[hh]
        inner = _dot(sc.astype(BF16), v)
        st = st_ref[hh]
        cross = _dot((q.astype(F32) * xi_ref[hh]).astype(BF16), st.astype(BF16))
        kz = (k.astype(F32) * zeta_ref[hh]).T.astype(BF16)
        st_ref[hh] = st * gch_ref[hh] + _dot(kz, v)
        o = inner + cross
        y = o * lax.rsqrt(jnp.mean(o * o, axis=-1, keepdims=True) + RMS_EPS) * g_ref[hh:hh + 1, :]
        o_ref[:, sl] = (_silu(gr_ref[:, sl]) * y).astype(BF16)


def _retention(qr, kr, vr, gr, g_ret, bsz, seq):
    t = bsz * seq
    c = RET_CHUNK
    n = seq // c
    log_g = jnp.log(1.0 - 2.0 ** (-5.0 - jnp.arange(RET_HEADS, dtype=F32)))
    i = jnp.arange(c, dtype=F32)
    diff = i[:, None] - i[None, :]
    causal = diff >= 0
    dec = jnp.where(causal, jnp.exp(log_g[:, None, None] * jnp.where(causal, diff, 0.0)), 0.0)
    xi = jnp.broadcast_to(jnp.exp(log_g[:, None] * (i + 1.0))[:, :, None], (RET_HEADS, c, LANES))
    zeta = jnp.broadcast_to(jnp.exp(log_g[:, None] * (c - 1.0 - i))[:, :, None], (RET_HEADS, c, LANES))
    gch = jnp.broadcast_to(jnp.exp(log_g * c)[:, None, None], (RET_HEADS, 1, LANES))
    w = RET_HEADS * RET_HEAD_DIM
    row = pl.BlockSpec((c, w), lambda b, j: (b * n + j, 0))
    full = lambda a: pl.BlockSpec(a.shape, lambda b, j: (0,) * a.ndim)
    return pl.pallas_call(
        _ret_kernel,
        grid=(bsz, n),
        in_specs=[row, row, row, row, full(dec), full(xi), full(zeta), full(gch), full(g_ret)],
        out_specs=row,
        out_shape=jax.ShapeDtypeStruct((t, w), BF16),
        scratch_shapes=[pltpu.VMEM((RET_HEADS, RET_HEAD_DIM, RET_HEAD_DIM), F32)],
        compiler_params=_cparams(("arbitrary", "arbitrary")),
        name="retention",
    )(qr, kr, vr, gr, dec, xi, zeta, gch, g_ret)


def _post_kernel(x_ref, onsa_ref, oret_ref, mod_ref, wo1_ref, wo2_ref, gffn_ref, wrt_ref, rb_ref,
                 wgs_ref, wus_ref, wds_ref, acc_ref, h2_ref, idx_ref, wt_ref):
    mix = _dot(onsa_ref[...], wo1_ref[...]) + _dot(oret_ref[...], wo2_ref[...])
    x1 = x_ref[...] + mod_ref[0, 2:3, :] * mix
    ms = jnp.mean(x1 * x1, axis=-1, keepdims=True)
    h2 = x1 * lax.rsqrt(ms + RMS_EPS) * gffn_ref[...] * (1.0 + mod_ref[0, 4:5, :]) + mod_ref[0, 3:4, :]
    h2_ref[...] = h2
    hb = h2.astype(BF16)
    hid = (_silu(_dot(hb, wgs_ref[...])) * _dot(hb, wus_ref[...])).astype(BF16)
    acc_ref[...] = x1 + mod_ref[0, 5:6, :] * _dot(hid, wds_ref[...])
    s = _sigmoid(_dot_nt(wrt_ref[...], h2, precision=HIGHEST))
    sb = s + rb_ref[...]
    per = N_EXPERTS // N_EXPERT_GROUPS
    ridx = lax.broadcasted_iota(I32, (per, 1), 0)
    blks, grp = [], []
    for gi in range(N_EXPERT_GROUPS):
        blk = sb[gi * per:(gi + 1) * per]
        m1 = jnp.max(blk, axis=0, keepdims=True)
        first = jnp.min(jnp.where(blk == m1, ridx, per), axis=0, keepdims=True)
        m2 = jnp.max(jnp.where(ridx == first, -jnp.inf, blk), axis=0, keepdims=True)
        blks.append(blk)
        grp.append(m1 + m2)
    masked = []
    for gi in range(N_EXPERT_GROUPS):
        rank = jnp.zeros_like(grp[gi])
        for gj in range(N_EXPERT_GROUPS):
            if gj < gi:
                rank = rank + (grp[gj] >= grp[gi]).astype(F32)
            elif gj > gi:
                rank = rank + (grp[gj] > grp[gi]).astype(F32)
        masked.append(jnp.where(rank < TOPK_GROUPS, blks[gi], NEG))
    val = jnp.concatenate(masked, axis=0)
    eidx = lax.broadcasted_iota(I32, (N_EXPERTS, 1), 0)
    ids, ws = [], []
    for _ in range(TOP_K):
        mx = jnp.max(val, axis=0, keepdims=True)
        emin = jnp.min(jnp.where(val == mx, eidx, N_EXPERTS), axis=0, keepdims=True)
        hit = eidx == emin
        ids.append(emin)
        ws.append(jnp.sum(jnp.where(hit, s, 0.0), axis=0, keepdims=True))
        val = jnp.where(hit, -jnp.inf, val)
    wsum = ws[0]
    for wk in ws[1:]:
        wsum = wsum + wk
    idx_ref[...] = jnp.concatenate(ids, axis=0)
    wt_ref[...] = jnp.concatenate(ws, axis=0) / wsum * ROUTED_SCALE


def _post(x2, onsa, oret, mod3, w_out, g_ffn, w_router, router_bias, wgs, wus, wds, seq):
    t, d = x2.shape
    tm = min(512, seq)
    hw = onsa.shape[1]
    wo1 = w_out[:hw].astype(BF16)
    wo2 = w_out[hw:].astype(BF16)
    wrt = w_router.T
    rb = jnp.broadcast_to(router_bias.reshape(N_EXPERTS, 1), (N_EXPERTS, tm))
    row = lambda n: pl.BlockSpec((tm, n), lambda i: (i, 0))
    full = lambda a: pl.BlockSpec(a.shape, lambda i: (0,) * a.ndim)
    col = pl.BlockSpec((TOP_K, tm), lambda i: (0, i))
    ops = (wo1, wo2, g_ffn, wrt, rb, wgs.astype(BF16), wus.astype(BF16), wds.astype(BF16))
    return pl.pallas_call(
        _post_kernel,
        grid=(t // tm,),
        in_specs=[row(d), row(hw), row(oret.shape[1]),
                  pl.BlockSpec((1, 6, d), lambda i: ((i * tm) // seq, 0, 0))] + [full(a) for a in ops],
        out_specs=[row(d), row(d), col, col],
        out_shape=[jax.ShapeDtypeStruct((t, d), F32), jax.ShapeDtypeStruct((t, d), F32),
                   jax.ShapeDtypeStruct((TOP_K, t), I32), jax.ShapeDtypeStruct((TOP_K, t), F32)],
        compiler_params=_cparams(("arbitrary",)),
        name="post",
    )(x2, onsa, oret, mod3, *ops)


def _route_plan(idx_t, n_tok):
    a = TOP_K * n_tok
    e_flat = idx_t.reshape(a)
    tok = jnp.tile(jnp.arange(n_tok, dtype=I32), TOP_K)
    order = jnp.argsort(e_flat, stable=True)
    e_sorted = e_flat[order]
    counts = jnp.bincount(e_flat, length=N_EXPERTS).astype(I32)
    padded = ((counts + MOE_TILE - 1) // MOE_TILE) * MOE_TILE
    ends_pad = jnp.cumsum(padded)
    starts_pad = ends_pad - padded
    starts = jnp.cumsum(counts) - counts
    dest = (starts_pad[e_sorted] + (jnp.arange(a, dtype=I32) - starts[e_sorted])).astype(I32)
    n_tiles = a // MOE_TILE + N_EXPERTS
    row_tok = jnp.zeros((n_tiles * MOE_TILE,), I32).at[dest].set(tok[order])
    pos = jnp.zeros((a,), I32).at[order].set(dest)
    n_used = (ends_pad[-1] // MOE_TILE).astype(I32)
    tile_start = jnp.minimum(jnp.arange(n_tiles, dtype=I32), n_used - 1) * MOE_TILE
    tile_e = jnp.minimum(jnp.searchsorted(ends_pad, tile_start, side='right'), N_EXPERTS - 1).astype(I32)
    first = jnp.concatenate([jnp.ones((1,), I32), (tile_e[1:] != tile_e[:-1]).astype(I32)])
    return row_tok.reshape(n_tiles, 1, MOE_TILE), pos.reshape(TOP_K, n_tok), tile_e, first, n_used.reshape(1)


def _expert_kernel(tile_e, first, n_used, tokc_ref, tokn_ref, h2_hbm, wg_ref, wu_ref, wd_ref, y_ref,
                   xbuf, wgb, wub, wdb, sem):
    i = pl.program_id(0)
    n = n_used[0]

    def issue(tok_ref, slot):
        def body(j, carry):
            row = tok_ref[0, 0, j]
            pltpu.make_async_copy(h2_hbm.at[pl.ds(row, 1), :], xbuf.at[slot, pl.ds(j, 1), :],
                                  sem.at[slot]).start()
            return carry
        lax.fori_loop(0, MOE_TILE, body, 0)

    @pl.when(i == 0)
    def _():
        issue(tokc_ref, 0)

    @pl.when(i + 1 < n)
    def _():
        issue(tokn_ref, (i + 1) % 2)

    @pl.when(i < n)
    def _():
        slot = i % 2
        pltpu.make_async_copy(h2_hbm.at[pl.ds(0, MOE_TILE), :], xbuf.at[slot], sem.at[slot]).wait()

        @pl.when(first[i] == 1)
        def _():
            wgb[...] = wg_ref[0].astype(BF16)
            wub[...] = wu_ref[0].astype(BF16)
            wdb[...] = wd_ref[0].astype(BF16)

        xb = xbuf[slot].astype(BF16)
        hid = (_silu(_dot(xb, wgb[...])) * _dot(xb, wub[...])).astype(BF16)
        y_ref[...] = _dot(hid, wdb[...])

    @pl.when(i >= n)
    def _():
        y_ref[...] = jnp.zeros_like(y_ref)


def _experts(h2, row_tok, tile_e, first, n_used, wg, wu, wd):
    t, d = h2.shape
    n_tiles = row_tok.shape[0]
    de = wg.shape[2]
    grid_spec = pltpu.PrefetchScalarGridSpec(
        num_scalar_prefetch=3,
        grid=(n_tiles,),
        in_specs=[pl.BlockSpec((1, 1, MOE_TILE), lambda i, te, fi, nu: (i, 0, 0), memory_space=pltpu.SMEM),
                  pl.BlockSpec((1, 1, MOE_TILE), lambda i, te, fi, nu: (jnp.minimum(i + 1, n_tiles - 1), 0, 0),
                               memory_space=pltpu.SMEM),
                  pl.BlockSpec(memory_space=pl.ANY),
                  pl.BlockSpec((1, d, de), lambda i, te, fi, nu: (te[i], 0, 0)),
                  pl.BlockSpec((1, d, de), lambda i, te, fi, nu: (te[i], 0, 0)),
                  pl.BlockSpec((1, de, d), lambda i, te, fi, nu: (te[i], 0, 0))],
        out_specs=pl.BlockSpec((MOE_TILE, d), lambda i, te, fi, nu: (i, 0)),
        scratch_shapes=[pltpu.VMEM((2, MOE_TILE, d), F32),
                        pltpu.VMEM((d, de), BF16), pltpu.VMEM((d, de), BF16), pltpu.VMEM((de, d), BF16),
                        pltpu.SemaphoreType.DMA((2,))])
    return pl.pallas_call(
        _expert_kernel,
        grid_spec=grid_spec,
        out_shape=jax.ShapeDtypeStruct((n_tiles * MOE_TILE, d), F32),
        compiler_params=_cparams(("arbitrary",)),
        name="experts",
    )(tile_e, first, n_used, row_tok, row_tok, h2, wg, wu, wd)


def _combine_kernel(posc_ref, posn_ref, ys_hbm, acc_ref, w_ref, mod_ref, g_ref, o_ref, buf, sem):
    i = pl.program_id(0)
    n = pl.num_programs(0)
    tc = acc_ref.shape[0]

    def issue(pos_ref, slot):
        for k in range(TOP_K):
            def body(j, carry):
                row = pos_ref[0, k, j]
                pltpu.make_async_copy(ys_hbm.at[pl.ds(row, 1), :], buf.at[slot, k, pl.ds(j, 1), :],
                                      sem.at[slot]).start()
                return carry
            lax.fori_loop(0, tc, body, 0)

    @pl.when(i == 0)
    def _():
        issue(posc_ref, 0)

    @pl.when(i + 1 < n)
    def _():
        issue(posn_ref, (i + 1) % 2)

    slot = i % 2
    for k in range(TOP_K):
        pltpu.make_async_copy(ys_hbm.at[pl.ds(0, tc), :], buf.at[slot, k], sem.at[slot]).wait()
    w = w_ref[...]
    routed = w[:, 0:1] * buf[slot, 0]
    for k in range(1, TOP_K):
        routed = routed + w[:, k:k + 1] * buf[slot, k]
    x2 = acc_ref[...] + mod_ref[0, 5:6, :] * routed
    o_ref[...] = x2 * lax.rsqrt(jnp.mean(x2 * x2, axis=-1, keepdims=True) + RMS_EPS) * g_ref[...]


def _combine(ys, pos_t, acc0, w_tok, mod3, g_final, seq):
    t, d = acc0.shape
    tc = 128
    nt = t // tc
    pos3 = pos_t.reshape(TOP_K, nt, tc).transpose(1, 0, 2)
    return pl.pallas_call(
        _combine_kernel,
        grid=(nt,),
        in_specs=[pl.BlockSpec((1, TOP_K, tc), lambda i: (i, 0, 0), memory_space=pltpu.SMEM),
                  pl.BlockSpec((1, TOP_K, tc), lambda i: (jnp.minimum(i + 1, nt - 1), 0, 0),
                               memory_space=pltpu.SMEM),
                  pl.BlockSpec(memory_space=pl.ANY),
                  pl.BlockSpec((tc, d), lambda i: (i, 0)),
                  pl.BlockSpec((tc, TOP_K), lambda i: (i, 0)),
                  pl.BlockSpec((1, 6, d), lambda i: ((i * tc) // seq, 0, 0)),
                  pl.BlockSpec((1, d), lambda i: (0, 0))],
        out_specs=pl.BlockSpec((tc, d), lambda i: (i, 0)),
        out_shape=jax.ShapeDtypeStruct((t, d), F32),
        scratch_shapes=[pltpu.VMEM((2, TOP_K, tc, d), F32), pltpu.SemaphoreType.DMA((2,))],
        compiler_params=_cparams(("arbitrary",)),
        name="combine",
    )(pos3, pos3, ys, acc0, w_tok, mod3, g_final)


def kernel(x, c, positions, w_ada, b_ada, g_norm_mix, w_in, cmp_pos_k, cmp_w1_k, cmp_w2_k, cmp_pos_v,
           cmp_w1_v, cmp_w2_v, g_nsa_out, g_ret_out, w_out, g_norm_ffn, w_router, router_bias,
           w_gate_e, w_up_e, w_down_e, w_gate_s, w_up_s, w_down_s, g_norm_final):
    bsz, seq, d = x.shape
    t = bsz * seq
    x2 = x.reshape(t, d)
    cos_t, sin_t = _rope_tables(positions.reshape(t, 1).astype(F32))
    n_cmp_pad = seq // CMP_STRIDE

    def cmp_rows(tab):
        rows = tab[:, :LANES].reshape(bsz, seq, LANES)[:, CMP_BLOCK - 1::CMP_STRIDE]
        return jnp.pad(rows, ((0, 0), (0, n_cmp_pad - rows.shape[1]), (0, 0)))

    cos_c, sin_c = cmp_rows(cos_t), cmp_rows(sin_t)
    for l in range(w_in.shape[0]):
        mod3 = _ada(c, w_ada[l], b_ada[l]).reshape(bsz, 6, d)
        q, kc, vc, ksw, gates, qr, kr, vr, gr = _inproj(
            x2, mod3, g_norm_mix[l].reshape(1, d), _arrange_w_in(w_in[l]), cos_t, sin_t, seq)
        kcmp, vcmp = _compress(kc, vc, cos_c, sin_c,
                               _arrange_cmp_weights(cmp_pos_k[l], cmp_w1_k[l], cmp_w2_k[l]),
                               _arrange_cmp_weights(cmp_pos_v[l], cmp_w1_v[l], cmp_w2_v[l]), bsz, seq)
        onsa = _nsa(q, kcmp, vcmp, ksw, gates, g_nsa_out[l].reshape(1, -1), bsz, seq)
        oret = _retention(qr, kr, vr, gr, g_ret_out[l], bsz, seq)
        acc0, h2, idx_t, w_t = _post(x2, onsa, oret, mod3, w_out[l], g_norm_ffn[l].reshape(1, d),
                                     w_router[l], router_bias[l], w_gate_s[l], w_up_s[l], w_down_s[l], seq)
        row_tok, pos_t, tile_e, first, n_used = _route_plan(idx_t, t)
        ys = _experts(h2, row_tok, tile_e, first, n_used, w_gate_e[l], w_up_e[l], w_down_e[l])
        last = l == w_in.shape[0] - 1
        gfin = g_norm_final.reshape(1, d)
        x2 = _combine(ys, pos_t, acc0, w_t.T, mod3, gfin, seq)
        assert last, "final norm is fused into the combine stage; depth 1 only"
    return x2.reshape(bsz, seq, d)
```

```python
import functools
import math

import numpy as np
import jax
import jax.numpy as jnp
from jax import lax
from jax.experimental import pallas as pl
from jax.experimental.pallas import tpu as pltpu

F32 = jnp.float32
BF16 = jnp.bfloat16
I32 = jnp.int32
HIGHEST = lax.Precision.HIGHEST

LANES = 128
SUBLANES = 8
NSA_HEAD_DIM = 64
NSA_HEADS = 8
NSA_GROUPS = 2
NSA_Q_PER_KV = NSA_HEADS // NSA_GROUPS
CMP_BLOCK = 32
CMP_STRIDE = 16
CMP_HIDDEN = 128
SEL_BLOCK = 64
SEL_TOP = 8
WINDOW = 512
Q_BLOCK = 128
FORCE_BONUS = 1.0e4
RET_HEADS = 4
RET_HEAD_DIM = 128
RET_CHUNK = 128
ROPE_THETA = 10000.0
RMS_EPS = 1e-6
N_EXPERTS = 256
N_EXPERT_GROUPS = 8
TOPK_GROUPS = 4
TOP_K = 8
ROUTED_SCALE = 2.5
MOE_TILE = 256
SEL_CHUNK = 256
NEG = -1e30
VMEM_LIMIT = 48 * 1024 * 1024


def _cparams(sem):
    return pltpu.CompilerParams(dimension_semantics=sem, vmem_limit_bytes=VMEM_LIMIT)


def _dot(a, b, **kw):
    return jnp.dot(a, b, preferred_element_type=F32, **kw)


def _dot_nt(a, b, **kw):
    return lax.dot_general(a, b, (((1,), (1,)), ((), ())), preferred_element_type=F32, **kw)


def _sigmoid(v):
    return 1.0 / (1.0 + jnp.exp(-v))


def _silu(v):
    return v * _sigmoid(v)


def _ada_kernel(c_ref, w_ref, b_ref, o_ref):
    o_ref[...] = _dot(_silu(c_ref[...]), w_ref[...], precision=HIGHEST) + b_ref[...]


def _ada(c, w, b):
    bsz, d = c.shape
    n = w.shape[1]
    tn = 1536
    cp = jnp.zeros((8, d), F32).at[:bsz].set(c)
    out = pl.pallas_call(
        _ada_kernel,
        grid=(n // tn,),
        in_specs=[pl.BlockSpec((8, d), lambda j: (0, 0)),
                  pl.BlockSpec((d, tn), lambda j: (0, j)),
                  pl.BlockSpec((1, tn), lambda j: (0, j))],
        out_specs=pl.BlockSpec((8, tn), lambda j: (0, j)),
        out_shape=jax.ShapeDtypeStruct((8, n), F32),
        compiler_params=_cparams(("arbitrary",)),
        name="ada",
    )(cp, w, b.reshape(1, n))
    return out[:bsz]


def _rope_table_kernel(pos_ref, inv_ref, sgn_ref, cos_ref, sin_ref):
    ang = pos_ref[...] * inv_ref[...]
    cos_ref[...] = jnp.cos(ang)
    sin_ref[...] = jnp.sin(ang) * sgn_ref[...]


def _rope_tables(posf):
    t = posf.shape[0]
    inv_n = ROPE_THETA ** (-jnp.arange(0, NSA_HEAD_DIM, 2, dtype=F32) / NSA_HEAD_DIM)
    inv_r = ROPE_THETA ** (-jnp.arange(0, RET_HEAD_DIM, 2, dtype=F32) / RET_HEAD_DIM)
    inv = jnp.concatenate([jnp.tile(inv_n, 4), jnp.tile(inv_r, 2)]).reshape(1, 2 * LANES)
    sgn_n = np.where((np.arange(LANES) % NSA_HEAD_DIM) < NSA_HEAD_DIM // 2, -1.0, 1.0)
    sgn_r = np.where(np.arange(LANES) < RET_HEAD_DIM // 2, -1.0, 1.0)
    sgn = jnp.asarray(np.concatenate([sgn_n, sgn_r]).reshape(1, 2 * LANES), F32)
    tm = min(t, 1024)
    return pl.pallas_call(
        _rope_table_kernel,
        grid=(t // tm,),
        in_specs=[pl.BlockSpec((tm, 1), lambda i: (i, 0)),
                  pl.BlockSpec((1, 2 * LANES), lambda i: (0, 0)),
                  pl.BlockSpec((1, 2 * LANES), lambda i: (0, 0))],
        out_specs=[pl.BlockSpec((tm, 2 * LANES), lambda i: (i, 0))] * 2,
        out_shape=[jax.ShapeDtypeStruct((t, 2 * LANES), F32)] * 2,
        compiler_params=_cparams(("arbitrary",)),
        name="rope_tables",
    )(posf, inv, sgn)


def _rope64(p, cos, sin, first_half):
    rot = jnp.where(first_half, pltpu.roll(p, 96, 1), pltpu.roll(p, 32, 1))
    return p * cos + rot * sin


def _rope128(p, cos, sin):
    return p * cos + pltpu.roll(p, 64, 1) * sin


_C_Q = 0
_C_KV = _C_Q + NSA_HEADS * LANES
_C_GATE = _C_KV + 6 * LANES
_C_RET = _C_GATE + LANES
_C_END = _C_RET + 4 * RET_HEADS * RET_HEAD_DIM


def _arrange_w_in(w_in):
    d = w_in.shape[0]
    nw = NSA_HEADS * NSA_HEAD_DIM
    q = w_in[:, :nw].reshape(d, NSA_HEADS, NSA_HEAD_DIM)
    z = jnp.zeros_like(q)
    grp = (jnp.arange(NSA_HEADS) // NSA_Q_PER_KV)[None, :, None]
    qpad = jnp.where(grp == 0, jnp.concatenate([q, z], -1), jnp.concatenate([z, q], -1))
    qpad = qpad.reshape(d, NSA_HEADS * LANES)
    kv = w_in[:, nw:nw + 6 * LANES]
    g0 = nw + 6 * LANES
    gate = jnp.pad(w_in[:, g0:g0 + 3 * NSA_HEADS], ((0, 0), (0, LANES - 3 * NSA_HEADS)))
    ret = w_in[:, g0 + 3 * NSA_HEADS:]
    return jnp.concatenate([qpad, kv, gate, ret], axis=1).astype(BF16)


def _inproj_kernel(x_ref, mod_ref, g_ref, w_ref, cos_ref, sin_ref,
                   q_ref, kc_ref, vc_ref, ksw_ref, gate_ref, qr_ref, kr_ref, vr_ref, gr_ref):
    x = x_ref[...]
    tm = x.shape[0]
    ms = jnp.mean(x * x, axis=-1, keepdims=True)
    y = x * lax.rsqrt(ms + RMS_EPS) * g_ref[...]
    h = y * (1.0 + mod_ref[0, 1:2, :]) + mod_ref[0, 0:1, :]
    hb = h.astype(BF16)
    cos_n, sin_n = cos_ref[:, 0:LANES], sin_ref[:, 0:LANES]
    cos_r, sin_r = cos_ref[:, LANES:], sin_ref[:, LANES:]
    lane = lax.broadcasted_iota(I32, (tm, LANES), 1)
    first_half = (lane % NSA_HEAD_DIM) < (NSA_HEAD_DIM // 2)
    scale_n = NSA_HEAD_DIM ** -0.5
    scale_r = RET_HEAD_DIM ** -0.5

    def proj(c0, n):
        return _dot(hb, w_ref[:, c0:c0 + n])

    for hh in range(NSA_HEADS):
        p = proj(_C_Q + hh * LANES, LANES)
        q_ref[:, hh * LANES:(hh + 1) * LANES] = (_rope64(p, cos_n, sin_n, first_half) * scale_n).astype(BF16)
    kv = proj(_C_KV, 6 * LANES)
    kc_ref[...] = kv[:, 0:LANES].astype(BF16)
    vc_ref[...] = kv[:, LANES:2 * LANES].astype(BF16)
    ksw_ref[:, 0:LANES] = _rope64(kv[:, 2 * LANES:3 * LANES], cos_n, sin_n, first_half).astype(BF16)
    ksw_ref[:, LANES:2 * LANES] = kv[:, 3 * LANES:4 * LANES].astype(BF16)
    ksw_ref[:, 2 * LANES:3 * LANES] = _rope64(kv[:, 4 * LANES:5 * LANES], cos_n, sin_n, first_half).astype(BF16)
    ksw_ref[:, 3 * LANES:4 * LANES] = kv[:, 5 * LANES:6 * LANES].astype(BF16)
    gate_ref[...] = _sigmoid(proj(_C_GATE, LANES))
    rw = RET_HEADS * RET_HEAD_DIM
    for hh in range(RET_HEADS):
        sl = slice(hh * LANES, (hh + 1) * LANES)
        pq = proj(_C_RET + hh * LANES, LANES)
        qr_ref[:, sl] = _rope128(pq, cos_r, sin_r).astype(BF16)
        pk = proj(_C_RET + rw + hh * LANES, LANES)
        kr_ref[:, sl] = (_rope128(pk, cos_r, sin_r) * scale_r).astype(BF16)
    vr_ref[...] = proj(_C_RET + 2 * rw, rw).astype(BF16)
    gr_ref[...] = proj(_C_RET + 3 * rw, rw)


def _inproj(x2, mod3, g_mix, w_in_p, cos_t, sin_t, seq):
    t, d = x2.shape
    tm = min(512, seq)
    rw = RET_HEADS * RET_HEAD_DIM
    row = lambda n: pl.BlockSpec((tm, n), lambda i: (i, 0))
    outs = [(NSA_HEADS * LANES, BF16), (LANES, BF16), (LANES, BF16), (4 * LANES, BF16), (LANES, F32),
            (rw, BF16), (rw, BF16), (rw, BF16), (rw, F32)]
    return pl.pallas_call(
        _inproj_kernel,
        grid=(t // tm,),
        in_specs=[row(d),
                  pl.BlockSpec((1, 6, d), lambda i: ((i * tm) // seq, 0, 0)),
                  pl.BlockSpec((1, d), lambda i: (0, 0)),
                  pl.BlockSpec((d, _C_END), lambda i: (0, 0)),
                  row(2 * LANES), row(2 * LANES)],
        out_specs=[row(n) for n, _ in outs],
        out_shape=[jax.ShapeDtypeStruct((t, n), dt) for n, dt in outs],
        compiler_params=_cparams(("arbitrary",)),
        name="inproj",
    )(x2, mod3, g_mix, w_in_p, cos_t, sin_t)


def _gelu_tanh(v):
    return v * (0.5 * (1.0 + jnp.tanh(math.sqrt(2.0 / math.pi) * (v + 0.044715 * (v * v * v)))))


def _compress_kernel(kc_ref, vc_ref, wkt_ref, wkb_ref, wk2_ref, pk_ref, wvt_ref, wvb_ref, wv2_ref, pv_ref,
                     cos_ref, sin_ref, ko_ref, vo_ref):
    def one(x_ref, wt_ref, wb_ref, w2_ref, p_ref):
        xx = x_ref[0]
        a = _dot(xx, wt_ref[...])
        b = _dot(xx, wb_ref[...])
        pb = p_ref[...].astype(BF16)
        bias = _dot(pb, wt_ref[...])[0:1] + _dot(pb, wb_ref[...])[1:2]
        hid = a + pltpu.roll(b, b.shape[0] - 1, 0) + bias
        return _dot(_gelu_tanh(hid).astype(BF16), w2_ref[...])

    k = one(kc_ref, wkt_ref, wkb_ref, wk2_ref, pk_ref)
    lane = lax.broadcasted_iota(I32, k.shape, 1)
    first_half = (lane % NSA_HEAD_DIM) < (NSA_HEAD_DIM // 2)
    ko_ref[0] = _rope64(k, cos_ref[0], sin_ref[0], first_half).astype(BF16)
    vo_ref[0] = one(vc_ref, wvt_ref, wvb_ref, wv2_ref, pv_ref).astype(BF16)


def _arrange_cmp_weights(pos, w1, w2):
    half = CMP_BLOCK // 2
    dh = NSA_HEAD_DIM
    w1r = w1.reshape(CMP_BLOCK, dh, CMP_HIDDEN)

    def block(wpart):
        z = jnp.zeros_like(wpart)
        g0 = jnp.concatenate([wpart, z], axis=-1)
        g1 = jnp.concatenate([z, wpart], axis=-1)
        return jnp.stack([g0, g1], axis=1).reshape(half * 2 * dh, 2 * CMP_HIDDEN)

    wt, wb = block(w1r[:half]), block(w1r[half:])
    z2 = jnp.zeros_like(w2)
    w2b = jnp.concatenate([jnp.concatenate([w2, z2], 1), jnp.concatenate([z2, w2], 1)], 0)
    ptop = jnp.tile(pos[:half], (1, 2)).reshape(1, -1)
    pbot = jnp.tile(pos[half:], (1, 2)).reshape(1, -1)
    prow = jnp.concatenate([ptop, pbot, jnp.zeros((6, ptop.shape[1]), F32)], 0)
    return wt.astype(BF16), wb.astype(BF16), w2b.astype(BF16), prow


def _compress(kc, vc, cos_c, sin_c, kparams, vparams, bsz, seq):
    nchunk = seq // CMP_STRIDE
    width = CMP_STRIDE * LANES
    kc16 = kc.reshape(bsz, nchunk, width)
    vc16 = vc.reshape(bsz, nchunk, width)
    full = lambda a: pl.BlockSpec(a.shape, lambda b: (0,) * a.ndim)
    per_b = lambda n: pl.BlockSpec((1, nchunk, n), lambda b: (b, 0, 0))
    return pl.pallas_call(
        _compress_kernel,
        grid=(bsz,),
        in_specs=[per_b(width), per_b(width)] + [full(a) for a in kparams] + [full(a) for a in vparams]
                 + [per_b(LANES), per_b(LANES)],
        out_specs=[per_b(LANES), per_b(LANES)],
        out_shape=[jax.ShapeDtypeStruct((bsz, nchunk, LANES), BF16)] * 2,
        compiler_params=_cparams(("arbitrary",)),
        name="compress",
    )(kc16, vc16, *kparams, *vparams, cos_c, sin_c)


def _online_update(carry, s, mask, v):
    m, l, acc = carry
    s = jnp.where(mask, s, NEG)
    m_new = jnp.maximum(m, jnp.max(s, axis=-1, keepdims=True))
    alpha = jnp.exp(m - m_new)
    e = jnp.where(mask, jnp.exp(s - m_new), 0.0)
    l = alpha * l + jnp.sum(e, axis=-1, keepdims=True)
    acc = alpha * acc + _dot(e.astype(BF16), v)
    return m_new, l, acc


def _nsa_kernel(q_ref, kc_ref, vc_ref, ks_ref, vs_ref, kw_ref, vw_ref, gate_ref, ovt_ref, exp_ref, g_ref,
                o_ref, *, seq):
    qb = pl.program_id(1)
    t0 = qb * Q_BLOCK
    rows = NSA_Q_PER_KV * Q_BLOCK
    n_cmp_pad = kc_ref.shape[1]
    n_sel = seq // SEL_BLOCK
    t_col = t0 + lax.broadcasted_iota(I32, (rows, 1), 0) % Q_BLOCK
    t_blk = t0 + lax.broadcasted_iota(I32, (Q_BLOCK, 1), 0)
    t_row = t0 + lax.broadcasted_iota(I32, (1, Q_BLOCK), 1)
    gates = gate_ref[...]
    lane = lax.broadcasted_iota(I32, (Q_BLOCK, LANES), 1)
    heads_out = []
    for g in range(NSA_GROUPS):
        q4 = jnp.concatenate([q_ref[:, (NSA_Q_PER_KV * g + r) * LANES:(NSA_Q_PER_KV * g + r + 1) * LANES]
                              for r in range(NSA_Q_PER_KV)], axis=0)
        s_c = _dot_nt(q4, kc_ref[0])
        cend = lax.broadcasted_iota(I32, (1, n_cmp_pad), 1) * CMP_STRIDE + (CMP_BLOCK - 1)
        mask_c = cend <= t_col
        s_c = jnp.where(mask_c, s_c, NEG)
        e_c = jnp.where(mask_c, jnp.exp(s_c - jnp.max(s_c, axis=-1, keepdims=True)), 0.0)
        p_c = e_c / jnp.maximum(jnp.sum(e_c, axis=-1, keepdims=True), 1e-20)
        o_c = _dot(p_c.astype(BF16), vc_ref[0])
        psum = p_c[0:Q_BLOCK]
        for r in range(1, NSA_Q_PER_KV):
            psum = psum + p_c[r * Q_BLOCK:(r + 1) * Q_BLOCK]
        imp_t = _dot_nt(ovt_ref[...], psum, precision=HIGHEST)
        nsp = imp_t.shape[0]
        jrow = lax.broadcasted_iota(I32, (nsp, 1), 0)
        cur = t_row // SEL_BLOCK
        valid = (jrow * SEL_BLOCK <= t_row) & (jrow < n_sel)
        forced = (jrow == 0) | (jrow == cur) | (jrow == cur - 1)
        val = jnp.where(valid, imp_t + jnp.where(forced, FORCE_BONUS, 0.0), -1.0)
        val = jnp.where(jrow < n_sel, val, -jnp.inf)
        sel_t = jnp.zeros((nsp, Q_BLOCK), F32)
        for _ in range(min(SEL_TOP, n_sel)):
            mx = jnp.max(val, axis=0, keepdims=True)
            jmin = jnp.min(jnp.where(val == mx, jrow, nsp), axis=0, keepdims=True)
            hit = jrow == jmin
            sel_t = jnp.where(hit, 1.0, sel_t)
            val = jnp.where(hit, -jnp.inf, val)
        if nsp < LANES:
            sel_t = jnp.concatenate([sel_t, jnp.zeros((LANES - nsp, Q_BLOCK), F32)], axis=0)
        sel_b = sel_t.T.astype(BF16)
        n_chunks = (t0 + Q_BLOCK + SEL_CHUNK - 1) // SEL_CHUNK

        def sel_step(c, carry):
            k0 = pl.multiple_of(c * SEL_CHUNK, SEL_CHUNK)
            s = _dot_nt(q4, ks_ref[pl.ds(k0, SEL_CHUNK), :])
            picked = _dot(sel_b, exp_ref[c])
            kpos = k0 + lax.broadcasted_iota(I32, (1, SEL_CHUNK), 1)
            m1 = (picked > 0.5) & (kpos <= t_blk)
            mask = jnp.concatenate([m1] * NSA_Q_PER_KV, axis=0)
            return _online_update(carry, s, mask, vs_ref[pl.ds(k0, SEL_CHUNK), :])

        init = (jnp.full((rows, 1), NEG, F32), jnp.zeros((rows, 1), F32), jnp.zeros((rows, LANES), F32))
        _, l_s, acc_s = lax.fori_loop(0, n_chunks, sel_step, init)
        o_s = acc_s / jnp.maximum(l_s, 1e-20)
        carry = init
        for w in range(WINDOW // Q_BLOCK + 1):
            wb = qb - WINDOW // Q_BLOCK + w
            k0 = pl.multiple_of(jnp.maximum(wb, 0) * Q_BLOCK, Q_BLOCK)
            s = _dot_nt(q4, kw_ref[pl.ds(k0, Q_BLOCK), :])
            spos = wb * Q_BLOCK + lax.broadcasted_iota(I32, (1, Q_BLOCK), 1)
            mask = (spos <= t_col) & (spos > t_col - WINDOW) & (spos >= 0)
            carry = _online_update(carry, s, mask, vw_ref[pl.ds(k0, Q_BLOCK), :])
        o_w = carry[2] / jnp.maximum(carry[1], 1e-20)
        for r in range(NSA_Q_PER_KV):
            hh = NSA_Q_PER_KV * g + r
            rs = slice(r * Q_BLOCK, (r + 1) * Q_BLOCK)
            o = (gates[:, 3 * hh:3 * hh + 1] * o_c[rs] + gates[:, 3 * hh + 1:3 * hh + 2] * o_s[rs]
                 + gates[:, 3 * hh + 2:3 * hh + 3] * o_w[rs])
            if (hh % 2) != g:
                o = pltpu.roll(o, NSA_HEAD_DIM, 1)
            heads_out.append(o)
    blocks = [jnp.where(lane < NSA_HEAD_DIM, heads_out[2 * i], heads_out[2 * i + 1])
              for i in range(NSA_HEADS // 2)]
    ss = sum(jnp.sum(b * b, axis=-1, keepdims=True) for b in blocks)
    inv = lax.rsqrt(ss / (NSA_HEADS * NSA_HEAD_DIM) + RMS_EPS)
    for i, b in enumerate(blocks):
        sl = slice(i * LANES, (i + 1) * LANES)
        o_ref[:, sl] = (b * inv * g_ref[:, sl]).astype(BF16)


def _nsa(q, kcmp, vcmp, ksw, gates, g_nsa, bsz, seq):
    t = bsz * seq
    nq = seq // Q_BLOCK
    n_cmp_pad = seq // CMP_STRIDE
    n_cmp = (seq - CMP_BLOCK) // CMP_STRIDE + 1
    n_sel = seq // SEL_BLOCK
    nsp = max(n_sel, 8)
    cs = np.arange(n_cmp_pad) * CMP_STRIDE
    ss = np.arange(nsp) * SEL_BLOCK
    ov = ((cs[None, :] < ss[:, None] + SEL_BLOCK) & (cs[None, :] + CMP_BLOCK > ss[:, None])
          & (np.arange(n_cmp_pad)[None, :] < n_cmp) & (np.arange(nsp)[:, None] < n_sel))
    ovt = jnp.asarray(ov, F32)
    nch = seq // SEL_CHUNK
    kp = np.arange(seq).reshape(nch, 1, SEL_CHUNK)
    expand = jnp.asarray(kp // SEL_BLOCK == np.arange(LANES).reshape(1, LANES, 1), BF16)
    seqcol = lambda c: pl.BlockSpec((seq, LANES), lambda b, i: (b, c))
    return pl.pallas_call(
        functools.partial(_nsa_kernel, seq=seq),
        grid=(bsz, nq),
        in_specs=[pl.BlockSpec((Q_BLOCK, NSA_HEADS * LANES), lambda b, i: (b * nq + i, 0)),
                  pl.BlockSpec((1, n_cmp_pad, LANES), lambda b, i: (b, 0, 0)),
                  pl.BlockSpec((1, n_cmp_pad, LANES), lambda b, i: (b, 0, 0)),
                  seqcol(0), seqcol(1), seqcol(2), seqcol(3),
                  pl.BlockSpec((Q_BLOCK, LANES), lambda b, i: (b * nq + i, 0)),
                  pl.BlockSpec(ovt.shape, lambda b, i: (0, 0)),
                  pl.BlockSpec(expand.shape, lambda b, i: (0, 0, 0)),
                  pl.BlockSpec((1, NSA_HEADS * NSA_HEAD_DIM), lambda b, i: (0, 0))],
        out_specs=pl.BlockSpec((Q_BLOCK, NSA_HEADS * NSA_HEAD_DIM), lambda b, i: (b * nq + i, 0)),
        out_shape=jax.ShapeDtypeStruct((t, NSA_HEADS * NSA_HEAD_DIM), BF16),
        compiler_params=_cparams(("arbitrary", "arbitrary")),
        name="nsa",
    )(q, kcmp, vcmp, ksw, ksw, ksw, ksw, gates, ovt, expand, g_nsa)


def _ret_kernel(q_ref, k_ref, v_ref, gr_ref, dec_ref, xi_ref, zeta_ref, gch_ref, g_ref, o_ref, st_ref):
    @pl.when(pl.program_id(1) == 0)
    def _():
        st_ref[...] = jnp.zeros_like(st_ref)

    for hh in range(RET_HEADS):
        sl = slice(hh * LANES, (hh + 1) * LANES)
        q, k, v = q_ref[:, sl], k_ref[:, sl], v_ref[:, sl]
        sc = _dot_nt(q, k) * dec_ref[hh]
        inner = _dot(sc.astype(BF16), v)
        st = st_ref[hh]
        cross = _dot((q.astype(F32) * xi_ref[hh]).astype(BF16), st.astype(BF16))
        kz = (k.astype(F32) * zeta_ref[hh]).T.astype(BF16)
        st_ref[hh] = st * gch_ref[hh] + _dot(kz, v)
        o = inner + cross
        y = o * lax.rsqrt(jnp.mean(o * o, axis=-1, keepdims=True) + RMS_EPS) * g_ref[hh:hh + 1, :]
        o_ref[:, sl] = (_silu(gr_ref[:, sl]) * y).astype(BF16)


def _retention(qr, kr, vr, gr, g_ret, bsz, seq):
    t = bsz * seq
    c = RET_CHUNK
    n = seq // c
    log_g = jnp.log(1.0 - 2.0 ** (-5.0 - jnp.arange(RET_HEADS, dtype=F32)))
    i = jnp.arange(c, dtype=F32)
    diff = i[:, None] - i[None, :]
    causal = diff >= 0
    dec = jnp.where(causal, jnp.exp(log_g[:, None, None] * jnp.where(causal, diff, 0.0)), 0.0)
    xi = jnp.broadcast_to(jnp.exp(log_g[:, None] * (i + 1.0))[:, :, None], (RET_HEADS, c, LANES))
    zeta = jnp.broadcast_to(jnp.exp(log_g[:, None] * (c - 1.0 - i))[:, :, None], (RET_HEADS, c, LANES))
    gch = jnp.broadcast_to(jnp.exp(log_g * c)[:, None, None], (RET_HEADS, 1, LANES))
    w = RET_HEADS * RET_HEAD_DIM
    row = pl.BlockSpec((c, w), lambda b, j: (b * n + j, 0))
    full = lambda a: pl.BlockSpec(a.shape, lambda b, j: (0,) * a.ndim)
    return pl.pallas_call(
        _ret_kernel,
        grid=(bsz, n),
        in_specs=[row, row, row, row, full(dec), full(xi), full(zeta), full(gch), full(g_ret)],
        out_specs=row,
        out_shape=jax.ShapeDtypeStruct((t, w), BF16),
        scratch_shapes=[pltpu.VMEM((RET_HEADS, RET_HEAD_DIM, RET_HEAD_DIM), F32)],
        compiler_params=_cparams(("arbitrary", "arbitrary")),
        name="retention",
    )(qr, kr, vr, gr, dec, xi, zeta, gch, g_ret)


def _post_kernel(x_ref, onsa_ref, oret_ref, mod_ref, wo1_ref, wo2_ref, gffn_ref, wrt_ref, rb_ref,
                 wgs_ref, wus_ref, wds_ref, acc_ref, h2_ref, idx_ref, wt_ref):
    mix = _dot(onsa_ref[...], wo1_ref[...]) + _dot(oret_ref[...], wo2_ref[...])
    x1 = x_ref[...] + mod_ref[0, 2:3, :] * mix
    ms = jnp.mean(x1 * x1, axis=-1, keepdims=True)
    h2 = x1 * lax.rsqrt(ms + RMS_EPS) * gffn_ref[...] * (1.0 + mod_ref[0, 4:5, :]) + mod_ref[0, 3:4, :]
    h2_ref[...] = h2
    hb = h2.astype(BF16)
    hid = (_silu(_dot(hb, wgs_ref[...])) * _dot(hb, wus_ref[...])).astype(BF16)
    acc_ref[...] = x1 + mod_ref[0, 5:6, :] * _dot(hid, wds_ref[...])
    s = _sigmoid(_dot_nt(wrt_ref[...], h2, precision=HIGHEST))
    sb = s + rb_ref[...]
    per = N_EXPERTS // N_EXPERT_GROUPS
    ridx = lax.broadcasted_iota(I32, (per, 1), 0)
    blks, grp = [], []
    for gi in range(N_EXPERT_GROUPS):
        blk = sb[gi * per:(gi + 1) * per]
        m1 = jnp.max(blk, axis=0, keepdims=True)
        first = jnp.min(jnp.where(blk == m1, ridx, per), axis=0, keepdims=True)
        m2 = jnp.max(jnp.where(ridx == first, -jnp.inf, blk), axis=0, keepdims=True)
        blks.append(blk)
        grp.append(m1 + m2)
    masked = []
    for gi in range(N_EXPERT_GROUPS):
        rank = jnp.zeros_like(grp[gi])
        for gj in range(N_EXPERT_GROUPS):
            if gj < gi:
                rank = rank + (grp[gj] >= grp[gi]).astype(F32)
            elif gj > gi:
                rank = rank + (grp[gj] > grp[gi]).astype(F32)
        masked.append(jnp.where(rank < TOPK_GROUPS, blks[gi], NEG))
    val = jnp.concatenate(masked, axis=0)
    eidx = lax.broadcasted_iota(I32, (N_EXPERTS, 1), 0)
    ids, ws = [], []
    for _ in range(TOP_K):
        mx = jnp.max(val, axis=0, keepdims=True)
        emin = jnp.min(jnp.where(val == mx, eidx, N_EXPERTS), axis=0, keepdims=True)
        hit = eidx == emin
        ids.append(emin)
        ws.append(jnp.sum(jnp.where(hit, s, 0.0), axis=0, keepdims=True))
        val = jnp.where(hit, -jnp.inf, val)
    wsum = ws[0]
    for wk in ws[1:]:
        wsum = wsum + wk
    idx_ref[...] = jnp.concatenate(ids, axis=0)
    wt_ref[...] = jnp.concatenate(ws, axis=0) / wsum * ROUTED_SCALE


def _post(x2, onsa, oret, mod3, w_out, g_ffn, w_router, router_bias, wgs, wus, wds, seq):
    t, d = x2.shape
    tm = min(512, seq)
    hw = onsa.shape[1]
    wo1 = w_out[:hw].astype(BF16)
    wo2 = w_out[hw:].astype(BF16)
    wrt = w_router.T
    rb = jnp.broadcast_to(router_bias.reshape(N_EXPERTS, 1), (N_EXPERTS, tm))
    row = lambda n: pl.BlockSpec((tm, n), lambda i: (i, 0))
    full = lambda a: pl.BlockSpec(a.shape, lambda i: (0,) * a.ndim)
    col = pl.BlockSpec((TOP_K, tm), lambda i: (0, i))
    ops = (wo1, wo2, g_ffn, wrt, rb, wgs.astype(BF16), wus.astype(BF16), wds.astype(BF16))
    return pl.pallas_call(
        _post_kernel,
        grid=(t // tm,),
        in_specs=[row(d), row(hw), row(oret.shape[1]),
                  pl.BlockSpec((1, 6, d), lambda i: ((i * tm) // seq, 0, 0))] + [full(a) for a in ops],
        out_specs=[row(d), row(d), col, col],
        out_shape=[jax.ShapeDtypeStruct((t, d), F32), jax.ShapeDtypeStruct((t, d), F32),
                   jax.ShapeDtypeStruct((TOP_K, t), I32), jax.ShapeDtypeStruct((TOP_K, t), F32)],
        compiler_params=_cparams(("arbitrary",)),
        name="post",
    )(x2, onsa, oret, mod3, *ops)


def _expert_select(idx_row, table, eidx):
    return jnp.sum(jnp.where(eidx == idx_row, table, 0.0), axis=0, keepdims=True)


def _rank_kernel(idx_ref, tri_ref, rank_ref, cnt_ref, carry):
    @pl.when(pl.program_id(0) == 0)
    def _():
        carry[...] = jnp.zeros_like(carry)

    idx = idx_ref[...]
    eidx = lax.broadcasted_iota(I32, (N_EXPERTS, 1), 0)
    member = jnp.zeros((N_EXPERTS, idx.shape[1]), F32)
    for k in range(TOP_K):
        member = member + (eidx == idx[k:k + 1, :]).astype(F32)
    before = _dot(member.astype(BF16), tri_ref[...]) + carry[:, 0:1]
    rank_ref[...] = jnp.concatenate(
        [_expert_select(idx[k:k + 1, :], before, eidx) for k in range(TOP_K)], axis=0).astype(I32)
    carry[...] = carry[...] + jnp.sum(member, axis=1, keepdims=True)
    cnt_ref[...] = carry[...]


def _dest_kernel(idx_ref, rank_ref, start_ref, pos_ref):
    idx = idx_ref[...]
    eidx = lax.broadcasted_iota(I32, (N_EXPERTS, 1), 0)
    start = start_ref[:, 0:1]
    base = jnp.concatenate([_expert_select(idx[k:k + 1, :], start, eidx) for k in range(TOP_K)], axis=0)
    pos_ref[...] = rank_ref[...] + base.astype(I32)


def _route_plan(idx_t, n_tok):
    tm = min(512, n_tok)
    tri = jnp.asarray(np.triu(np.ones((tm, tm), np.float32), 1), BF16)
    col = pl.BlockSpec((TOP_K, tm), lambda i: (0, i))
    rank, cnt = pl.pallas_call(
        _rank_kernel,
        grid=(n_tok // tm,),
        in_specs=[col, pl.BlockSpec((tm, tm), lambda i: (0, 0))],
        out_specs=[col, pl.BlockSpec((N_EXPERTS, LANES), lambda i: (0, 0))],
        out_shape=[jax.ShapeDtypeStruct((TOP_K, n_tok), I32), jax.ShapeDtypeStruct((N_EXPERTS, LANES), F32)],
        scratch_shapes=[pltpu.VMEM((N_EXPERTS, LANES), F32)],
        compiler_params=_cparams(("arbitrary",)),
        name="route_rank",
    )(idx_t, tri)
    counts = cnt[:, 0].astype(I32)
    padded = ((counts + MOE_TILE - 1) // MOE_TILE) * MOE_TILE
    ends = jnp.cumsum(padded)
    starts = ends - padded
    start_b = jnp.broadcast_to(starts.astype(F32)[:, None], (N_EXPERTS, LANES))
    pos = pl.pallas_call(
        _dest_kernel,
        grid=(n_tok // tm,),
        in_specs=[col, col, pl.BlockSpec((N_EXPERTS, LANES), lambda i: (0, 0))],
        out_specs=col,
        out_shape=jax.ShapeDtypeStruct((TOP_K, n_tok), I32),
        compiler_params=_cparams(("arbitrary",)),
        name="route_dest",
    )(idx_t, rank, start_b)
    n_tiles = TOP_K * n_tok // MOE_TILE + N_EXPERTS
    n_used = (ends[-1] // MOE_TILE).astype(I32)
    tile_start = jnp.minimum(jnp.arange(n_tiles, dtype=I32), n_used - 1) * MOE_TILE
    tile_e = jnp.minimum(jnp.searchsorted(ends, tile_start, side='right'), N_EXPERTS - 1).astype(I32)
    first = jnp.concatenate([jnp.ones((1,), I32), (tile_e[1:] != tile_e[:-1]).astype(I32)])
    return pos, tile_e, first, n_used.reshape(1), (starts + counts).astype(I32), (padded - counts).astype(I32)


def _pad_chunks(n):
    out = []
    b = MOE_TILE // 2
    while b >= SUBLANES:
        out.append(((n // (2 * b)) * (2 * b), b))
        b //= 2
    return out


def _dispatch_kernel(pad_start, pad_n, n_used, pos_ref, h2_ref, xs_hbm, zbuf, sem, zsem, *,
                     experts_per_step, n_tiles):
    i = pl.program_id(0)
    tc = h2_ref.shape[0]

    @pl.when(i == 0)
    def _():
        zbuf[...] = jnp.zeros_like(zbuf)

    for k in range(TOP_K):
        def body(jj, carry):
            for u in range(8):
                j = jj * 8 + u
                row = pos_ref[0, k, j]
                pltpu.make_async_copy(h2_ref.at[pl.ds(j, 1), :], xs_hbm.at[pl.ds(row, 1), :], sem).start()
            return carry
        lax.fori_loop(0, tc // 8, body, 0)

    def pad_copies(e, fn):
        a0 = pad_start[e]
        head = (-a0) & (SUBLANES - 1)
        for r in range(SUBLANES - 1):
            @pl.when(r < head)
            def _():
                fn(pltpu.make_async_copy(zbuf.at[pl.ds(0, 1), :], xs_hbm.at[pl.ds(a0 + r, 1), :], zsem))
        n = pad_n[e] - head
        for off, cnt in _pad_chunks(n):
            @pl.when((n & cnt) != 0)
            def _():
                dst = pl.multiple_of(a0 + head + off, SUBLANES)
                fn(pltpu.make_async_copy(zbuf.at[pl.ds(0, cnt), :], xs_hbm.at[pl.ds(dst, cnt), :], zsem))

    def tail_copies(tile, fn):
        for half in range(2):
            dst = pl.multiple_of(tile * MOE_TILE + half * (MOE_TILE // 2), MOE_TILE // 2)
            fn(pltpu.make_async_copy(zbuf, xs_hbm.at[pl.ds(dst, MOE_TILE // 2), :], zsem))

    for s in range(experts_per_step):
        e = i * experts_per_step + s
        tile = n_used[0] + e

        @pl.when(e < N_EXPERTS)
        def _():
            pad_copies(e, lambda cp: cp.start())

        @pl.when(tile < n_tiles)
        def _():
            tail_copies(tile, lambda cp: cp.start())

        @pl.when(e < N_EXPERTS)
        def _():
            pad_copies(e, lambda cp: cp.wait())

        @pl.when(tile < n_tiles)
        def _():
            tail_copies(tile, lambda cp: cp.wait())

    for k in range(TOP_K):
        pltpu.make_async_copy(h2_ref, xs_hbm.at[pl.ds(0, tc), :], sem).wait()


def _dispatch(h2, pos3, pad_start, pad_n, n_used, n_rows):
    t, d = h2.shape
    tc = pos3.shape[2]
    nt = t // tc
    eps = -(-N_EXPERTS // nt)
    grid_spec = pltpu.PrefetchScalarGridSpec(
        num_scalar_prefetch=3,
        grid=(nt,),
        in_specs=[pl.BlockSpec((1, TOP_K, tc), lambda i, ps, pn, nu: (i, 0, 0), memory_space=pltpu.SMEM),
                  pl.BlockSpec((tc, d), lambda i, ps, pn, nu: (i, 0))],
        out_specs=pl.BlockSpec(memory_space=pl.ANY),
        scratch_shapes=[pltpu.VMEM((MOE_TILE // 2, d), F32), pltpu.SemaphoreType.DMA(()),
                        pltpu.SemaphoreType.DMA(())])
    return pl.pallas_call(
        functools.partial(_dispatch_kernel, experts_per_step=eps, n_tiles=n_rows // MOE_TILE),
        grid_spec=grid_spec,
        out_shape=jax.ShapeDtypeStruct((n_rows, d), F32),
        compiler_params=_cparams(("arbitrary",)),
        name="dispatch",
    )(pad_start, pad_n, n_used, pos3, h2)


def _expert_kernel(tile_e, first, n_used, x_ref, wg_ref, wu_ref, wd_ref, y_ref, wgb, wub, wdb):
    i = pl.program_id(0)
    n = n_used[0]

    @pl.when(i < n)
    def _():
        @pl.when(first[i] == 1)
        def _():
            wgb[...] = wg_ref[0].astype(BF16)
            wub[...] = wu_ref[0].astype(BF16)
            wdb[...] = wd_ref[0].astype(BF16)

        xb = x_ref[...].astype(BF16)
        hid = (_silu(_dot(xb, wgb[...])) * _dot(xb, wub[...])).astype(BF16)
        y_ref[...] = _dot(hid, wdb[...])

    @pl.when(i >= n)
    def _():
        y_ref[...] = jnp.zeros_like(y_ref)


def _experts(xs, tile_e, first, n_used, wg, wu, wd):
    n_rows, d = xs.shape
    n_tiles = n_rows // MOE_TILE
    de = wg.shape[2]
    grid_spec = pltpu.PrefetchScalarGridSpec(
        num_scalar_prefetch=3,
        grid=(n_tiles,),
        in_specs=[pl.BlockSpec((MOE_TILE, d), lambda i, te, fi, nu: (jnp.minimum(i, nu[0] - 1), 0)),
                  pl.BlockSpec((1, d, de), lambda i, te, fi, nu: (te[i], 0, 0)),
                  pl.BlockSpec((1, d, de), lambda i, te, fi, nu: (te[i], 0, 0)),
                  pl.BlockSpec((1, de, d), lambda i, te, fi, nu: (te[i], 0, 0))],
        out_specs=pl.BlockSpec((MOE_TILE, d), lambda i, te, fi, nu: (i, 0)),
        scratch_shapes=[pltpu.VMEM((d, de), BF16), pltpu.VMEM((d, de), BF16), pltpu.VMEM((de, d), BF16)])
    return pl.pallas_call(
        _expert_kernel,
        grid_spec=grid_spec,
        out_shape=jax.ShapeDtypeStruct((n_rows, d), F32),
        compiler_params=_cparams(("arbitrary",)),
        name="experts",
    )(tile_e, first, n_used, xs, wg, wu, wd)


def _combine_kernel(posc_ref, posn_ref, ys_hbm, acc_ref, w_ref, mod_ref, g_ref, o_ref, buf, sem):
    i = pl.program_id(0)
    n = pl.num_programs(0)
    tc = acc_ref.shape[0]

    def issue(pos_ref, slot):
        for k in range(TOP_K):
            def body(jj, carry):
                for u in range(8):
                    j = jj * 8 + u
                    row = pos_ref[0, k, j]
                    pltpu.make_async_copy(ys_hbm.at[pl.ds(row, 1), :], buf.at[slot, k, pl.ds(j, 1), :],
                                          sem.at[slot]).start()
                return carry
            lax.fori_loop(0, tc // 8, body, 0)

    @pl.when(i == 0)
    def _():
        issue(posc_ref, 0)

    @pl.when(i + 1 < n)
    def _():
        issue(posn_ref, (i + 1) % 2)

    slot = i % 2
    for k in range(TOP_K):
        pltpu.make_async_copy(ys_hbm.at[pl.ds(0, tc), :], buf.at[slot, k], sem.at[slot]).wait()
    w = w_ref[...]
    routed = w[:, 0:1] * buf[slot, 0]
    for k in range(1, TOP_K):
        routed = routed + w[:, k:k + 1] * buf[slot, k]
    x2 = acc_ref[...] + mod_ref[0, 5:6, :] * routed
    o_ref[...] = x2 * lax.rsqrt(jnp.mean(x2 * x2, axis=-1, keepdims=True) + RMS_EPS) * g_ref[...]


def _tile_pos(pos_t, tc):
    return pos_t.reshape(TOP_K, pos_t.shape[1] // tc, tc).transpose(1, 0, 2)


def _combine(ys, pos3, acc0, w_tok, mod3, g_final, seq):
    t, d = acc0.shape
    tc = pos3.shape[2]
    nt = t // tc
    return pl.pallas_call(
        _combine_kernel,
        grid=(nt,),
        in_specs=[pl.BlockSpec((1, TOP_K, tc), lambda i: (i, 0, 0), memory_space=pltpu.SMEM),
                  pl.BlockSpec((1, TOP_K, tc), lambda i: (jnp.minimum(i + 1, nt - 1), 0, 0),
                               memory_space=pltpu.SMEM),
                  pl.BlockSpec(memory_space=pl.ANY),
                  pl.BlockSpec((tc, d), lambda i: (i, 0)),
                  pl.BlockSpec((tc, TOP_K), lambda i: (i, 0)),
                  pl.BlockSpec((1, 6, d), lambda i: ((i * tc) // seq, 0, 0)),
                  pl.BlockSpec((1, d), lambda i: (0, 0))],
        out_specs=pl.BlockSpec((tc, d), lambda i: (i, 0)),
        out_shape=jax.ShapeDtypeStruct((t, d), F32),
        scratch_shapes=[pltpu.VMEM((2, TOP_K, tc, d), F32), pltpu.SemaphoreType.DMA((2,))],
        compiler_params=_cparams(("arbitrary",)),
        name="combine",
    )(pos3, pos3, ys, acc0, w_tok, mod3, g_final)


def kernel(x, c, positions, w_ada, b_ada, g_norm_mix, w_in, cmp_pos_k, cmp_w1_k, cmp_w2_k, cmp_pos_v,
           cmp_w1_v, cmp_w2_v, g_nsa_out, g_ret_out, w_out, g_norm_ffn, w_router, router_bias,
           w_gate_e, w_up_e, w_down_e, w_gate_s, w_up_s, w_down_s, g_norm_final):
    bsz, seq, d = x.shape
    t = bsz * seq
    x2 = x.reshape(t, d)
    cos_t, sin_t = _rope_tables(positions.reshape(t, 1).astype(F32))
    n_cmp_pad = seq // CMP_STRIDE

    def cmp_rows(tab):
        rows = tab[:, :LANES].reshape(bsz, seq, LANES)[:, CMP_BLOCK - 1::CMP_STRIDE]
        return jnp.pad(rows, ((0, 0), (0, n_cmp_pad - rows.shape[1]), (0, 0)))

    cos_c, sin_c = cmp_rows(cos_t), cmp_rows(sin_t)
    for l in range(w_in.shape[0]):
        mod3 = _ada(c, w_ada[l], b_ada[l]).reshape(bsz, 6, d)
        q, kc, vc, ksw, gates, qr, kr, vr, gr = _inproj(
            x2, mod3, g_norm_mix[l].reshape(1, d), _arrange_w_in(w_in[l]), cos_t, sin_t, seq)
        kcmp, vcmp = _compress(kc, vc, cos_c, sin_c,
                               _arrange_cmp_weights(cmp_pos_k[l], cmp_w1_k[l], cmp_w2_k[l]),
                               _arrange_cmp_weights(cmp_pos_v[l], cmp_w1_v[l], cmp_w2_v[l]), bsz, seq)
        onsa = _nsa(q, kcmp, vcmp, ksw, gates, g_nsa_out[l].reshape(1, -1), bsz, seq)
        oret = _retention(qr, kr, vr, gr, g_ret_out[l], bsz, seq)
        acc0, h2, idx_t, w_t = _post(x2, onsa, oret, mod3, w_out[l], g_norm_ffn[l].reshape(1, d),
                                     w_router[l], router_bias[l], w_gate_s[l], w_up_s[l], w_down_s[l], seq)
        pos_t, tile_e, first, n_used, pad_start, pad_n = _route_plan(idx_t, t)
        n_rows = TOP_K * t + N_EXPERTS * MOE_TILE
        xs = _dispatch(h2, _tile_pos(pos_t, min(256, seq)), pad_start, pad_n, n_used, n_rows)
        ys = _experts(xs, tile_e, first, n_used, w_gate_e[l], w_up_e[l], w_down_e[l])
        last = l == w_in.shape[0] - 1
        gfin = g_norm_final.reshape(1, d)
        x2 = _combine(ys, _tile_pos(pos_t, 128), acc0, w_t.T, mod3, gfin, seq)
        assert last, "final norm is fused into the combine stage; depth 1 only"
    return x2.reshape(bsz, seq, d)
```

```python
import functools
import math

import numpy as np
import jax
import jax.numpy as jnp
from jax import lax
from jax.experimental import pallas as pl
from jax.experimental.pallas import tpu as pltpu

F32 = jnp.float32
BF16 = jnp.bfloat16
I32 = jnp.int32
HIGHEST = lax.Precision.HIGHEST

LANES = 128
SUBLANES = 8
NSA_HEAD_DIM = 64
NSA_HEADS = 8
NSA_GROUPS = 2
NSA_Q_PER_KV = NSA_HEADS // NSA_GROUPS
CMP_BLOCK = 32
CMP_STRIDE = 16
CMP_HIDDEN = 128
SEL_BLOCK = 64
SEL_TOP = 8
WINDOW = 512
Q_BLOCK = 128
FORCE_BONUS = 1.0e4
RET_HEADS = 4
RET_HEAD_DIM = 128
RET_CHUNK = 128
ROPE_THETA = 10000.0
RMS_EPS = 1e-6
N_EXPERTS = 256
N_EXPERT_GROUPS = 8
TOPK_GROUPS = 4
TOP_K = 8
ROUTED_SCALE = 2.5
MOE_TILE = 256
SEL_CHUNK = 256
MASK_BIG = 2.0 ** 100
MASK_FLOOR = 2.0 ** 99
NEG = -1e30
VMEM_LIMIT = 48 * 1024 * 1024


def _cparams(sem):
    return pltpu.CompilerParams(dimension_semantics=sem, vmem_limit_bytes=VMEM_LIMIT)


def _dot(a, b, **kw):
    return jnp.dot(a, b, preferred_element_type=F32, **kw)


def _dot_nt(a, b, **kw):
    return lax.dot_general(a, b, (((1,), (1,)), ((), ())), preferred_element_type=F32, **kw)


def _sigmoid(v):
    return 1.0 / (1.0 + jnp.exp(-v))


def _silu(v):
    return v * _sigmoid(v)


def _ada_kernel(c_ref, w_ref, b_ref, o_ref):
    o_ref[...] = _dot(_silu(c_ref[...]), w_ref[...], precision=HIGHEST) + b_ref[...]


def _ada(c, w, b):
    bsz, d = c.shape
    n = w.shape[1]
    tn = 1536
    cp = jnp.zeros((8, d), F32).at[:bsz].set(c)
    out = pl.pallas_call(
        _ada_kernel,
        grid=(n // tn,),
        in_specs=[pl.BlockSpec((8, d), lambda j: (0, 0)),
                  pl.BlockSpec((d, tn), lambda j: (0, j)),
                  pl.BlockSpec((1, tn), lambda j: (0, j))],
        out_specs=pl.BlockSpec((8, tn), lambda j: (0, j)),
        out_shape=jax.ShapeDtypeStruct((8, n), F32),
        compiler_params=_cparams(("arbitrary",)),
        name="ada",
    )(cp, w, b.reshape(1, n))
    return out[:bsz]


def _rope_table_kernel(pos_ref, inv_ref, sgn_ref, cos_ref, sin_ref):
    ang = pos_ref[...] * inv_ref[...]
    cos_ref[...] = jnp.cos(ang)
    sin_ref[...] = jnp.sin(ang) * sgn_ref[...]


def _rope_tables(posf):
    t = posf.shape[0]
    inv_n = ROPE_THETA ** (-jnp.arange(0, NSA_HEAD_DIM, 2, dtype=F32) / NSA_HEAD_DIM)
    inv_r = ROPE_THETA ** (-jnp.arange(0, RET_HEAD_DIM, 2, dtype=F32) / RET_HEAD_DIM)
    inv = jnp.concatenate([jnp.tile(inv_n, 4), jnp.tile(inv_r, 2)]).reshape(1, 2 * LANES)
    sgn_n = np.where((np.arange(LANES) % NSA_HEAD_DIM) < NSA_HEAD_DIM // 2, -1.0, 1.0)
    sgn_r = np.where(np.arange(LANES) < RET_HEAD_DIM // 2, -1.0, 1.0)
    sgn = jnp.asarray(np.concatenate([sgn_n, sgn_r]).reshape(1, 2 * LANES), F32)
    tm = min(t, 1024)
    return pl.pallas_call(
        _rope_table_kernel,
        grid=(t // tm,),
        in_specs=[pl.BlockSpec((tm, 1), lambda i: (i, 0)),
                  pl.BlockSpec((1, 2 * LANES), lambda i: (0, 0)),
                  pl.BlockSpec((1, 2 * LANES), lambda i: (0, 0))],
        out_specs=[pl.BlockSpec((tm, 2 * LANES), lambda i: (i, 0))] * 2,
        out_shape=[jax.ShapeDtypeStruct((t, 2 * LANES), F32)] * 2,
        compiler_params=_cparams(("arbitrary",)),
        name="rope_tables",
    )(posf, inv, sgn)


def _rope64(p, cos, sin, first_half):
    rot = jnp.where(first_half, pltpu.roll(p, 96, 1), pltpu.roll(p, 32, 1))
    return p * cos + rot * sin


def _rope128(p, cos, sin):
    return p * cos + pltpu.roll(p, 64, 1) * sin


_C_Q = 0
_C_KV = _C_Q + NSA_HEADS * LANES
_C_GATE = _C_KV + 6 * LANES
_C_RET = _C_GATE + LANES
_C_END = _C_RET + 4 * RET_HEADS * RET_HEAD_DIM


def _arrange_w_in(w_in):
    d = w_in.shape[0]
    nw = NSA_HEADS * NSA_HEAD_DIM
    q = w_in[:, :nw].reshape(d, NSA_HEADS, NSA_HEAD_DIM)
    z = jnp.zeros_like(q)
    grp = (jnp.arange(NSA_HEADS) // NSA_Q_PER_KV)[None, :, None]
    qpad = jnp.where(grp == 0, jnp.concatenate([q, z], -1), jnp.concatenate([z, q], -1))
    qpad = qpad.reshape(d, NSA_HEADS * LANES)
    kv = w_in[:, nw:nw + 6 * LANES]
    g0 = nw + 6 * LANES
    gate = jnp.pad(w_in[:, g0:g0 + 3 * NSA_HEADS], ((0, 0), (0, LANES - 3 * NSA_HEADS)))
    ret = w_in[:, g0 + 3 * NSA_HEADS:]
    return jnp.concatenate([qpad, kv, gate, ret], axis=1).astype(BF16)


def _inproj_kernel(x_ref, mod_ref, g_ref, w_ref, cos_ref, sin_ref,
                   q_ref, kc_ref, vc_ref, kk_ref, gate_ref, qr_ref, kr_ref, vr_ref, gr_ref, vst_ref, vwt_ref):
    x = x_ref[...]
    tm = x.shape[0]
    ms = jnp.mean(x * x, axis=-1, keepdims=True)
    y = x * lax.rsqrt(ms + RMS_EPS) * g_ref[...]
    h = y * (1.0 + mod_ref[0, 1:2, :]) + mod_ref[0, 0:1, :]
    hb = h.astype(BF16)
    cos_n, sin_n = cos_ref[:, 0:LANES], sin_ref[:, 0:LANES]
    cos_r, sin_r = cos_ref[:, LANES:], sin_ref[:, LANES:]
    lane = lax.broadcasted_iota(I32, (tm, LANES), 1)
    first_half = (lane % NSA_HEAD_DIM) < (NSA_HEAD_DIM // 2)
    scale_n = NSA_HEAD_DIM ** -0.5
    scale_r = RET_HEAD_DIM ** -0.5

    def proj(c0, n):
        return _dot(hb, w_ref[:, c0:c0 + n])

    for hh in range(NSA_HEADS):
        p = proj(_C_Q + hh * LANES, LANES)
        q_ref[:, hh * LANES:(hh + 1) * LANES] = (_rope64(p, cos_n, sin_n, first_half) * scale_n).astype(BF16)
    kv = proj(_C_KV, 6 * LANES)
    kc_ref[...] = kv[:, 0:LANES].astype(BF16)
    vc_ref[...] = kv[:, LANES:2 * LANES].astype(BF16)
    kk_ref[:, 0:LANES] = _rope64(kv[:, 2 * LANES:3 * LANES], cos_n, sin_n, first_half).astype(BF16)
    kk_ref[:, LANES:2 * LANES] = _rope64(kv[:, 4 * LANES:5 * LANES], cos_n, sin_n, first_half).astype(BF16)
    group0 = lane < NSA_HEAD_DIM
    for c0, vt_ref, chunk in ((3 * LANES, vst_ref, SEL_CHUNK), (5 * LANES, vwt_ref, Q_BLOCK)):
        v = kv[:, c0:c0 + LANES]
        for g, vg in enumerate((jnp.where(group0, v, 1.0), jnp.where(group0, 1.0, v))):
            for cc in range(tm // chunk):
                vt_ref[cc, g] = vg[cc * chunk:(cc + 1) * chunk].T.astype(BF16)
    gate_ref[...] = _sigmoid(proj(_C_GATE, LANES))
    rw = RET_HEADS * RET_HEAD_DIM
    for hh in range(RET_HEADS):
        sl = slice(hh * LANES, (hh + 1) * LANES)
        pq = proj(_C_RET + hh * LANES, LANES)
        qr_ref[:, sl] = _rope128(pq, cos_r, sin_r).astype(BF16)
        pk = proj(_C_RET + rw + hh * LANES, LANES)
        kr_ref[:, sl] = (_rope128(pk, cos_r, sin_r) * scale_r).astype(BF16)
    vr_ref[...] = proj(_C_RET + 2 * rw, rw).astype(BF16)
    gr_ref[...] = proj(_C_RET + 3 * rw, rw)


def _inproj(x2, mod3, g_mix, w_in_p, cos_t, sin_t, seq):
    t, d = x2.shape
    tm = min(512, seq)
    rw = RET_HEADS * RET_HEAD_DIM
    row = lambda n: pl.BlockSpec((tm, n), lambda i: (i, 0))
    outs = [(NSA_HEADS * LANES, BF16), (LANES, BF16), (LANES, BF16), (2 * LANES, BF16), (LANES, F32),
            (rw, BF16), (rw, BF16), (rw, BF16), (rw, F32)]
    vt_specs, vt_shapes = [], []
    for chunk in (SEL_CHUNK, Q_BLOCK):
        vt_specs.append(pl.BlockSpec((tm // chunk, NSA_GROUPS, LANES, chunk), lambda i: (i, 0, 0, 0)))
        vt_shapes.append(jax.ShapeDtypeStruct((t // chunk, NSA_GROUPS, LANES, chunk), BF16))
    return pl.pallas_call(
        _inproj_kernel,
        grid=(t // tm,),
        in_specs=[row(d),
                  pl.BlockSpec((1, 6, d), lambda i: ((i * tm) // seq, 0, 0)),
                  pl.BlockSpec((1, d), lambda i: (0, 0)),
                  pl.BlockSpec((d, _C_END), lambda i: (0, 0)),
                  row(2 * LANES), row(2 * LANES)],
        out_specs=[row(n) for n, _ in outs] + vt_specs,
        out_shape=[jax.ShapeDtypeStruct((t, n), dt) for n, dt in outs] + vt_shapes,
        compiler_params=_cparams(("arbitrary",)),
        name="inproj",
    )(x2, mod3, g_mix, w_in_p, cos_t, sin_t)


def _gelu_tanh(v):
    return v * (0.5 * (1.0 + jnp.tanh(math.sqrt(2.0 / math.pi) * (v + 0.044715 * (v * v * v)))))


def _compress_kernel(kc_ref, vc_ref, wkt_ref, wkb_ref, wk2_ref, pk_ref, wvt_ref, wvb_ref, wv2_ref, pv_ref,
                     cos_ref, sin_ref, ko_ref, vo_ref):
    def one(x_ref, wt_ref, wb_ref, w2_ref, p_ref):
        xx = x_ref[0]
        a = _dot(xx, wt_ref[...])
        b = _dot(xx, wb_ref[...])
        pb = p_ref[...].astype(BF16)
        bias = _dot(pb, wt_ref[...])[0:1] + _dot(pb, wb_ref[...])[1:2]
        hid = a + pltpu.roll(b, b.shape[0] - 1, 0) + bias
        return _dot(_gelu_tanh(hid).astype(BF16), w2_ref[...])

    k = one(kc_ref, wkt_ref, wkb_ref, wk2_ref, pk_ref)
    lane = lax.broadcasted_iota(I32, k.shape, 1)
    first_half = (lane % NSA_HEAD_DIM) < (NSA_HEAD_DIM // 2)
    ko_ref[0] = _rope64(k, cos_ref[0], sin_ref[0], first_half).astype(BF16)
    v = one(vc_ref, wvt_ref, wvb_ref, wv2_ref, pv_ref)
    group0 = lane < NSA_HEAD_DIM
    vo_ref[0, 0] = jnp.where(group0, v, 1.0).T.astype(BF16)
    vo_ref[0, 1] = jnp.where(group0, 1.0, v).T.astype(BF16)


def _arrange_cmp_weights(pos, w1, w2):
    half = CMP_BLOCK // 2
    dh = NSA_HEAD_DIM
    w1r = w1.reshape(CMP_BLOCK, dh, CMP_HIDDEN)

    def block(wpart):
        z = jnp.zeros_like(wpart)
        g0 = jnp.concatenate([wpart, z], axis=-1)
        g1 = jnp.concatenate([z, wpart], axis=-1)
        return jnp.stack([g0, g1], axis=1).reshape(half * 2 * dh, 2 * CMP_HIDDEN)

    wt, wb = block(w1r[:half]), block(w1r[half:])
    z2 = jnp.zeros_like(w2)
    w2b = jnp.concatenate([jnp.concatenate([w2, z2], 1), jnp.concatenate([z2, w2], 1)], 0)
    ptop = jnp.tile(pos[:half], (1, 2)).reshape(1, -1)
    pbot = jnp.tile(pos[half:], (1, 2)).reshape(1, -1)
    prow = jnp.concatenate([ptop, pbot, jnp.zeros((6, ptop.shape[1]), F32)], 0)
    return wt.astype(BF16), wb.astype(BF16), w2b.astype(BF16), prow


def _compress(kc, vc, cos_c, sin_c, kparams, vparams, bsz, seq):
    nchunk = seq // CMP_STRIDE
    width = CMP_STRIDE * LANES
    kc16 = kc.reshape(bsz, nchunk, width)
    vc16 = vc.reshape(bsz, nchunk, width)
    full = lambda a: pl.BlockSpec(a.shape, lambda b: (0,) * a.ndim)
    per_b = lambda n: pl.BlockSpec((1, nchunk, n), lambda b: (b, 0, 0))
    return pl.pallas_call(
        _compress_kernel,
        grid=(bsz,),
        in_specs=[per_b(width), per_b(width)] + [full(a) for a in kparams] + [full(a) for a in vparams]
                 + [per_b(LANES), per_b(LANES)],
        out_specs=[per_b(LANES), pl.BlockSpec((1, NSA_GROUPS, LANES, nchunk), lambda b: (b, 0, 0, 0))],
        out_shape=[jax.ShapeDtypeStruct((bsz, nchunk, LANES), BF16),
                   jax.ShapeDtypeStruct((bsz, NSA_GROUPS, LANES, nchunk), BF16)],
        compiler_params=_cparams(("arbitrary",)),
        name="compress",
    )(kc16, vc16, *kparams, *vparams, cos_c, sin_c)


def _softmax_chunk(m, acc, s, vt):
    m_new = jnp.maximum(m, jnp.max(s, axis=0, keepdims=True))
    e = jnp.exp(s - m_new).astype(BF16)
    acc = jnp.exp(m - m_new) * acc + _dot(vt, e)
    return m_new, acc


def _nsa_kernel(q_ref, kc_ref, vct_ref, ks_ref, kw_ref, vst_ref, vwt_ref, gate_ref,
                ovt_ref, exp_ref, g_ref, o_ref, *, seq):
    qb = pl.program_id(1)
    t0 = qb * Q_BLOCK
    cols = NSA_Q_PER_KV * Q_BLOCK
    n_cmp_pad = kc_ref.shape[1]
    n_sel = seq // SEL_BLOCK
    t_row = t0 + lax.broadcasted_iota(I32, (1, Q_BLOCK), 1)
    gates_t = gate_ref[...].T
    lane = lax.broadcasted_iota(I32, (Q_BLOCK, LANES), 1)
    key = lax.broadcasted_iota(I32, (Q_BLOCK, Q_BLOCK), 0)
    tok = lax.broadcasted_iota(I32, (Q_BLOCK, Q_BLOCK), 1)
    cend = lax.broadcasted_iota(I32, (n_cmp_pad, 1), 0) * CMP_STRIDE + (CMP_BLOCK - 1)
    bias_cmp = jnp.where(cend <= t_row, 0.0, -MASK_BIG)
    last_c = qb // (SEL_CHUNK // Q_BLOCK)
    kpos_last = last_c * SEL_CHUNK + lax.broadcasted_iota(I32, (SEL_CHUNK, 1), 0)
    bias_diag = jnp.where(kpos_last <= t_row, 0.0, -MASK_BIG)
    n_win = WINDOW // Q_BLOCK
    bias_win_first = jnp.where(key > tok, 0.0, -MASK_BIG)
    bias_win_last = jnp.where(key <= tok, 0.0, -MASK_BIG)
    init = (jnp.full((1, cols), -MASK_FLOOR, F32), jnp.zeros((LANES, cols), F32))
    tile_heads = lambda b: jnp.concatenate([b] * NSA_Q_PER_KV, axis=1)

    def finish(acc, g):
        ones_row = NSA_HEAD_DIM * (1 - g)
        return acc * (1.0 / jnp.maximum(acc[ones_row:ones_row + 1, :], 1e-20))

    q4s, o_cs, sel_bs = [], [], []
    for g in range(NSA_GROUPS):
        q4 = jnp.concatenate([q_ref[:, (NSA_Q_PER_KV * g + r) * LANES:(NSA_Q_PER_KV * g + r + 1) * LANES]
                              for r in range(NSA_Q_PER_KV)], axis=0)
        q4s.append(q4)
        s_c = _dot_nt(kc_ref[0], q4) + tile_heads(bias_cmp)
        e_c = jnp.exp(s_c - jnp.maximum(jnp.max(s_c, axis=0, keepdims=True), -MASK_FLOOR))
        p_c = e_c * (1.0 / jnp.maximum(jnp.sum(e_c, axis=0, keepdims=True), 1e-20))
        psum = p_c[:, 0:Q_BLOCK]
        for r in range(1, NSA_Q_PER_KV):
            psum = psum + p_c[:, r * Q_BLOCK:(r + 1) * Q_BLOCK]
        o_cs.append(finish(_dot(vct_ref[0, g], e_c.astype(BF16)), g))
        imp_t = _dot(ovt_ref[...], psum, precision=HIGHEST)
        nsp = imp_t.shape[0]
        jrow = lax.broadcasted_iota(I32, (nsp, 1), 0)
        cur = t_row // SEL_BLOCK
        valid = (jrow * SEL_BLOCK <= t_row) & (jrow < n_sel)
        forced = (jrow == 0) | (jrow == cur) | (jrow == cur - 1)
        val = jnp.where(valid, imp_t + jnp.where(forced, FORCE_BONUS, 0.0), -1.0)
        val = jnp.where(jrow < n_sel, val, -jnp.inf)
        sel_t = jnp.zeros((nsp, Q_BLOCK), F32)
        for _ in range(min(SEL_TOP, n_sel)):
            mx = jnp.max(val, axis=0, keepdims=True)
            jmin = jnp.min(jnp.where(val == mx, jrow, nsp), axis=0, keepdims=True)
            hit = jrow == jmin
            sel_t = jnp.where(hit, 1.0, sel_t)
            val = jnp.where(hit, -jnp.inf, val)
        sel_bs.append(jnp.concatenate([sel_t[:n_sel], jnp.ones((SUBLANES, Q_BLOCK), F32),
                                       jnp.zeros((LANES - n_sel - SUBLANES, Q_BLOCK), F32)], axis=0).astype(BF16))

    def sel_scores(c):
        keys = ks_ref[pl.ds(pl.multiple_of(c * SEL_CHUNK, SEL_CHUNK), SEL_CHUNK), :]
        expand = exp_ref[c]
        return tuple(_dot_nt(keys, q4s[g]) + tile_heads(_dot(expand, sel_bs[g])) for g in range(NSA_GROUPS))

    def sel_reduce(c, state, scores):
        out = []
        for g in range(NSA_GROUPS):
            out.extend(_softmax_chunk(state[2 * g], state[2 * g + 1], scores[g], vst_ref[c, g]))
        return tuple(out)

    def sel_step(c, carry):
        state, scores = carry
        return sel_reduce(c, state, scores), sel_scores(c + 1)

    state, scores = lax.fori_loop(0, last_c, sel_step, (init * NSA_GROUPS, sel_scores(0)))
    diag = tile_heads(bias_diag)
    state = sel_reduce(last_c, state, tuple(s + diag for s in scores))
    o_ss = [finish(state[2 * g + 1], g) for g in range(NSA_GROUPS)]
    s_ws = [[] for _ in range(NSA_GROUPS)]
    kbs = []
    for w in range(n_win + 1):
        wb = qb - n_win + w
        kb = jnp.maximum(wb, 0)
        kbs.append(kb)
        keys = kw_ref[pl.ds(pl.multiple_of(kb * Q_BLOCK, Q_BLOCK), Q_BLOCK), :]
        if w == n_win:
            bias = bias_win_last
        else:
            before_start = jnp.where(wb >= 0, 0.0, -MASK_BIG)
            bias = (bias_win_first + before_start) if w == 0 else jnp.full((Q_BLOCK, Q_BLOCK), before_start)
        for g in range(NSA_GROUPS):
            s_ws[g].append(_dot_nt(keys, q4s[g]) + tile_heads(bias))
    o_ws = []
    for g in range(NSA_GROUPS):
        m_w = jnp.full((1, cols), -MASK_FLOOR, F32)
        for s in s_ws[g]:
            m_w = jnp.maximum(m_w, jnp.max(s, axis=0, keepdims=True))
        acc = None
        for kb, s in zip(kbs, s_ws[g]):
            pv = _dot(vwt_ref[kb, g], jnp.exp(s - m_w).astype(BF16))
            acc = pv if acc is None else acc + pv
        o_ws.append(finish(acc, g))
    heads_out = []
    for g in range(NSA_GROUPS):
        o_c, o_s, o_w = o_cs[g], o_ss[g], o_ws[g]
        for r in range(NSA_Q_PER_KV):
            hh = NSA_Q_PER_KV * g + r
            cs = slice(r * Q_BLOCK, (r + 1) * Q_BLOCK)
            o = (gates_t[3 * hh:3 * hh + 1, :] * o_c[:, cs] + gates_t[3 * hh + 1:3 * hh + 2, :] * o_s[:, cs]
                 + gates_t[3 * hh + 2:3 * hh + 3, :] * o_w[:, cs]).T
            if (hh % 2) != g:
                o = pltpu.roll(o, NSA_HEAD_DIM, 1)
            heads_out.append(o)
    blocks = [jnp.where(lane < NSA_HEAD_DIM, heads_out[2 * i], heads_out[2 * i + 1])
              for i in range(NSA_HEADS // 2)]
    ss = sum(jnp.sum(b * b, axis=-1, keepdims=True) for b in blocks)
    inv = lax.rsqrt(ss / (NSA_HEADS * NSA_HEAD_DIM) + RMS_EPS)
    for i, b in enumerate(blocks):
        sl = slice(i * LANES, (i + 1) * LANES)
        o_ref[:, sl] = (b * inv * g_ref[:, sl]).astype(BF16)


def _nsa(q, kcmp, vcmp_t, kk, vsel_t, vwin_t, gates, g_nsa, bsz, seq):
    t = bsz * seq
    nq = seq // Q_BLOCK
    n_cmp_pad = seq // CMP_STRIDE
    n_cmp = (seq - CMP_BLOCK) // CMP_STRIDE + 1
    n_sel = seq // SEL_BLOCK
    assert n_sel % SUBLANES == 0 and n_sel + SUBLANES <= LANES, "selection mask needs a spare expansion row"
    nsp = n_sel
    cs = np.arange(n_cmp_pad) * CMP_STRIDE
    ss = np.arange(nsp) * SEL_BLOCK
    ov = ((cs[None, :] < ss[:, None] + SEL_BLOCK) & (cs[None, :] + CMP_BLOCK > ss[:, None])
          & (np.arange(n_cmp_pad)[None, :] < n_cmp) & (np.arange(nsp)[:, None] < n_sel))
    ovt = jnp.asarray(ov, F32)
    nch = seq // SEL_CHUNK
    kp = np.arange(seq).reshape(nch, SEL_CHUNK, 1)
    col = np.arange(LANES).reshape(1, 1, LANES)
    expand = jnp.asarray(np.where(kp // SEL_BLOCK == col, MASK_BIG, 0.0) + np.where(col == n_sel, -MASK_BIG, 0.0),
                         BF16)
    seqcol = lambda c: pl.BlockSpec((seq, LANES), lambda b, i: (b, c))
    per_b = lambda a: pl.BlockSpec((a.shape[0] // bsz,) + a.shape[1:], lambda b, i: (b,) + (0,) * (a.ndim - 1))
    return pl.pallas_call(
        functools.partial(_nsa_kernel, seq=seq),
        grid=(bsz, nq),
        in_specs=[pl.BlockSpec((Q_BLOCK, NSA_HEADS * LANES), lambda b, i: (b * nq + i, 0)),
                  per_b(kcmp), per_b(vcmp_t), seqcol(0), seqcol(1), per_b(vsel_t), per_b(vwin_t),
                  pl.BlockSpec((Q_BLOCK, LANES), lambda b, i: (b * nq + i, 0)),
                  pl.BlockSpec(ovt.shape, lambda b, i: (0, 0)),
                  pl.BlockSpec(expand.shape, lambda b, i: (0, 0, 0)),
                  pl.BlockSpec((1, NSA_HEADS * NSA_HEAD_DIM), lambda b, i: (0, 0))],
        out_specs=pl.BlockSpec((Q_BLOCK, NSA_HEADS * NSA_HEAD_DIM), lambda b, i: (b * nq + i, 0)),
        out_shape=jax.ShapeDtypeStruct((t, NSA_HEADS * NSA_HEAD_DIM), BF16),
        compiler_params=_cparams(("arbitrary", "arbitrary")),
        name="nsa",
    )(q, kcmp, vcmp_t, kk, kk, vsel_t, vwin_t, gates, ovt, expand, g_nsa)


def _ret_kernel(q_ref, k_ref, v_ref, gr_ref, dec_ref, xi_ref, zeta_ref, gch_ref, g_ref, o_ref, st_ref):
    @pl.when(pl.program_id(1) == 0)
    def _():
        st_ref[...] = jnp.zeros_like(st_ref)

    for hh in range(RET_HEADS):
        sl = slice(hh * LANES, (hh + 1) * LANES)
        q, k, v = q_ref[:, sl], k_ref[:, sl], v_ref[:, sl]
        sc = _dot_nt(q, k) * dec_ref[hh]
        inner = _dot(sc.astype(BF16), v)
        st = st_ref[hh]
        cross = _dot((q.astype(F32) * xi_ref[hh]).astype(BF16), st.astype(BF16))
        kz = (k.astype(F32) * zeta_ref[hh]).T.astype(BF16)
        st_ref[hh] = st * gch_ref[hh] + _dot(kz, v)
        o = inner + cross
        y = o * lax.rsqrt(jnp.mean(o * o, axis=-1, keepdims=True) + RMS_EPS) * g_ref[hh:hh + 1, :]
        o_ref[:, sl] = (_silu(gr_ref[:, sl]) * y).astype(BF16)


def _retention(qr, kr, vr, gr, g_ret, bsz, seq):
    t = bsz * seq
    c = RET_CHUNK
    n = seq // c
    log_g = jnp.log(1.0 - 2.0 ** (-5.0 - jnp.arange(RET_HEADS, dtype=F32)))
    i = jnp.arange(c, dtype=F32)
    diff = i[:, None] - i[None, :]
    causal = diff >= 0
    dec = jnp.where(causal, jnp.exp(log_g[:, None, None] * jnp.where(causal, diff, 0.0)), 0.0)
    xi = jnp.broadcast_to(jnp.exp(log_g[:, None] * (i + 1.0))[:, :, None], (RET_HEADS, c, LANES))
    zeta = jnp.broadcast_to(jnp.exp(log_g[:, None] * (c - 1.0 - i))[:, :, None], (RET_HEADS, c, LANES))
    gch = jnp.broadcast_to(jnp.exp(log_g * c)[:, None, None], (RET_HEADS, 1, LANES))
    w = RET_HEADS * RET_HEAD_DIM
    row = pl.BlockSpec((c, w), lambda b, j: (b * n + j, 0))
    full = lambda a: pl.BlockSpec(a.shape, lambda b, j: (0,) * a.ndim)
    return pl.pallas_call(
        _ret_kernel,
        grid=(bsz, n),
        in_specs=[row, row, row, row, full(dec), full(xi), full(zeta), full(gch), full(g_ret)],
        out_specs=row,
        out_shape=jax.ShapeDtypeStruct((t, w), BF16),
        scratch_shapes=[pltpu.VMEM((RET_HEADS, RET_HEAD_DIM, RET_HEAD_DIM), F32)],
        compiler_params=_cparams(("arbitrary", "arbitrary")),
        name="retention",
    )(qr, kr, vr, gr, dec, xi, zeta, gch, g_ret)


def _post_kernel(x_ref, onsa_ref, oret_ref, mod_ref, wo1_ref, wo2_ref, gffn_ref, wrt_ref, rb_ref,
                 wgs_ref, wus_ref, wds_ref, acc_ref, h2_ref, idx_ref, wt_ref):
    mix = _dot(onsa_ref[...], wo1_ref[...]) + _dot(oret_ref[...], wo2_ref[...])
    x1 = x_ref[...] + mod_ref[0, 2:3, :] * mix
    ms = jnp.mean(x1 * x1, axis=-1, keepdims=True)
    h2 = x1 * lax.rsqrt(ms + RMS_EPS) * gffn_ref[...] * (1.0 + mod_ref[0, 4:5, :]) + mod_ref[0, 3:4, :]
    h2_ref[...] = h2
    hb = h2.astype(BF16)
    hid = (_silu(_dot(hb, wgs_ref[...])) * _dot(hb, wus_ref[...])).astype(BF16)
    acc_ref[...] = x1 + mod_ref[0, 5:6, :] * _dot(hid, wds_ref[...])
    s = _sigmoid(_dot_nt(wrt_ref[...], h2, precision=HIGHEST))
    sb = s + rb_ref[...]
    per = N_EXPERTS // N_EXPERT_GROUPS
    ridx = lax.broadcasted_iota(I32, (per, 1), 0)
    blks, grp = [], []
    for gi in range(N_EXPERT_GROUPS):
        blk = sb[gi * per:(gi + 1) * per]
        m1 = jnp.max(blk, axis=0, keepdims=True)
        first = jnp.min(jnp.where(blk == m1, ridx, per), axis=0, keepdims=True)
        m2 = jnp.max(jnp.where(ridx == first, -jnp.inf, blk), axis=0, keepdims=True)
        blks.append(blk)
        grp.append(m1 + m2)
    masked = []
    for gi in range(N_EXPERT_GROUPS):
        rank = jnp.zeros_like(grp[gi])
        for gj in range(N_EXPERT_GROUPS):
            if gj < gi:
                rank = rank + (grp[gj] >= grp[gi]).astype(F32)
            elif gj > gi:
                rank = rank + (grp[gj] > grp[gi]).astype(F32)
        masked.append(jnp.where(rank < TOPK_GROUPS, blks[gi], NEG))
    val = jnp.concatenate(masked, axis=0)
    eidx = lax.broadcasted_iota(I32, (N_EXPERTS, 1), 0)
    ids, ws = [], []
    for _ in range(TOP_K):
        mx = jnp.max(val, axis=0, keepdims=True)
        emin = jnp.min(jnp.where(val == mx, eidx, N_EXPERTS), axis=0, keepdims=True)
        hit = eidx == emin
        ids.append(emin)
        ws.append(jnp.sum(jnp.where(hit, s, 0.0), axis=0, keepdims=True))
        val = jnp.where(hit, -jnp.inf, val)
    wsum = ws[0]
    for wk in ws[1:]:
        wsum = wsum + wk
    idx_ref[...] = jnp.concatenate(ids, axis=0)
    wt_ref[...] = jnp.concatenate(ws, axis=0) / wsum * ROUTED_SCALE


def _post(x2, onsa, oret, mod3, w_out, g_ffn, w_router, router_bias, wgs, wus, wds, seq):
    t, d = x2.shape
    tm = min(512, seq)
    hw = onsa.shape[1]
    wo1 = w_out[:hw].astype(BF16)
    wo2 = w_out[hw:].astype(BF16)
    wrt = w_router.T
    rb = jnp.broadcast_to(router_bias.reshape(N_EXPERTS, 1), (N_EXPERTS, tm))
    row = lambda n: pl.BlockSpec((tm, n), lambda i: (i, 0))
    full = lambda a: pl.BlockSpec(a.shape, lambda i: (0,) * a.ndim)
    col = pl.BlockSpec((TOP_K, tm), lambda i: (0, i))
    ops = (wo1, wo2, g_ffn, wrt, rb, wgs.astype(BF16), wus.astype(BF16), wds.astype(BF16))
    return pl.pallas_call(
        _post_kernel,
        grid=(t // tm,),
        in_specs=[row(d), row(hw), row(oret.shape[1]),
                  pl.BlockSpec((1, 6, d), lambda i: ((i * tm) // seq, 0, 0))] + [full(a) for a in ops],
        out_specs=[row(d), row(d), col, col],
        out_shape=[jax.ShapeDtypeStruct((t, d), F32), jax.ShapeDtypeStruct((t, d), F32),
                   jax.ShapeDtypeStruct((TOP_K, t), I32), jax.ShapeDtypeStruct((TOP_K, t), F32)],
        compiler_params=_cparams(("arbitrary",)),
        name="post",
    )(x2, onsa, oret, mod3, *ops)


def _expert_select(idx_row, table, eidx):
    return jnp.sum(jnp.where(eidx == idx_row, table, 0.0), axis=0, keepdims=True)


def _rank_kernel(idx_ref, tri_ref, rank_ref, cnt_ref, carry):
    @pl.when(pl.program_id(0) == 0)
    def _():
        carry[...] = jnp.zeros_like(carry)

    idx = idx_ref[...]
    eidx = lax.broadcasted_iota(I32, (N_EXPERTS, 1), 0)
    member = jnp.zeros((N_EXPERTS, idx.shape[1]), F32)
    for k in range(TOP_K):
        member = member + (eidx == idx[k:k + 1, :]).astype(F32)
    before = _dot(member.astype(BF16), tri_ref[...]) + carry[:, 0:1]
    rank_ref[...] = jnp.concatenate(
        [_expert_select(idx[k:k + 1, :], before, eidx) for k in range(TOP_K)], axis=0).astype(I32)
    carry[...] = carry[...] + jnp.sum(member, axis=1, keepdims=True)
    cnt_ref[...] = carry[...]


def _dest_kernel(idx_ref, rank_ref, start_ref, pos_ref):
    idx = idx_ref[...]
    eidx = lax.broadcasted_iota(I32, (N_EXPERTS, 1), 0)
    start = start_ref[:, 0:1]
    base = jnp.concatenate([_expert_select(idx[k:k + 1, :], start, eidx) for k in range(TOP_K)], axis=0)
    pos_ref[...] = rank_ref[...] + base.astype(I32)


def _route_plan(idx_t, n_tok):
    tm = min(512, n_tok)
    tri = jnp.asarray(np.triu(np.ones((tm, tm), np.float32), 1), BF16)
    col = pl.BlockSpec((TOP_K, tm), lambda i: (0, i))
    rank, cnt = pl.pallas_call(
        _rank_kernel,
        grid=(n_tok // tm,),
        in_specs=[col, pl.BlockSpec((tm, tm), lambda i: (0, 0))],
        out_specs=[col, pl.BlockSpec((N_EXPERTS, LANES), lambda i: (0, 0))],
        out_shape=[jax.ShapeDtypeStruct((TOP_K, n_tok), I32), jax.ShapeDtypeStruct((N_EXPERTS, LANES), F32)],
        scratch_shapes=[pltpu.VMEM((N_EXPERTS, LANES), F32)],
        compiler_params=_cparams(("arbitrary",)),
        name="route_rank",
    )(idx_t, tri)
    counts = cnt[:, 0].astype(I32)
    padded = ((counts + MOE_TILE - 1) // MOE_TILE) * MOE_TILE
    ends = jnp.cumsum(padded)
    starts = ends - padded
    start_b = jnp.broadcast_to(starts.astype(F32)[:, None], (N_EXPERTS, LANES))
    pos = pl.pallas_call(
        _dest_kernel,
        grid=(n_tok // tm,),
        in_specs=[col, col, pl.BlockSpec((N_EXPERTS, LANES), lambda i: (0, 0))],
        out_specs=col,
        out_shape=jax.ShapeDtypeStruct((TOP_K, n_tok), I32),
        compiler_params=_cparams(("arbitrary",)),
        name="route_dest",
    )(idx_t, rank, start_b)
    n_tiles = TOP_K * n_tok // MOE_TILE + N_EXPERTS
    n_used = (ends[-1] // MOE_TILE).astype(I32)
    tile_start = jnp.minimum(jnp.arange(n_tiles, dtype=I32), n_used - 1) * MOE_TILE
    tile_e = jnp.minimum(jnp.searchsorted(ends, tile_start, side='right'), N_EXPERTS - 1).astype(I32)
    first = jnp.concatenate([jnp.ones((1,), I32), (tile_e[1:] != tile_e[:-1]).astype(I32)])
    return pos, tile_e, first, n_used.reshape(1), (starts + counts).astype(I32), (padded - counts).astype(I32)


def _pad_chunks(n):
    out = []
    b = MOE_TILE // 2
    while b >= SUBLANES:
        out.append(((n // (2 * b)) * (2 * b), b))
        b //= 2
    return out


def _dispatch_kernel(pad_start, pad_n, n_used, pos_ref, h2_ref, xs_hbm, zbuf, sem, zsem, *,
                     experts_per_step, n_tiles):
    i = pl.program_id(0)
    tc = h2_ref.shape[0]

    @pl.when(i == 0)
    def _():
        zbuf[...] = jnp.zeros_like(zbuf)

    for k in range(TOP_K):
        def body(jj, carry):
            for u in range(8):
                j = jj * 8 + u
                row = pos_ref[0, k, j]
                pltpu.make_async_copy(h2_ref.at[pl.ds(j, 1), :], xs_hbm.at[pl.ds(row, 1), :], sem).start()
            return carry
        lax.fori_loop(0, tc // 8, body, 0)

    def pad_copies(e, fn):
        a0 = pad_start[e]
        head = (-a0) & (SUBLANES - 1)
        for r in range(SUBLANES - 1):
            @pl.when(r < head)
            def _():
                fn(pltpu.make_async_copy(zbuf.at[pl.ds(0, 1), :], xs_hbm.at[pl.ds(a0 + r, 1), :], zsem))
        n = pad_n[e] - head
        for off, cnt in _pad_chunks(n):
            @pl.when((n & cnt) != 0)
            def _():
                dst = pl.multiple_of(a0 + head + off, SUBLANES)
                fn(pltpu.make_async_copy(zbuf.at[pl.ds(0, cnt), :], xs_hbm.at[pl.ds(dst, cnt), :], zsem))

    def tail_copies(tile, fn):
        for half in range(2):
            dst = pl.multiple_of(tile * MOE_TILE + half * (MOE_TILE // 2), MOE_TILE // 2)
            fn(pltpu.make_async_copy(zbuf, xs_hbm.at[pl.ds(dst, MOE_TILE // 2), :], zsem))

    for s in range(experts_per_step):
        e = i * experts_per_step + s
        tile = n_used[0] + e

        @pl.when(e < N_EXPERTS)
        def _():
            pad_copies(e, lambda cp: cp.start())

        @pl.when(tile < n_tiles)
        def _():
            tail_copies(tile, lambda cp: cp.start())

        @pl.when(e < N_EXPERTS)
        def _():
            pad_copies(e, lambda cp: cp.wait())

        @pl.when(tile < n_tiles)
        def _():
            tail_copies(tile, lambda cp: cp.wait())

    for k in range(TOP_K):
        pltpu.make_async_copy(h2_ref, xs_hbm.at[pl.ds(0, tc), :], sem).wait()


def _dispatch(h2, pos3, pad_start, pad_n, n_used, n_rows):
    t, d = h2.shape
    tc = pos3.shape[2]
    nt = t // tc
    eps = -(-N_EXPERTS // nt)
    grid_spec = pltpu.PrefetchScalarGridSpec(
        num_scalar_prefetch=3,
        grid=(nt,),
        in_specs=[pl.BlockSpec((1, TOP_K, tc), lambda i, ps, pn, nu: (i, 0, 0), memory_space=pltpu.SMEM),
                  pl.BlockSpec((tc, d), lambda i, ps, pn, nu: (i, 0))],
        out_specs=pl.BlockSpec(memory_space=pl.ANY),
        scratch_shapes=[pltpu.VMEM((MOE_TILE // 2, d), F32), pltpu.SemaphoreType.DMA(()),
                        pltpu.SemaphoreType.DMA(())])
    return pl.pallas_call(
        functools.partial(_dispatch_kernel, experts_per_step=eps, n_tiles=n_rows // MOE_TILE),
        grid_spec=grid_spec,
        out_shape=jax.ShapeDtypeStruct((n_rows, d), F32),
        compiler_params=_cparams(("arbitrary",)),
        name="dispatch",
    )(pad_start, pad_n, n_used, pos3, h2)


def _expert_kernel(tile_e, first, n_used, x_ref, wg_ref, wu_ref, wd_ref, y_ref, wgb, wub, wdb):
    i = pl.program_id(0)
    n = n_used[0]

    @pl.when(i < n)
    def _():
        @pl.when(first[i] == 1)
        def _():
            wgb[...] = wg_ref[0].astype(BF16)
            wub[...] = wu_ref[0].astype(BF16)
            wdb[...] = wd_ref[0].astype(BF16)

        xb = x_ref[...].astype(BF16)
        hid = (_silu(_dot(xb, wgb[...])) * _dot(xb, wub[...])).astype(BF16)
        y_ref[...] = _dot(hid, wdb[...])

    @pl.when(i >= n)
    def _():
        y_ref[...] = jnp.zeros_like(y_ref)


def _experts(xs, tile_e, first, n_used, wg, wu, wd):
    n_rows, d = xs.shape
    n_tiles = n_rows // MOE_TILE
    de = wg.shape[2]
    grid_spec = pltpu.PrefetchScalarGridSpec(
        num_scalar_prefetch=3,
        grid=(n_tiles,),
        in_specs=[pl.BlockSpec((MOE_TILE, d), lambda i, te, fi, nu: (jnp.minimum(i, nu[0] - 1), 0)),
                  pl.BlockSpec((1, d, de), lambda i, te, fi, nu: (te[i], 0, 0)),
                  pl.BlockSpec((1, d, de), lambda i, te, fi, nu: (te[i], 0, 0)),
                  pl.BlockSpec((1, de, d), lambda i, te, fi, nu: (te[i], 0, 0))],
        out_specs=pl.BlockSpec((MOE_TILE, d), lambda i, te, fi, nu: (i, 0)),
        scratch_shapes=[pltpu.VMEM((d, de), BF16), pltpu.VMEM((d, de), BF16), pltpu.VMEM((de, d), BF16)])
    return pl.pallas_call(
        _expert_kernel,
        grid_spec=grid_spec,
        out_shape=jax.ShapeDtypeStruct((n_rows, d), F32),
        compiler_params=_cparams(("arbitrary",)),
        name="experts",
    )(tile_e, first, n_used, xs, wg, wu, wd)


def _combine_kernel(posc_ref, posn_ref, ys_hbm, acc_ref, w_ref, mod_ref, g_ref, o_ref, buf, sem):
    i = pl.program_id(0)
    n = pl.num_programs(0)
    tc = acc_ref.shape[0]

    def issue(pos_ref, slot):
        for k in range(TOP_K):
            def body(jj, carry):
                for u in range(8):
                    j = jj * 8 + u
                    row = pos_ref[0, k, j]
                    pltpu.make_async_copy(ys_hbm.at[pl.ds(row, 1), :], buf.at[slot, k, pl.ds(j, 1), :],
                                          sem.at[slot]).start()
                return carry
            lax.fori_loop(0, tc // 8, body, 0)

    @pl.when(i == 0)
    def _():
        issue(posc_ref, 0)

    @pl.when(i + 1 < n)
    def _():
        issue(posn_ref, (i + 1) % 2)

    slot = i % 2
    for k in range(TOP_K):
        pltpu.make_async_copy(ys_hbm.at[pl.ds(0, tc), :], buf.at[slot, k], sem.at[slot]).wait()
    w = w_ref[...]
    routed = w[:, 0:1] * buf[slot, 0]
    for k in range(1, TOP_K):
        routed = routed + w[:, k:k + 1] * buf[slot, k]
    x2 = acc_ref[...] + mod_ref[0, 5:6, :] * routed
    o_ref[...] = x2 * lax.rsqrt(jnp.mean(x2 * x2, axis=-1, keepdims=True) + RMS_EPS) * g_ref[...]


def _tile_pos(pos_t, tc):
    return pos_t.reshape(TOP_K, pos_t.shape[1] // tc, tc).transpose(1, 0, 2)


def _combine(ys, pos3, acc0, w_tok, mod3, g_final, seq):
    t, d = acc0.shape
    tc = pos3.shape[2]
    nt = t // tc
    return pl.pallas_call(
        _combine_kernel,
        grid=(nt,),
        in_specs=[pl.BlockSpec((1, TOP_K, tc), lambda i: (i, 0, 0), memory_space=pltpu.SMEM),
                  pl.BlockSpec((1, TOP_K, tc), lambda i: (jnp.minimum(i + 1, nt - 1), 0, 0),
                               memory_space=pltpu.SMEM),
                  pl.BlockSpec(memory_space=pl.ANY),
                  pl.BlockSpec((tc, d), lambda i: (i, 0)),
                  pl.BlockSpec((tc, TOP_K), lambda i: (i, 0)),
                  pl.BlockSpec((1, 6, d), lambda i: ((i * tc) // seq, 0, 0)),
                  pl.BlockSpec((1, d), lambda i: (0, 0))],
        out_specs=pl.BlockSpec((tc, d), lambda i: (i, 0)),
        out_shape=jax.ShapeDtypeStruct((t, d), F32),
        scratch_shapes=[pltpu.VMEM((2, TOP_K, tc, d), F32), pltpu.SemaphoreType.DMA((2,))],
        compiler_params=_cparams(("arbitrary",)),
        name="combine",
    )(pos3, pos3, ys, acc0, w_tok, mod3, g_final)


def kernel(x, c, positions, w_ada, b_ada, g_norm_mix, w_in, cmp_pos_k, cmp_w1_k, cmp_w2_k, cmp_pos_v,
           cmp_w1_v, cmp_w2_v, g_nsa_out, g_ret_out, w_out, g_norm_ffn, w_router, router_bias,
           w_gate_e, w_up_e, w_down_e, w_gate_s, w_up_s, w_down_s, g_norm_final):
    bsz, seq, d = x.shape
    t = bsz * seq
    x2 = x.reshape(t, d)
    cos_t, sin_t = _rope_tables(positions.reshape(t, 1).astype(F32))
    n_cmp_pad = seq // CMP_STRIDE

    def cmp_rows(tab):
        rows = tab[:, :LANES].reshape(bsz, seq, LANES)[:, CMP_BLOCK - 1::CMP_STRIDE]
        return jnp.pad(rows, ((0, 0), (0, n_cmp_pad - rows.shape[1]), (0, 0)))

    cos_c, sin_c = cmp_rows(cos_t), cmp_rows(sin_t)
    for l in range(w_in.shape[0]):
        mod3 = _ada(c, w_ada[l], b_ada[l]).reshape(bsz, 6, d)
        q, kc, vc, kk, gates, qr, kr, vr, gr, vsel_t, vwin_t = _inproj(
            x2, mod3, g_norm_mix[l].reshape(1, d), _arrange_w_in(w_in[l]), cos_t, sin_t, seq)
        kcmp, vcmp = _compress(kc, vc, cos_c, sin_c,
                               _arrange_cmp_weights(cmp_pos_k[l], cmp_w1_k[l], cmp_w2_k[l]),
                               _arrange_cmp_weights(cmp_pos_v[l], cmp_w1_v[l], cmp_w2_v[l]), bsz, seq)
        onsa = _nsa(q, kcmp, vcmp, kk, vsel_t, vwin_t, gates, g_nsa_out[l].reshape(1, -1), bsz, seq)
        oret = _retention(qr, kr, vr, gr, g_ret_out[l], bsz, seq)
        acc0, h2, idx_t, w_t = _post(x2, onsa, oret, mod3, w_out[l], g_norm_ffn[l].reshape(1, d),
                                     w_router[l], router_bias[l], w_gate_s[l], w_up_s[l], w_down_s[l], seq)
        pos_t, tile_e, first, n_used, pad_start, pad_n = _route_plan(idx_t, t)
        n_rows = TOP_K * t + N_EXPERTS * MOE_TILE
        xs = _dispatch(h2, _tile_pos(pos_t, min(256, seq)), pad_start, pad_n, n_used, n_rows)
        ys = _experts(xs, tile_e, first, n_used, w_gate_e[l], w_up_e[l], w_down_e[l])
        last = l == w_in.shape[0] - 1
        gfin = g_norm_final.reshape(1, d)
        x2 = _combine(ys, _tile_pos(pos_t, 128), acc0, w_t.T, mod3, gfin, seq)
        assert last, "final norm is fused into the combine stage; depth 1 only"
    return x2.reshape(bsz, seq, d)
```

```python
import functools
import math

import numpy as np
import jax
import jax.numpy as jnp
from jax import lax
from jax.experimental import pallas as pl
from jax.experimental.pallas import tpu as pltpu

F32 = jnp.float32
BF16 = jnp.bfloat16
I32 = jnp.int32
HIGHEST = lax.Precision.HIGHEST

LANES = 128
SUBLANES = 8
NSA_HEAD_DIM = 64
NSA_HEADS = 8
NSA_GROUPS = 2
NSA_Q_PER_KV = NSA_HEADS // NSA_GROUPS
CMP_BLOCK = 32
CMP_STRIDE = 16
CMP_HIDDEN = 128
SEL_BLOCK = 64
SEL_TOP = 8
WINDOW = 512
Q_BLOCK = 128
FORCE_BONUS = 1.0e4
RET_HEADS = 4
RET_HEAD_DIM = 128
RET_CHUNK = 128
ROPE_THETA = 10000.0
RMS_EPS = 1e-6
N_EXPERTS = 256
N_EXPERT_GROUPS = 8
TOPK_GROUPS = 4
TOP_K = 8
ROUTED_SCALE = 2.5
MOE_TILE = 256
SEL_CHUNK = 256
MASK_BIG = 2.0 ** 100
MASK_FLOOR = 2.0 ** 99
NEG = -1e30
VMEM_LIMIT = 48 * 1024 * 1024


def _cparams(sem):
    return pltpu.CompilerParams(dimension_semantics=sem, vmem_limit_bytes=VMEM_LIMIT)


def _dot(a, b, **kw):
    return jnp.dot(a, b, preferred_element_type=F32, **kw)


def _dot_nt(a, b, **kw):
    return lax.dot_general(a, b, (((1,), (1,)), ((), ())), preferred_element_type=F32, **kw)


def _sigmoid(v):
    return 1.0 / (1.0 + jnp.exp(-v))


def _silu(v):
    return v * _sigmoid(v)


def _ada_kernel(c_ref, w_ref, b_ref, o_ref):
    o_ref[...] = _dot(_silu(c_ref[...]), w_ref[...], precision=HIGHEST) + b_ref[...]


def _ada(c, w, b):
    bsz, d = c.shape
    n = w.shape[1]
    tn = 1536
    cp = jnp.zeros((8, d), F32).at[:bsz].set(c)
    out = pl.pallas_call(
        _ada_kernel,
        grid=(n // tn,),
        in_specs=[pl.BlockSpec((8, d), lambda j: (0, 0)),
                  pl.BlockSpec((d, tn), lambda j: (0, j)),
                  pl.BlockSpec((1, tn), lambda j: (0, j))],
        out_specs=pl.BlockSpec((8, tn), lambda j: (0, j)),
        out_shape=jax.ShapeDtypeStruct((8, n), F32),
        compiler_params=_cparams(("arbitrary",)),
        name="ada",
    )(cp, w, b.reshape(1, n))
    return out[:bsz]


def _rope_table_kernel(pos_ref, inv_ref, sgn_ref, cos_ref, sin_ref):
    ang = pos_ref[...] * inv_ref[...]
    cos_ref[...] = jnp.cos(ang)
    sin_ref[...] = jnp.sin(ang) * sgn_ref[...]


def _rope_tables(posf):
    t = posf.shape[0]
    inv_n = ROPE_THETA ** (-jnp.arange(0, NSA_HEAD_DIM, 2, dtype=F32) / NSA_HEAD_DIM)
    inv_r = ROPE_THETA ** (-jnp.arange(0, RET_HEAD_DIM, 2, dtype=F32) / RET_HEAD_DIM)
    inv = jnp.concatenate([jnp.tile(inv_n, 4), jnp.tile(inv_r, 2)]).reshape(1, 2 * LANES)
    sgn_n = np.where((np.arange(LANES) % NSA_HEAD_DIM) < NSA_HEAD_DIM // 2, -1.0, 1.0)
    sgn_r = np.where(np.arange(LANES) < RET_HEAD_DIM // 2, -1.0, 1.0)
    sgn = jnp.asarray(np.concatenate([sgn_n, sgn_r]).reshape(1, 2 * LANES), F32)
    tm = min(t, 1024)
    return pl.pallas_call(
        _rope_table_kernel,
        grid=(t // tm,),
        in_specs=[pl.BlockSpec((tm, 1), lambda i: (i, 0)),
                  pl.BlockSpec((1, 2 * LANES), lambda i: (0, 0)),
                  pl.BlockSpec((1, 2 * LANES), lambda i: (0, 0))],
        out_specs=[pl.BlockSpec((tm, 2 * LANES), lambda i: (i, 0))] * 2,
        out_shape=[jax.ShapeDtypeStruct((t, 2 * LANES), F32)] * 2,
        compiler_params=_cparams(("arbitrary",)),
        name="rope_tables",
    )(posf, inv, sgn)


def _rope64(p, cos, sin, first_half):
    rot = jnp.where(first_half, pltpu.roll(p, 96, 1), pltpu.roll(p, 32, 1))
    return p * cos + rot * sin


def _rope128(p, cos, sin):
    return p * cos + pltpu.roll(p, 64, 1) * sin


_C_Q = 0
_C_KV = _C_Q + NSA_HEADS * LANES
_C_GATE = _C_KV + 6 * LANES
_C_RET = _C_GATE + LANES
_C_END = _C_RET + 4 * RET_HEADS * RET_HEAD_DIM


def _arrange_w_in(w_in):
    d = w_in.shape[0]
    nw = NSA_HEADS * NSA_HEAD_DIM
    q = w_in[:, :nw].reshape(d, NSA_HEADS, NSA_HEAD_DIM)
    z = jnp.zeros_like(q)
    grp = (jnp.arange(NSA_HEADS) // NSA_Q_PER_KV)[None, :, None]
    qpad = jnp.where(grp == 0, jnp.concatenate([q, z], -1), jnp.concatenate([z, q], -1))
    qpad = qpad.reshape(d, NSA_HEADS * LANES)
    kv = w_in[:, nw:nw + 6 * LANES]
    g0 = nw + 6 * LANES
    gate = jnp.pad(w_in[:, g0:g0 + 3 * NSA_HEADS], ((0, 0), (0, LANES - 3 * NSA_HEADS)))
    ret = w_in[:, g0 + 3 * NSA_HEADS:]
    return jnp.concatenate([qpad, kv, gate, ret], axis=1).astype(BF16)


def _inproj_kernel(x_ref, mod_ref, g_ref, w_ref, cos_ref, sin_ref,
                   q_ref, kc_ref, vc_ref, kk_ref, gate_ref, qr_ref, kr_ref, vr_ref, gr_ref, vst_ref, vwt_ref):
    x = x_ref[...]
    tm = x.shape[0]
    ms = jnp.mean(x * x, axis=-1, keepdims=True)
    y = x * lax.rsqrt(ms + RMS_EPS) * g_ref[...]
    h = y * (1.0 + mod_ref[0, 1:2, :]) + mod_ref[0, 0:1, :]
    hb = h.astype(BF16)
    cos_n, sin_n = cos_ref[:, 0:LANES], sin_ref[:, 0:LANES]
    cos_r, sin_r = cos_ref[:, LANES:], sin_ref[:, LANES:]
    lane = lax.broadcasted_iota(I32, (tm, LANES), 1)
    first_half = (lane % NSA_HEAD_DIM) < (NSA_HEAD_DIM // 2)
    scale_n = NSA_HEAD_DIM ** -0.5
    scale_r = RET_HEAD_DIM ** -0.5

    def proj(c0, n):
        return _dot(hb, w_ref[:, c0:c0 + n])

    for hh in range(NSA_HEADS):
        p = proj(_C_Q + hh * LANES, LANES)
        q_ref[:, hh * LANES:(hh + 1) * LANES] = (_rope64(p, cos_n, sin_n, first_half) * scale_n).astype(BF16)
    kv = proj(_C_KV, 6 * LANES)
    kc_ref[...] = kv[:, 0:LANES].astype(BF16)
    vc_ref[...] = kv[:, LANES:2 * LANES].astype(BF16)
    kk_ref[:, 0:LANES] = _rope64(kv[:, 2 * LANES:3 * LANES], cos_n, sin_n, first_half).astype(BF16)
    kk_ref[:, LANES:2 * LANES] = _rope64(kv[:, 4 * LANES:5 * LANES], cos_n, sin_n, first_half).astype(BF16)
    group0 = lane < NSA_HEAD_DIM
    for c0, vt_ref, chunk in ((3 * LANES, vst_ref, SEL_CHUNK), (5 * LANES, vwt_ref, Q_BLOCK)):
        v = kv[:, c0:c0 + LANES]
        for g, vg in enumerate((jnp.where(group0, v, 1.0), jnp.where(group0, 1.0, v))):
            for cc in range(tm // chunk):
                vt_ref[cc, g] = vg[cc * chunk:(cc + 1) * chunk].T.astype(BF16)
    gate_ref[...] = _sigmoid(proj(_C_GATE, LANES))
    rw = RET_HEADS * RET_HEAD_DIM
    for hh in range(RET_HEADS):
        sl = slice(hh * LANES, (hh + 1) * LANES)
        pq = proj(_C_RET + hh * LANES, LANES)
        qr_ref[:, sl] = _rope128(pq, cos_r, sin_r).astype(BF16)
        pk = proj(_C_RET + rw + hh * LANES, LANES)
        kr_ref[:, sl] = (_rope128(pk, cos_r, sin_r) * scale_r).astype(BF16)
    vr_ref[...] = proj(_C_RET + 2 * rw, rw).astype(BF16)
    gr_ref[...] = proj(_C_RET + 3 * rw, rw)


def _inproj(x2, mod3, g_mix, w_in_p, cos_t, sin_t, seq):
    t, d = x2.shape
    tm = min(512, seq)
    rw = RET_HEADS * RET_HEAD_DIM
    row = lambda n: pl.BlockSpec((tm, n), lambda i: (i, 0))
    outs = [(NSA_HEADS * LANES, BF16), (LANES, BF16), (LANES, BF16), (2 * LANES, BF16), (LANES, F32),
            (rw, BF16), (rw, BF16), (rw, BF16), (rw, F32)]
    vt_specs, vt_shapes = [], []
    for chunk in (SEL_CHUNK, Q_BLOCK):
        vt_specs.append(pl.BlockSpec((tm // chunk, NSA_GROUPS, LANES, chunk), lambda i: (i, 0, 0, 0)))
        vt_shapes.append(jax.ShapeDtypeStruct((t // chunk, NSA_GROUPS, LANES, chunk), BF16))
    return pl.pallas_call(
        _inproj_kernel,
        grid=(t // tm,),
        in_specs=[row(d),
                  pl.BlockSpec((1, 6, d), lambda i: ((i * tm) // seq, 0, 0)),
                  pl.BlockSpec((1, d), lambda i: (0, 0)),
                  pl.BlockSpec((d, _C_END), lambda i: (0, 0)),
                  row(2 * LANES), row(2 * LANES)],
        out_specs=[row(n) for n, _ in outs] + vt_specs,
        out_shape=[jax.ShapeDtypeStruct((t, n), dt) for n, dt in outs] + vt_shapes,
        compiler_params=_cparams(("arbitrary",)),
        name="inproj",
    )(x2, mod3, g_mix, w_in_p, cos_t, sin_t)


def _gelu_tanh(v):
    return v * (0.5 * (1.0 + jnp.tanh(math.sqrt(2.0 / math.pi) * (v + 0.044715 * (v * v * v)))))


def _compress_kernel(kc_ref, vc_ref, wkt_ref, wkb_ref, wk2_ref, pk_ref, wvt_ref, wvb_ref, wv2_ref, pv_ref,
                     cos_ref, sin_ref, ko_ref, vo_ref):
    def one(x_ref, wt_ref, wb_ref, w2_ref, p_ref):
        xx = x_ref[0]
        a = _dot(xx, wt_ref[...])
        b = _dot(xx, wb_ref[...])
        pb = p_ref[...].astype(BF16)
        bias = _dot(pb, wt_ref[...])[0:1] + _dot(pb, wb_ref[...])[1:2]
        hid = a + pltpu.roll(b, b.shape[0] - 1, 0) + bias
        return _dot(_gelu_tanh(hid).astype(BF16), w2_ref[...])

    k = one(kc_ref, wkt_ref, wkb_ref, wk2_ref, pk_ref)
    lane = lax.broadcasted_iota(I32, k.shape, 1)
    first_half = (lane % NSA_HEAD_DIM) < (NSA_HEAD_DIM // 2)
    ko_ref[0] = _rope64(k, cos_ref[0], sin_ref[0], first_half).astype(BF16)
    v = one(vc_ref, wvt_ref, wvb_ref, wv2_ref, pv_ref)
    group0 = lane < NSA_HEAD_DIM
    vo_ref[0, 0] = jnp.where(group0, v, 1.0).T.astype(BF16)
    vo_ref[0, 1] = jnp.where(group0, 1.0, v).T.astype(BF16)


def _arrange_cmp_weights(pos, w1, w2):
    half = CMP_BLOCK // 2
    dh = NSA_HEAD_DIM
    w1r = w1.reshape(CMP_BLOCK, dh, CMP_HIDDEN)

    def block(wpart):
        z = jnp.zeros_like(wpart)
        g0 = jnp.concatenate([wpart, z], axis=-1)
        g1 = jnp.concatenate([z, wpart], axis=-1)
        return jnp.stack([g0, g1], axis=1).reshape(half * 2 * dh, 2 * CMP_HIDDEN)

    wt, wb = block(w1r[:half]), block(w1r[half:])
    z2 = jnp.zeros_like(w2)
    w2b = jnp.concatenate([jnp.concatenate([w2, z2], 1), jnp.concatenate([z2, w2], 1)], 0)
    ptop = jnp.tile(pos[:half], (1, 2)).reshape(1, -1)
    pbot = jnp.tile(pos[half:], (1, 2)).reshape(1, -1)
    prow = jnp.concatenate([ptop, pbot, jnp.zeros((6, ptop.shape[1]), F32)], 0)
    return wt.astype(BF16), wb.astype(BF16), w2b.astype(BF16), prow


def _compress(kc, vc, cos_c, sin_c, kparams, vparams, bsz, seq):
    nchunk = seq // CMP_STRIDE
    width = CMP_STRIDE * LANES
    kc16 = kc.reshape(bsz, nchunk, width)
    vc16 = vc.reshape(bsz, nchunk, width)
    full = lambda a: pl.BlockSpec(a.shape, lambda b: (0,) * a.ndim)
    per_b = lambda n: pl.BlockSpec((1, nchunk, n), lambda b: (b, 0, 0))
    return pl.pallas_call(
        _compress_kernel,
        grid=(bsz,),
        in_specs=[per_b(width), per_b(width)] + [full(a) for a in kparams] + [full(a) for a in vparams]
                 + [per_b(LANES), per_b(LANES)],
        out_specs=[per_b(LANES), pl.BlockSpec((1, NSA_GROUPS, LANES, nchunk), lambda b: (b, 0, 0, 0))],
        out_shape=[jax.ShapeDtypeStruct((bsz, nchunk, LANES), BF16),
                   jax.ShapeDtypeStruct((bsz, NSA_GROUPS, LANES, nchunk), BF16)],
        compiler_params=_cparams(("arbitrary",)),
        name="compress",
    )(kc16, vc16, *kparams, *vparams, cos_c, sin_c)


def _softmax_chunk(m, acc, s, vt):
    m_new = jnp.maximum(m, jnp.max(s, axis=0, keepdims=True))
    e = jnp.exp(s - m_new).astype(BF16)
    acc = jnp.exp(m - m_new) * acc + _dot(vt, e)
    return m_new, acc


def _nsa_kernel(q_ref, kc_ref, vct_ref, ks_ref, kw_ref, vst_ref, vwt_ref, gate_ref,
                ovt_ref, exp_ref, g_ref, o_ref, *, seq):
    qb = pl.program_id(1)
    t0 = qb * Q_BLOCK
    cols = NSA_Q_PER_KV * Q_BLOCK
    n_cmp_pad = kc_ref.shape[1]
    n_sel = seq // SEL_BLOCK
    t_row = t0 + lax.broadcasted_iota(I32, (1, Q_BLOCK), 1)
    gates_t = gate_ref[...].T
    lane = lax.broadcasted_iota(I32, (Q_BLOCK, LANES), 1)
    key = lax.broadcasted_iota(I32, (Q_BLOCK, Q_BLOCK), 0)
    tok = lax.broadcasted_iota(I32, (Q_BLOCK, Q_BLOCK), 1)
    cend = lax.broadcasted_iota(I32, (n_cmp_pad, 1), 0) * CMP_STRIDE + (CMP_BLOCK - 1)
    bias_cmp = jnp.where(cend <= t_row, 0.0, -MASK_BIG)
    last_c = qb // (SEL_CHUNK // Q_BLOCK)
    kpos_last = last_c * SEL_CHUNK + lax.broadcasted_iota(I32, (SEL_CHUNK, 1), 0)
    bias_diag = jnp.where(kpos_last <= t_row, 0.0, -MASK_BIG)
    n_win = WINDOW // Q_BLOCK
    bias_win_first = jnp.where(key > tok, 0.0, -MASK_BIG)
    bias_win_last = jnp.where(key <= tok, 0.0, -MASK_BIG)
    init = (jnp.full((1, cols), -MASK_FLOOR, F32), jnp.zeros((LANES, cols), F32))
    tile_heads = lambda b: jnp.concatenate([b] * NSA_Q_PER_KV, axis=1)

    def finish(acc, g):
        ones_row = NSA_HEAD_DIM * (1 - g)
        return acc * (1.0 / jnp.maximum(acc[ones_row:ones_row + 1, :], 1e-20))

    q4s, o_cs, sel_bs = [], [], []
    for g in range(NSA_GROUPS):
        q4 = jnp.concatenate([q_ref[:, (NSA_Q_PER_KV * g + r) * LANES:(NSA_Q_PER_KV * g + r + 1) * LANES]
                              for r in range(NSA_Q_PER_KV)], axis=0)
        q4s.append(q4)
        s_c = _dot_nt(kc_ref[0], q4) + tile_heads(bias_cmp)
        e_c = jnp.exp(s_c - jnp.maximum(jnp.max(s_c, axis=0, keepdims=True), -MASK_FLOOR))
        p_c = e_c * (1.0 / jnp.maximum(jnp.sum(e_c, axis=0, keepdims=True), 1e-20))
        psum = p_c[:, 0:Q_BLOCK]
        for r in range(1, NSA_Q_PER_KV):
            psum = psum + p_c[:, r * Q_BLOCK:(r + 1) * Q_BLOCK]
        o_cs.append(finish(_dot(vct_ref[0, g], e_c.astype(BF16)), g))
        imp_t = _dot(ovt_ref[...], psum, precision=HIGHEST)
        nsp = imp_t.shape[0]
        jrow = lax.broadcasted_iota(I32, (nsp, 1), 0)
        cur = t_row // SEL_BLOCK
        valid = (jrow * SEL_BLOCK <= t_row) & (jrow < n_sel)
        forced = (jrow == 0) | (jrow == cur) | (jrow == cur - 1)
        val = jnp.where(valid, imp_t + jnp.where(forced, FORCE_BONUS, 0.0), -1.0)
        val = jnp.where(jrow < n_sel, val, -jnp.inf)
        sel_t = jnp.zeros((nsp, Q_BLOCK), F32)
        for _ in range(min(SEL_TOP, n_sel)):
            mx = jnp.max(val, axis=0, keepdims=True)
            jmin = jnp.min(jnp.where(val == mx, jrow, nsp), axis=0, keepdims=True)
            hit = jrow == jmin
            sel_t = jnp.where(hit, 1.0, sel_t)
            val = jnp.where(hit, -jnp.inf, val)
        sel_bs.append(jnp.concatenate([sel_t[:n_sel], jnp.ones((SUBLANES, Q_BLOCK), F32),
                                       jnp.zeros((LANES - n_sel - SUBLANES, Q_BLOCK), F32)], axis=0).astype(BF16))

    def sel_scores(c):
        keys = ks_ref[pl.ds(pl.multiple_of(c * SEL_CHUNK, SEL_CHUNK), SEL_CHUNK), :]
        expand = exp_ref[c]
        return tuple(_dot_nt(keys, q4s[g]) + tile_heads(_dot(expand, sel_bs[g])) for g in range(NSA_GROUPS))

    def sel_reduce(c, state, scores):
        out = []
        for g in range(NSA_GROUPS):
            out.extend(_softmax_chunk(state[2 * g], state[2 * g + 1], scores[g], vst_ref[c, g]))
        return tuple(out)

    def sel_step(c, carry):
        state, scores = carry
        return sel_reduce(c, state, scores), sel_scores(c + 1)

    state, scores = lax.fori_loop(0, last_c, sel_step, (init * NSA_GROUPS, sel_scores(0)))
    diag = tile_heads(bias_diag)
    state = sel_reduce(last_c, state, tuple(s + diag for s in scores))
    o_ss = [finish(state[2 * g + 1], g) for g in range(NSA_GROUPS)]
    s_ws = [[] for _ in range(NSA_GROUPS)]
    kbs = []
    for w in range(n_win + 1):
        wb = qb - n_win + w
        kb = jnp.maximum(wb, 0)
        kbs.append(kb)
        keys = kw_ref[pl.ds(pl.multiple_of(kb * Q_BLOCK, Q_BLOCK), Q_BLOCK), :]
        if w == n_win:
            bias = bias_win_last
        else:
            before_start = jnp.where(wb >= 0, 0.0, -MASK_BIG)
            bias = (bias_win_first + before_start) if w == 0 else jnp.full((Q_BLOCK, Q_BLOCK), before_start)
        for g in range(NSA_GROUPS):
            s_ws[g].append(_dot_nt(keys, q4s[g]) + tile_heads(bias))
    o_ws = []
    for g in range(NSA_GROUPS):
        m_w = jnp.full((1, cols), -MASK_FLOOR, F32)
        for s in s_ws[g]:
            m_w = jnp.maximum(m_w, jnp.max(s, axis=0, keepdims=True))
        acc = None
        for kb, s in zip(kbs, s_ws[g]):
            pv = _dot(vwt_ref[kb, g], jnp.exp(s - m_w).astype(BF16))
            acc = pv if acc is None else acc + pv
        o_ws.append(finish(acc, g))
    heads_out = []
    for g in range(NSA_GROUPS):
        o_c, o_s, o_w = o_cs[g], o_ss[g], o_ws[g]
        for r in range(NSA_Q_PER_KV):
            hh = NSA_Q_PER_KV * g + r
            cs = slice(r * Q_BLOCK, (r + 1) * Q_BLOCK)
            o = (gates_t[3 * hh:3 * hh + 1, :] * o_c[:, cs] + gates_t[3 * hh + 1:3 * hh + 2, :] * o_s[:, cs]
                 + gates_t[3 * hh + 2:3 * hh + 3, :] * o_w[:, cs]).T
            if (hh % 2) != g:
                o = pltpu.roll(o, NSA_HEAD_DIM, 1)
            heads_out.append(o)
    blocks = [jnp.where(lane < NSA_HEAD_DIM, heads_out[2 * i], heads_out[2 * i + 1])
              for i in range(NSA_HEADS // 2)]
    ss = sum(jnp.sum(b * b, axis=-1, keepdims=True) for b in blocks)
    inv = lax.rsqrt(ss / (NSA_HEADS * NSA_HEAD_DIM) + RMS_EPS)
    for i, b in enumerate(blocks):
        sl = slice(i * LANES, (i + 1) * LANES)
        o_ref[:, sl] = (b * inv * g_ref[:, sl]).astype(BF16)


def _nsa(q, kcmp, vcmp_t, kk, vsel_t, vwin_t, gates, g_nsa, bsz, seq):
    t = bsz * seq
    nq = seq // Q_BLOCK
    n_cmp_pad = seq // CMP_STRIDE
    n_cmp = (seq - CMP_BLOCK) // CMP_STRIDE + 1
    n_sel = seq // SEL_BLOCK
    assert n_sel % SUBLANES == 0 and n_sel + SUBLANES <= LANES, "selection mask needs a spare expansion row"
    nsp = n_sel
    cs = np.arange(n_cmp_pad) * CMP_STRIDE
    ss = np.arange(nsp) * SEL_BLOCK
    ov = ((cs[None, :] < ss[:, None] + SEL_BLOCK) & (cs[None, :] + CMP_BLOCK > ss[:, None])
          & (np.arange(n_cmp_pad)[None, :] < n_cmp) & (np.arange(nsp)[:, None] < n_sel))
    ovt = jnp.asarray(ov, F32)
    nch = seq // SEL_CHUNK
    kp = np.arange(seq).reshape(nch, SEL_CHUNK, 1)
    col = np.arange(LANES).reshape(1, 1, LANES)
    expand = jnp.asarray(np.where(kp // SEL_BLOCK == col, MASK_BIG, 0.0) + np.where(col == n_sel, -MASK_BIG, 0.0),
                         BF16)
    seqcol = lambda c: pl.BlockSpec((seq, LANES), lambda b, i: (b, c))
    per_b = lambda a: pl.BlockSpec((a.shape[0] // bsz,) + a.shape[1:], lambda b, i: (b,) + (0,) * (a.ndim - 1))
    return pl.pallas_call(
        functools.partial(_nsa_kernel, seq=seq),
        grid=(bsz, nq),
        in_specs=[pl.BlockSpec((Q_BLOCK, NSA_HEADS * LANES), lambda b, i: (b * nq + i, 0)),
                  per_b(kcmp), per_b(vcmp_t), seqcol(0), seqcol(1), per_b(vsel_t), per_b(vwin_t),
                  pl.BlockSpec((Q_BLOCK, LANES), lambda b, i: (b * nq + i, 0)),
                  pl.BlockSpec(ovt.shape, lambda b, i: (0, 0)),
                  pl.BlockSpec(expand.shape, lambda b, i: (0, 0, 0)),
                  pl.BlockSpec((1, NSA_HEADS * NSA_HEAD_DIM), lambda b, i: (0, 0))],
        out_specs=pl.BlockSpec((Q_BLOCK, NSA_HEADS * NSA_HEAD_DIM), lambda b, i: (b * nq + i, 0)),
        out_shape=jax.ShapeDtypeStruct((t, NSA_HEADS * NSA_HEAD_DIM), BF16),
        compiler_params=_cparams(("arbitrary", "arbitrary")),
        name="nsa",
    )(q, kcmp, vcmp_t, kk, kk, vsel_t, vwin_t, gates, ovt, expand, g_nsa)


def _ret_kernel(q_ref, k_ref, v_ref, gr_ref, dec_ref, xi_ref, zeta_ref, gch_ref, g_ref, o_ref, st_ref):
    @pl.when(pl.program_id(1) == 0)
    def _():
        st_ref[...] = jnp.zeros_like(st_ref)

    for hh in range(RET_HEADS):
        sl = slice(hh * LANES, (hh + 1) * LANES)
        q, k, v = q_ref[:, sl], k_ref[:, sl], v_ref[:, sl]
        sc = _dot_nt(q, k) * dec_ref[hh]
        inner = _dot(sc.astype(BF16), v)
        st = st_ref[hh]
        cross = _dot((q.astype(F32) * xi_ref[hh]).astype(BF16), st.astype(BF16))
        kz = (k.astype(F32) * zeta_ref[hh]).T.astype(BF16)
        st_ref[hh] = st * gch_ref[hh] + _dot(kz, v)
        o = inner + cross
        y = o * lax.rsqrt(jnp.mean(o * o, axis=-1, keepdims=True) + RMS_EPS) * g_ref[hh:hh + 1, :]
        o_ref[:, sl] = (_silu(gr_ref[:, sl]) * y).astype(BF16)


def _retention(qr, kr, vr, gr, g_ret, bsz, seq):
    t = bsz * seq
    c = RET_CHUNK
    n = seq // c
    log_g = jnp.log(1.0 - 2.0 ** (-5.0 - jnp.arange(RET_HEADS, dtype=F32)))
    i = jnp.arange(c, dtype=F32)
    diff = i[:, None] - i[None, :]
    causal = diff >= 0
    dec = jnp.where(causal, jnp.exp(log_g[:, None, None] * jnp.where(causal, diff, 0.0)), 0.0)
    xi = jnp.broadcast_to(jnp.exp(log_g[:, None] * (i + 1.0))[:, :, None], (RET_HEADS, c, LANES))
    zeta = jnp.broadcast_to(jnp.exp(log_g[:, None] * (c - 1.0 - i))[:, :, None], (RET_HEADS, c, LANES))
    gch = jnp.broadcast_to(jnp.exp(log_g * c)[:, None, None], (RET_HEADS, 1, LANES))
    w = RET_HEADS * RET_HEAD_DIM
    row = pl.BlockSpec((c, w), lambda b, j: (b * n + j, 0))
    full = lambda a: pl.BlockSpec(a.shape, lambda b, j: (0,) * a.ndim)
    return pl.pallas_call(
        _ret_kernel,
        grid=(bsz, n),
        in_specs=[row, row, row, row, full(dec), full(xi), full(zeta), full(gch), full(g_ret)],
        out_specs=row,
        out_shape=jax.ShapeDtypeStruct((t, w), BF16),
        scratch_shapes=[pltpu.VMEM((RET_HEADS, RET_HEAD_DIM, RET_HEAD_DIM), F32)],
        compiler_params=_cparams(("arbitrary", "arbitrary")),
        name="retention",
    )(qr, kr, vr, gr, dec, xi, zeta, gch, g_ret)


def _post_kernel(x_ref, onsa_ref, oret_ref, mod_ref, wo1_ref, wo2_ref, gffn_ref, wrt_ref, rb_ref,
                 wgs_ref, wus_ref, wds_ref, acc_ref, h2_ref, idx_ref, wt_ref):
    mix = _dot(onsa_ref[...], wo1_ref[...]) + _dot(oret_ref[...], wo2_ref[...])
    x1 = x_ref[...] + mod_ref[0, 2:3, :] * mix
    ms = jnp.mean(x1 * x1, axis=-1, keepdims=True)
    h2 = x1 * lax.rsqrt(ms + RMS_EPS) * gffn_ref[...] * (1.0 + mod_ref[0, 4:5, :]) + mod_ref[0, 3:4, :]
    _rows_to_tiles(h2_ref, h2)
    hb = h2.astype(BF16)
    hid = (_silu(_dot(hb, wgs_ref[...])) * _dot(hb, wus_ref[...])).astype(BF16)
    acc_ref[...] = x1 + mod_ref[0, 5:6, :] * _dot(hid, wds_ref[...])
    s = _sigmoid(_dot_nt(wrt_ref[...], h2, precision=HIGHEST))
    sb = s + rb_ref[...]
    per = N_EXPERTS // N_EXPERT_GROUPS
    ridx = lax.broadcasted_iota(I32, (per, 1), 0)
    blks, grp = [], []
    for gi in range(N_EXPERT_GROUPS):
        blk = sb[gi * per:(gi + 1) * per]
        m1 = jnp.max(blk, axis=0, keepdims=True)
        first = jnp.min(jnp.where(blk == m1, ridx, per), axis=0, keepdims=True)
        m2 = jnp.max(jnp.where(ridx == first, -jnp.inf, blk), axis=0, keepdims=True)
        blks.append(blk)
        grp.append(m1 + m2)
    masked = []
    for gi in range(N_EXPERT_GROUPS):
        rank = jnp.zeros_like(grp[gi])
        for gj in range(N_EXPERT_GROUPS):
            if gj < gi:
                rank = rank + (grp[gj] >= grp[gi]).astype(F32)
            elif gj > gi:
                rank = rank + (grp[gj] > grp[gi]).astype(F32)
        masked.append(jnp.where(rank < TOPK_GROUPS, blks[gi], NEG))
    val = jnp.concatenate(masked, axis=0)
    eidx = lax.broadcasted_iota(I32, (N_EXPERTS, 1), 0)
    ids, ws = [], []
    for _ in range(TOP_K):
        mx = jnp.max(val, axis=0, keepdims=True)
        emin = jnp.min(jnp.where(val == mx, eidx, N_EXPERTS), axis=0, keepdims=True)
        hit = eidx == emin
        ids.append(emin)
        ws.append(jnp.sum(jnp.where(hit, s, 0.0), axis=0, keepdims=True))
        val = jnp.where(hit, -jnp.inf, val)
    wsum = ws[0]
    for wk in ws[1:]:
        wsum = wsum + wk
    idx_ref[...] = jnp.concatenate(ids, axis=0)
    wt_ref[...] = jnp.concatenate(ws, axis=0) / wsum * ROUTED_SCALE


def _post(x2, onsa, oret, mod3, w_out, g_ffn, w_router, router_bias, wgs, wus, wds, seq):
    t, d = x2.shape
    tm = min(512, seq)
    hw = onsa.shape[1]
    wo1 = w_out[:hw].astype(BF16)
    wo2 = w_out[hw:].astype(BF16)
    wrt = w_router.T
    rb = jnp.broadcast_to(router_bias.reshape(N_EXPERTS, 1), (N_EXPERTS, tm))
    row = lambda n: pl.BlockSpec((tm, n), lambda i: (i, 0))
    full = lambda a: pl.BlockSpec(a.shape, lambda i: (0,) * a.ndim)
    col = pl.BlockSpec((TOP_K, tm), lambda i: (0, i))
    ops = (wo1, wo2, g_ffn, wrt, rb, wgs.astype(BF16), wus.astype(BF16), wds.astype(BF16))
    return pl.pallas_call(
        _post_kernel,
        grid=(t // tm,),
        in_specs=[row(d), row(hw), row(oret.shape[1]),
                  pl.BlockSpec((1, 6, d), lambda i: ((i * tm) // seq, 0, 0))] + [full(a) for a in ops],
        out_specs=[row(d), pl.BlockSpec((tm * SUBLANES, LANES), lambda i: (i, 0)), col, col],
        out_shape=[jax.ShapeDtypeStruct((t, d), F32), jax.ShapeDtypeStruct((t * SUBLANES, LANES), F32),
                   jax.ShapeDtypeStruct((TOP_K, t), I32), jax.ShapeDtypeStruct((TOP_K, t), F32)],
        compiler_params=_cparams(("arbitrary",)),
        name="post",
    )(x2, onsa, oret, mod3, *ops)


def _expert_select(idx_row, table, eidx):
    return jnp.sum(jnp.where(eidx == idx_row, table, 0.0), axis=0, keepdims=True)


def _rank_kernel(idx_ref, tri_ref, rank_ref, cnt_ref, carry):
    @pl.when(pl.program_id(0) == 0)
    def _():
        carry[...] = jnp.zeros_like(carry)

    idx = idx_ref[...]
    eidx = lax.broadcasted_iota(I32, (N_EXPERTS, 1), 0)
    member = jnp.zeros((N_EXPERTS, idx.shape[1]), F32)
    for k in range(TOP_K):
        member = member + (eidx == idx[k:k + 1, :]).astype(F32)
    before = _dot(member.astype(BF16), tri_ref[...]) + carry[:, 0:1]
    rank_ref[...] = jnp.concatenate(
        [_expert_select(idx[k:k + 1, :], before, eidx) for k in range(TOP_K)], axis=0).astype(I32)
    carry[...] = carry[...] + jnp.sum(member, axis=1, keepdims=True)
    cnt_ref[...] = carry[...]


def _dest_kernel(idx_ref, rank_ref, start_ref, pos_ref):
    idx = idx_ref[...]
    eidx = lax.broadcasted_iota(I32, (N_EXPERTS, 1), 0)
    start = start_ref[:, 0:1]
    base = jnp.concatenate([_expert_select(idx[k:k + 1, :], start, eidx) for k in range(TOP_K)], axis=0)
    pos_ref[...] = rank_ref[...] + base.astype(I32)


def _route_plan(idx_t, n_tok):
    tm = min(512, n_tok)
    tri = jnp.asarray(np.triu(np.ones((tm, tm), np.float32), 1), BF16)
    col = pl.BlockSpec((TOP_K, tm), lambda i: (0, i))
    rank, cnt = pl.pallas_call(
        _rank_kernel,
        grid=(n_tok // tm,),
        in_specs=[col, pl.BlockSpec((tm, tm), lambda i: (0, 0))],
        out_specs=[col, pl.BlockSpec((N_EXPERTS, LANES), lambda i: (0, 0))],
        out_shape=[jax.ShapeDtypeStruct((TOP_K, n_tok), I32), jax.ShapeDtypeStruct((N_EXPERTS, LANES), F32)],
        scratch_shapes=[pltpu.VMEM((N_EXPERTS, LANES), F32)],
        compiler_params=_cparams(("arbitrary",)),
        name="route_rank",
    )(idx_t, tri)
    counts = cnt[:, 0].astype(I32)
    padded = ((counts + MOE_TILE - 1) // MOE_TILE) * MOE_TILE
    ends = jnp.cumsum(padded)
    starts = ends - padded
    start_b = jnp.broadcast_to(starts.astype(F32)[:, None], (N_EXPERTS, LANES))
    pos = pl.pallas_call(
        _dest_kernel,
        grid=(n_tok // tm,),
        in_specs=[col, col, pl.BlockSpec((N_EXPERTS, LANES), lambda i: (0, 0))],
        out_specs=col,
        out_shape=jax.ShapeDtypeStruct((TOP_K, n_tok), I32),
        compiler_params=_cparams(("arbitrary",)),
        name="route_dest",
    )(idx_t, rank, start_b)
    n_tiles = TOP_K * n_tok // MOE_TILE + N_EXPERTS
    n_used = (ends[-1] // MOE_TILE).astype(I32)
    tile_start = jnp.minimum(jnp.arange(n_tiles, dtype=I32), n_used - 1) * MOE_TILE
    tile_e = jnp.minimum(jnp.searchsorted(ends, tile_start, side='right'), N_EXPERTS - 1).astype(I32)
    first = jnp.concatenate([jnp.ones((1,), I32), (tile_e[1:] != tile_e[:-1]).astype(I32)])
    return pos, tile_e, first, n_used.reshape(1), (starts + counts).astype(I32), (padded - counts).astype(I32)


def _pad_chunks(n):
    out = []
    b = MOE_TILE // 2
    while b >= 1:
        out.append(((n // (2 * b)) * (2 * b), b))
        b //= 2
    return out


def _rows_to_tiles(ref, val):
    n = val.shape[0]
    for s in range(SUBLANES):
        ref[pl.ds(s, n, stride=SUBLANES), :] = val[:, s * LANES:(s + 1) * LANES]


def _tiles_to_rows(ref, n, *lead):
    return jnp.concatenate([ref[(*lead, pl.ds(s, n, stride=SUBLANES), slice(None))] for s in range(SUBLANES)],
                           axis=1)


def _tile_rows(ref, row, n=1):
    start = pl.multiple_of(row * SUBLANES, SUBLANES)
    return ref.at[pl.ds(start, n * SUBLANES), :]


def _dispatch_kernel(pad_start, pad_n, n_used, pos_ref, h2_ref, xs_hbm, zbuf, sem, zsem, *,
                     experts_per_step, n_tiles):
    i = pl.program_id(0)
    tc = h2_ref.shape[0] // SUBLANES

    @pl.when(i == 0)
    def _():
        zbuf[...] = jnp.zeros_like(zbuf)

    for k in range(TOP_K):
        def body(jj, carry):
            for u in range(8):
                j = jj * 8 + u
                pltpu.make_async_copy(_tile_rows(h2_ref, j), _tile_rows(xs_hbm, pos_ref[0, k, j]), sem).start()
            return carry
        lax.fori_loop(0, tc // 8, body, 0)

    def pad_copies(e, fn):
        n = pad_n[e]
        for off, cnt in _pad_chunks(n):
            @pl.when((n & cnt) != 0)
            def _():
                fn(pltpu.make_async_copy(_tile_rows(zbuf, 0, cnt), _tile_rows(xs_hbm, pad_start[e] + off, cnt),
                                         zsem))

    def tail_copies(tile, fn):
        for half in range(2):
            fn(pltpu.make_async_copy(zbuf, _tile_rows(xs_hbm, tile * MOE_TILE + half * (MOE_TILE // 2),
                                                      MOE_TILE // 2), zsem))

    for s in range(experts_per_step):
        e = i * experts_per_step + s
        tile = n_used[0] + e

        @pl.when(e < N_EXPERTS)
        def _():
            pad_copies(e, lambda cp: cp.start())

        @pl.when(tile < n_tiles)
        def _():
            tail_copies(tile, lambda cp: cp.start())

        @pl.when(e < N_EXPERTS)
        def _():
            pad_copies(e, lambda cp: cp.wait())

        @pl.when(tile < n_tiles)
        def _():
            tail_copies(tile, lambda cp: cp.wait())

    for k in range(TOP_K):
        pltpu.make_async_copy(h2_ref, _tile_rows(xs_hbm, 0, tc), sem).wait()


def _dispatch(h2, pos3, pad_start, pad_n, n_used, n_rows):
    t = h2.shape[0] // SUBLANES
    tc = pos3.shape[2]
    nt = t // tc
    eps = -(-N_EXPERTS // nt)
    grid_spec = pltpu.PrefetchScalarGridSpec(
        num_scalar_prefetch=3,
        grid=(nt,),
        in_specs=[pl.BlockSpec((1, TOP_K, tc), lambda i, ps, pn, nu: (i, 0, 0), memory_space=pltpu.SMEM),
                  pl.BlockSpec((tc * SUBLANES, LANES), lambda i, ps, pn, nu: (i, 0))],
        out_specs=pl.BlockSpec(memory_space=pl.ANY),
        scratch_shapes=[pltpu.VMEM((MOE_TILE // 2 * SUBLANES, LANES), F32), pltpu.SemaphoreType.DMA(()),
                        pltpu.SemaphoreType.DMA(())])
    return pl.pallas_call(
        functools.partial(_dispatch_kernel, experts_per_step=eps, n_tiles=n_rows // MOE_TILE),
        grid_spec=grid_spec,
        out_shape=jax.ShapeDtypeStruct((n_rows * SUBLANES, LANES), F32),
        compiler_params=_cparams(("arbitrary",)),
        name="dispatch",
    )(pad_start, pad_n, n_used, pos3, h2)


def _expert_kernel(tile_e, first, n_used, x_ref, wg_ref, wu_ref, wd_ref, y_ref, wgb, wub, wdb):
    i = pl.program_id(0)
    n = n_used[0]

    @pl.when(i < n)
    def _():
        @pl.when(first[i] == 1)
        def _():
            wgb[...] = wg_ref[0].astype(BF16)
            wub[...] = wu_ref[0].astype(BF16)
            wdb[...] = wd_ref[0].astype(BF16)

        xb = _tiles_to_rows(x_ref, MOE_TILE).astype(BF16)
        hid = (_silu(_dot(xb, wgb[...])) * _dot(xb, wub[...])).astype(BF16)
        _rows_to_tiles(y_ref, _dot(hid, wdb[...]))

    @pl.when(i >= n)
    def _():
        y_ref[...] = jnp.zeros_like(y_ref)


def _experts(xs, tile_e, first, n_used, wg, wu, wd):
    n_rows = xs.shape[0] // SUBLANES
    n_tiles = n_rows // MOE_TILE
    d, de = wg.shape[1], wg.shape[2]
    blk = (MOE_TILE * SUBLANES, LANES)
    grid_spec = pltpu.PrefetchScalarGridSpec(
        num_scalar_prefetch=3,
        grid=(n_tiles,),
        in_specs=[pl.BlockSpec(blk, lambda i, te, fi, nu: (jnp.minimum(i, nu[0] - 1), 0)),
                  pl.BlockSpec((1, d, de), lambda i, te, fi, nu: (te[i], 0, 0)),
                  pl.BlockSpec((1, d, de), lambda i, te, fi, nu: (te[i], 0, 0)),
                  pl.BlockSpec((1, de, d), lambda i, te, fi, nu: (te[i], 0, 0))],
        out_specs=pl.BlockSpec(blk, lambda i, te, fi, nu: (i, 0)),
        scratch_shapes=[pltpu.VMEM((d, de), BF16), pltpu.VMEM((d, de), BF16), pltpu.VMEM((de, d), BF16)])
    return pl.pallas_call(
        _expert_kernel,
        grid_spec=grid_spec,
        out_shape=jax.ShapeDtypeStruct(xs.shape, F32),
        compiler_params=_cparams(("arbitrary",)),
        name="experts",
    )(tile_e, first, n_used, xs, wg, wu, wd)


def _combine_kernel(posc_ref, posn_ref, ys_hbm, acc_ref, w_ref, mod_ref, g_ref, o_ref, buf, sem):
    i = pl.program_id(0)
    n = pl.num_programs(0)
    tc = acc_ref.shape[0]

    def issue(pos_ref, slot):
        for k in range(TOP_K):
            def body(jj, carry):
                for u in range(8):
                    j = jj * 8 + u
                    pltpu.make_async_copy(_tile_rows(ys_hbm, pos_ref[0, k, j]), _tile_rows(buf.at[slot, k], j),
                                          sem.at[slot]).start()
                return carry
            lax.fori_loop(0, tc // 8, body, 0)

    @pl.when(i == 0)
    def _():
        issue(posc_ref, 0)

    @pl.when(i + 1 < n)
    def _():
        issue(posn_ref, (i + 1) % 2)

    slot = i % 2
    for k in range(TOP_K):
        pltpu.make_async_copy(_tile_rows(ys_hbm, 0, tc), buf.at[slot, k], sem.at[slot]).wait()
    w = w_ref[...]
    routed = w[:, 0:1] * _tiles_to_rows(buf, tc, slot, 0)
    for k in range(1, TOP_K):
        routed = routed + w[:, k:k + 1] * _tiles_to_rows(buf, tc, slot, k)
    x2 = acc_ref[...] + mod_ref[0, 5:6, :] * routed
    o_ref[...] = x2 * lax.rsqrt(jnp.mean(x2 * x2, axis=-1, keepdims=True) + RMS_EPS) * g_ref[...]


def _tile_pos(pos_t, tc):
    return pos_t.reshape(TOP_K, pos_t.shape[1] // tc, tc).transpose(1, 0, 2)


def _combine(ys, pos3, acc0, w_tok, mod3, g_final, seq):
    t, d = acc0.shape
    tc = pos3.shape[2]
    nt = t // tc
    return pl.pallas_call(
        _combine_kernel,
        grid=(nt,),
        in_specs=[pl.BlockSpec((1, TOP_K, tc), lambda i: (i, 0, 0), memory_space=pltpu.SMEM),
                  pl.BlockSpec((1, TOP_K, tc), lambda i: (jnp.minimum(i + 1, nt - 1), 0, 0),
                               memory_space=pltpu.SMEM),
                  pl.BlockSpec(memory_space=pl.ANY),
                  pl.BlockSpec((tc, d), lambda i: (i, 0)),
                  pl.BlockSpec((tc, TOP_K), lambda i: (i, 0)),
                  pl.BlockSpec((1, 6, d), lambda i: ((i * tc) // seq, 0, 0)),
                  pl.BlockSpec((1, d), lambda i: (0, 0))],
        out_specs=pl.BlockSpec((tc, d), lambda i: (i, 0)),
        out_shape=jax.ShapeDtypeStruct((t, d), F32),
        scratch_shapes=[pltpu.VMEM((2, TOP_K, tc * SUBLANES, LANES), F32), pltpu.SemaphoreType.DMA((2,))],
        compiler_params=_cparams(("arbitrary",)),
        name="combine",
    )(pos3, pos3, ys, acc0, w_tok, mod3, g_final)


def kernel(x, c, positions, w_ada, b_ada, g_norm_mix, w_in, cmp_pos_k, cmp_w1_k, cmp_w2_k, cmp_pos_v,
           cmp_w1_v, cmp_w2_v, g_nsa_out, g_ret_out, w_out, g_norm_ffn, w_router, router_bias,
           w_gate_e, w_up_e, w_down_e, w_gate_s, w_up_s, w_down_s, g_norm_final):
    bsz, seq, d = x.shape
    assert d == SUBLANES * LANES, "MoE rows are moved as one (8, 128) tile each"
    t = bsz * seq
    x2 = x.reshape(t, d)
    cos_t, sin_t = _rope_tables(positions.reshape(t, 1).astype(F32))
    n_cmp_pad = seq // CMP_STRIDE

    def cmp_rows(tab):
        rows = tab[:, :LANES].reshape(bsz, seq, LANES)[:, CMP_BLOCK - 1::CMP_STRIDE]
        return jnp.pad(rows, ((0, 0), (0, n_cmp_pad - rows.shape[1]), (0, 0)))

    cos_c, sin_c = cmp_rows(cos_t), cmp_rows(sin_t)
    for l in range(w_in.shape[0]):
        mod3 = _ada(c, w_ada[l], b_ada[l]).reshape(bsz, 6, d)
        q, kc, vc, kk, gates, qr, kr, vr, gr, vsel_t, vwin_t = _inproj(
            x2, mod3, g_norm_mix[l].reshape(1, d), _arrange_w_in(w_in[l]), cos_t, sin_t, seq)
        kcmp, vcmp = _compress(kc, vc, cos_c, sin_c,
                               _arrange_cmp_weights(cmp_pos_k[l], cmp_w1_k[l], cmp_w2_k[l]),
                               _arrange_cmp_weights(cmp_pos_v[l], cmp_w1_v[l], cmp_w2_v[l]), bsz, seq)
        onsa = _nsa(q, kcmp, vcmp, kk, vsel_t, vwin_t, gates, g_nsa_out[l].reshape(1, -1), bsz, seq)
        oret = _retention(qr, kr, vr, gr, g_ret_out[l], bsz, seq)
        acc0, h2, idx_t, w_t = _post(x2, onsa, oret, mod3, w_out[l], g_norm_ffn[l].reshape(1, d),
                                     w_router[l], router_bias[l], w_gate_s[l], w_up_s[l], w_down_s[l], seq)
        pos_t, tile_e, first, n_used, pad_start, pad_n = _route_plan(idx_t, t)
        n_rows = TOP_K * t + N_EXPERTS * MOE_TILE
        xs = _dispatch(h2, _tile_pos(pos_t, min(256, seq)), pad_start, pad_n, n_used, n_rows)
        ys = _experts(xs, tile_e, first, n_used, w_gate_e[l], w_up_e[l], w_down_e[l])
        last = l == w_in.shape[0] - 1
        gfin = g_norm_final.reshape(1, d)
        x2 = _combine(ys, _tile_pos(pos_t, 128), acc0, w_t.T, mod3, gfin, seq)
        assert last, "final norm is fused into the combine stage; depth 1 only"
    return x2.reshape(bsz, seq, d)
```

```python
import functools
import math

import numpy as np
import jax
import jax.numpy as jnp
from jax import lax
from jax.experimental import pallas as pl
from jax.experimental.pallas import tpu as pltpu

F32 = jnp.float32
BF16 = jnp.bfloat16
I32 = jnp.int32
HIGHEST = lax.Precision.HIGHEST

LANES = 128
SUBLANES = 8
NSA_HEAD_DIM = 64
NSA_HEADS = 8
NSA_GROUPS = 2
NSA_Q_PER_KV = NSA_HEADS // NSA_GROUPS
CMP_BLOCK = 32
CMP_STRIDE = 16
CMP_HIDDEN = 128
SEL_BLOCK = 64
SEL_TOP = 8
WINDOW = 512
Q_BLOCK = 128
FORCE_BONUS = 1.0e4
RET_HEADS = 4
RET_HEAD_DIM = 128
RET_CHUNK = 128
ROPE_THETA = 10000.0
RMS_EPS = 1e-6
N_EXPERTS = 256
N_EXPERT_GROUPS = 8
TOPK_GROUPS = 4
TOP_K = 8
ROUTED_SCALE = 2.5
MOE_TILE = 256
SEL_CHUNK = 256
MASK_BIG = 2.0 ** 100
MASK_FLOOR = 2.0 ** 99
NEG = -1e30
VMEM_LIMIT = 48 * 1024 * 1024


def _cparams(sem):
    return pltpu.CompilerParams(dimension_semantics=sem, vmem_limit_bytes=VMEM_LIMIT)


def _dot(a, b, **kw):
    return jnp.dot(a, b, preferred_element_type=F32, **kw)


def _dot_nt(a, b, **kw):
    return lax.dot_general(a, b, (((1,), (1,)), ((), ())), preferred_element_type=F32, **kw)


def _sigmoid(v):
    return 1.0 / (1.0 + jnp.exp(-v))


def _silu(v):
    return v * _sigmoid(v)


def _ada_kernel(c_ref, w_ref, b_ref, o_ref):
    o_ref[...] = _dot(_silu(c_ref[...]), w_ref[...], precision=HIGHEST) + b_ref[...]


def _ada(c, w, b):
    bsz, d = c.shape
    n = w.shape[1]
    tn = 1536
    cp = jnp.zeros((8, d), F32).at[:bsz].set(c)
    out = pl.pallas_call(
        _ada_kernel,
        grid=(n // tn,),
        in_specs=[pl.BlockSpec((8, d), lambda j: (0, 0)),
                  pl.BlockSpec((d, tn), lambda j: (0, j)),
                  pl.BlockSpec((1, tn), lambda j: (0, j))],
        out_specs=pl.BlockSpec((8, tn), lambda j: (0, j)),
        out_shape=jax.ShapeDtypeStruct((8, n), F32),
        compiler_params=_cparams(("arbitrary",)),
        name="ada",
    )(cp, w, b.reshape(1, n))
    return out[:bsz]


def _rope_table_kernel(pos_ref, inv_ref, sgn_ref, cos_ref, sin_ref):
    ang = pos_ref[...] * inv_ref[...]
    cos_ref[...] = jnp.cos(ang)
    sin_ref[...] = jnp.sin(ang) * sgn_ref[...]


def _rope_tables(posf):
    t = posf.shape[0]
    inv_n = ROPE_THETA ** (-jnp.arange(0, NSA_HEAD_DIM, 2, dtype=F32) / NSA_HEAD_DIM)
    inv_r = ROPE_THETA ** (-jnp.arange(0, RET_HEAD_DIM, 2, dtype=F32) / RET_HEAD_DIM)
    inv = jnp.concatenate([jnp.tile(inv_n, 4), jnp.tile(inv_r, 2)]).reshape(1, 2 * LANES)
    sgn_n = np.where((np.arange(LANES) % NSA_HEAD_DIM) < NSA_HEAD_DIM // 2, -1.0, 1.0)
    sgn_r = np.where(np.arange(LANES) < RET_HEAD_DIM // 2, -1.0, 1.0)
    sgn = jnp.asarray(np.concatenate([sgn_n, sgn_r]).reshape(1, 2 * LANES), F32)
    tm = min(t, 1024)
    return pl.pallas_call(
        _rope_table_kernel,
        grid=(t // tm,),
        in_specs=[pl.BlockSpec((tm, 1), lambda i: (i, 0)),
                  pl.BlockSpec((1, 2 * LANES), lambda i: (0, 0)),
                  pl.BlockSpec((1, 2 * LANES), lambda i: (0, 0))],
        out_specs=[pl.BlockSpec((tm, 2 * LANES), lambda i: (i, 0))] * 2,
        out_shape=[jax.ShapeDtypeStruct((t, 2 * LANES), F32)] * 2,
        compiler_params=_cparams(("arbitrary",)),
        name="rope_tables",
    )(posf, inv, sgn)


def _rope64(p, cos, sin, first_half):
    rot = jnp.where(first_half, pltpu.roll(p, 96, 1), pltpu.roll(p, 32, 1))
    return p * cos + rot * sin


def _rope128(p, cos, sin):
    return p * cos + pltpu.roll(p, 64, 1) * sin


_C_Q = 0
_C_KV = _C_Q + NSA_HEADS * LANES
_C_GATE = _C_KV + 6 * LANES
_C_RET = _C_GATE + LANES
_C_END = _C_RET + 4 * RET_HEADS * RET_HEAD_DIM


def _arrange_w_in(w_in):
    d = w_in.shape[0]
    nw = NSA_HEADS * NSA_HEAD_DIM
    q = w_in[:, :nw].reshape(d, NSA_HEADS, NSA_HEAD_DIM)
    z = jnp.zeros_like(q)
    grp = (jnp.arange(NSA_HEADS) // NSA_Q_PER_KV)[None, :, None]
    qpad = jnp.where(grp == 0, jnp.concatenate([q, z], -1), jnp.concatenate([z, q], -1))
    qpad = qpad.reshape(d, NSA_HEADS * LANES)
    kv = w_in[:, nw:nw + 6 * LANES]
    g0 = nw + 6 * LANES
    gate = jnp.pad(w_in[:, g0:g0 + 3 * NSA_HEADS], ((0, 0), (0, LANES - 3 * NSA_HEADS)))
    ret = w_in[:, g0 + 3 * NSA_HEADS:]
    return jnp.concatenate([qpad, kv, gate, ret], axis=1).astype(BF16)


def _inproj_kernel(x_ref, mod_ref, g_ref, w_ref, cos_ref, sin_ref,
                   q_ref, kc_ref, vc_ref, kk_ref, gate_ref, qr_ref, kr_ref, vr_ref, gr_ref, vst_ref, vwt_ref):
    x = x_ref[...]
    tm = x.shape[0]
    ms = jnp.mean(x * x, axis=-1, keepdims=True)
    y = x * lax.rsqrt(ms + RMS_EPS) * g_ref[...]
    h = y * (1.0 + mod_ref[0, 1:2, :]) + mod_ref[0, 0:1, :]
    hb = h.astype(BF16)
    cos_n, sin_n = cos_ref[:, 0:LANES], sin_ref[:, 0:LANES]
    cos_r, sin_r = cos_ref[:, LANES:], sin_ref[:, LANES:]
    lane = lax.broadcasted_iota(I32, (tm, LANES), 1)
    first_half = (lane % NSA_HEAD_DIM) < (NSA_HEAD_DIM // 2)
    scale_n = NSA_HEAD_DIM ** -0.5
    scale_r = RET_HEAD_DIM ** -0.5

    def proj(c0, n):
        return _dot(hb, w_ref[:, c0:c0 + n])

    for hh in range(NSA_HEADS):
        p = proj(_C_Q + hh * LANES, LANES)
        q_ref[:, hh * LANES:(hh + 1) * LANES] = (_rope64(p, cos_n, sin_n, first_half) * scale_n).astype(BF16)
    kv = proj(_C_KV, 6 * LANES)
    kc_ref[...] = kv[:, 0:LANES].astype(BF16)
    vc_ref[...] = kv[:, LANES:2 * LANES].astype(BF16)
    kk_ref[:, 0:LANES] = _rope64(kv[:, 2 * LANES:3 * LANES], cos_n, sin_n, first_half).astype(BF16)
    kk_ref[:, LANES:2 * LANES] = _rope64(kv[:, 4 * LANES:5 * LANES], cos_n, sin_n, first_half).astype(BF16)
    group0 = lane < NSA_HEAD_DIM
    for c0, vt_ref, chunk in ((3 * LANES, vst_ref, SEL_CHUNK), (5 * LANES, vwt_ref, Q_BLOCK)):
        v = kv[:, c0:c0 + LANES]
        for g, vg in enumerate((jnp.where(group0, v, 1.0), jnp.where(group0, 1.0, v))):
            for cc in range(tm // chunk):
                vt_ref[cc, g] = vg[cc * chunk:(cc + 1) * chunk].T.astype(BF16)
    gate_ref[...] = _sigmoid(proj(_C_GATE, LANES))
    rw = RET_HEADS * RET_HEAD_DIM
    for hh in range(RET_HEADS):
        sl = slice(hh * LANES, (hh + 1) * LANES)
        pq = proj(_C_RET + hh * LANES, LANES)
        qr_ref[:, sl] = _rope128(pq, cos_r, sin_r).astype(BF16)
        pk = proj(_C_RET + rw + hh * LANES, LANES)
        kr_ref[:, sl] = (_rope128(pk, cos_r, sin_r) * scale_r).astype(BF16)
    vr_ref[...] = proj(_C_RET + 2 * rw, rw).astype(BF16)
    gr_ref[...] = proj(_C_RET + 3 * rw, rw)


def _inproj(x2, mod3, g_mix, w_in_p, cos_t, sin_t, seq):
    t, d = x2.shape
    tm = min(512, seq)
    rw = RET_HEADS * RET_HEAD_DIM
    row = lambda n: pl.BlockSpec((tm, n), lambda i: (i, 0))
    outs = [(NSA_HEADS * LANES, BF16), (LANES, BF16), (LANES, BF16), (2 * LANES, BF16), (LANES, F32),
            (rw, BF16), (rw, BF16), (rw, BF16), (rw, F32)]
    vt_specs, vt_shapes = [], []
    for chunk in (SEL_CHUNK, Q_BLOCK):
        vt_specs.append(pl.BlockSpec((tm // chunk, NSA_GROUPS, LANES, chunk), lambda i: (i, 0, 0, 0)))
        vt_shapes.append(jax.ShapeDtypeStruct((t // chunk, NSA_GROUPS, LANES, chunk), BF16))
    return pl.pallas_call(
        _inproj_kernel,
        grid=(t // tm,),
        in_specs=[row(d),
                  pl.BlockSpec((1, 6, d), lambda i: ((i * tm) // seq, 0, 0)),
                  pl.BlockSpec((1, d), lambda i: (0, 0)),
                  pl.BlockSpec((d, _C_END), lambda i: (0, 0)),
                  row(2 * LANES), row(2 * LANES)],
        out_specs=[row(n) for n, _ in outs] + vt_specs,
        out_shape=[jax.ShapeDtypeStruct((t, n), dt) for n, dt in outs] + vt_shapes,
        compiler_params=_cparams(("arbitrary",)),
        name="inproj",
    )(x2, mod3, g_mix, w_in_p, cos_t, sin_t)


def _gelu_tanh(v):
    return v * (0.5 * (1.0 + jnp.tanh(math.sqrt(2.0 / math.pi) * (v + 0.044715 * (v * v * v)))))


def _compress_kernel(kc_ref, vc_ref, wkt_ref, wkb_ref, wk2_ref, pk_ref, wvt_ref, wvb_ref, wv2_ref, pv_ref,
                     cos_ref, sin_ref, ko_ref, vo_ref):
    def one(x_ref, wt_ref, wb_ref, w2_ref, p_ref):
        xx = x_ref[0]
        a = _dot(xx, wt_ref[...])
        b = _dot(xx, wb_ref[...])
        pb = p_ref[...].astype(BF16)
        bias = _dot(pb, wt_ref[...])[0:1] + _dot(pb, wb_ref[...])[1:2]
        hid = a + pltpu.roll(b, b.shape[0] - 1, 0) + bias
        return _dot(_gelu_tanh(hid).astype(BF16), w2_ref[...])

    k = one(kc_ref, wkt_ref, wkb_ref, wk2_ref, pk_ref)
    lane = lax.broadcasted_iota(I32, k.shape, 1)
    first_half = (lane % NSA_HEAD_DIM) < (NSA_HEAD_DIM // 2)
    ko_ref[0] = _rope64(k, cos_ref[0], sin_ref[0], first_half).astype(BF16)
    v = one(vc_ref, wvt_ref, wvb_ref, wv2_ref, pv_ref)
    group0 = lane < NSA_HEAD_DIM
    vo_ref[0, 0] = jnp.where(group0, v, 1.0).T.astype(BF16)
    vo_ref[0, 1] = jnp.where(group0, 1.0, v).T.astype(BF16)


def _arrange_cmp_weights(pos, w1, w2):
    half = CMP_BLOCK // 2
    dh = NSA_HEAD_DIM
    w1r = w1.reshape(CMP_BLOCK, dh, CMP_HIDDEN)

    def block(wpart):
        z = jnp.zeros_like(wpart)
        g0 = jnp.concatenate([wpart, z], axis=-1)
        g1 = jnp.concatenate([z, wpart], axis=-1)
        return jnp.stack([g0, g1], axis=1).reshape(half * 2 * dh, 2 * CMP_HIDDEN)

    wt, wb = block(w1r[:half]), block(w1r[half:])
    z2 = jnp.zeros_like(w2)
    w2b = jnp.concatenate([jnp.concatenate([w2, z2], 1), jnp.concatenate([z2, w2], 1)], 0)
    ptop = jnp.tile(pos[:half], (1, 2)).reshape(1, -1)
    pbot = jnp.tile(pos[half:], (1, 2)).reshape(1, -1)
    prow = jnp.concatenate([ptop, pbot, jnp.zeros((6, ptop.shape[1]), F32)], 0)
    return wt.astype(BF16), wb.astype(BF16), w2b.astype(BF16), prow


def _compress(kc, vc, cos_c, sin_c, kparams, vparams, bsz, seq):
    nchunk = seq // CMP_STRIDE
    width = CMP_STRIDE * LANES
    kc16 = kc.reshape(bsz, nchunk, width)
    vc16 = vc.reshape(bsz, nchunk, width)
    full = lambda a: pl.BlockSpec(a.shape, lambda b: (0,) * a.ndim)
    per_b = lambda n: pl.BlockSpec((1, nchunk, n), lambda b: (b, 0, 0))
    return pl.pallas_call(
        _compress_kernel,
        grid=(bsz,),
        in_specs=[per_b(width), per_b(width)] + [full(a) for a in kparams] + [full(a) for a in vparams]
                 + [per_b(LANES), per_b(LANES)],
        out_specs=[per_b(LANES), pl.BlockSpec((1, NSA_GROUPS, LANES, nchunk), lambda b: (b, 0, 0, 0))],
        out_shape=[jax.ShapeDtypeStruct((bsz, nchunk, LANES), BF16),
                   jax.ShapeDtypeStruct((bsz, NSA_GROUPS, LANES, nchunk), BF16)],
        compiler_params=_cparams(("arbitrary",)),
        name="compress",
    )(kc16, vc16, *kparams, *vparams, cos_c, sin_c)


def _softmax_chunk(m, acc, s, vt):
    m_new = jnp.maximum(m, jnp.max(s, axis=0, keepdims=True))
    e = jnp.exp(s - m_new).astype(BF16)
    acc = jnp.exp(m - m_new) * acc + _dot(vt, e)
    return m_new, acc


def _nsa_kernel(q_ref, kc_ref, vct_ref, ks_ref, kw_ref, vst_ref, vwt_ref, gate_ref,
                ovt_ref, exp_ref, g_ref, o_ref, *, seq):
    qb = pl.program_id(1)
    t0 = qb * Q_BLOCK
    cols = NSA_Q_PER_KV * Q_BLOCK
    n_cmp_pad = kc_ref.shape[1]
    n_sel = seq // SEL_BLOCK
    t_row = t0 + lax.broadcasted_iota(I32, (1, Q_BLOCK), 1)
    gates_t = gate_ref[...].T
    lane = lax.broadcasted_iota(I32, (Q_BLOCK, LANES), 1)
    key = lax.broadcasted_iota(I32, (Q_BLOCK, Q_BLOCK), 0)
    tok = lax.broadcasted_iota(I32, (Q_BLOCK, Q_BLOCK), 1)
    cend = lax.broadcasted_iota(I32, (n_cmp_pad, 1), 0) * CMP_STRIDE + (CMP_BLOCK - 1)
    bias_cmp = jnp.where(cend <= t_row, 0.0, -MASK_BIG)
    last_c = qb // (SEL_CHUNK // Q_BLOCK)
    kpos_last = last_c * SEL_CHUNK + lax.broadcasted_iota(I32, (SEL_CHUNK, 1), 0)
    bias_diag = jnp.where(kpos_last <= t_row, 0.0, -MASK_BIG)
    n_win = WINDOW // Q_BLOCK
    bias_win_first = jnp.where(key > tok, 0.0, -MASK_BIG)
    bias_win_last = jnp.where(key <= tok, 0.0, -MASK_BIG)
    init = (jnp.full((1, cols), -MASK_FLOOR, F32), jnp.zeros((LANES, cols), F32))
    tile_heads = lambda b: jnp.concatenate([b] * NSA_Q_PER_KV, axis=1)

    def finish(acc, g):
        ones_row = NSA_HEAD_DIM * (1 - g)
        return acc * (1.0 / jnp.maximum(acc[ones_row:ones_row + 1, :], 1e-20))

    q4s, o_cs, sel_bs = [], [], []
    for g in range(NSA_GROUPS):
        q4 = jnp.concatenate([q_ref[:, (NSA_Q_PER_KV * g + r) * LANES:(NSA_Q_PER_KV * g + r + 1) * LANES]
                              for r in range(NSA_Q_PER_KV)], axis=0)
        q4s.append(q4)
        s_c = _dot_nt(kc_ref[0], q4) + tile_heads(bias_cmp)
        e_c = jnp.exp(s_c - jnp.maximum(jnp.max(s_c, axis=0, keepdims=True), -MASK_FLOOR))
        p_c = e_c * (1.0 / jnp.maximum(jnp.sum(e_c, axis=0, keepdims=True), 1e-20))
        psum = p_c[:, 0:Q_BLOCK]
        for r in range(1, NSA_Q_PER_KV):
            psum = psum + p_c[:, r * Q_BLOCK:(r + 1) * Q_BLOCK]
        o_cs.append(finish(_dot(vct_ref[0, g], e_c.astype(BF16)), g))
        imp_t = _dot(ovt_ref[...], psum, precision=HIGHEST)
        nsp = imp_t.shape[0]
        jrow = lax.broadcasted_iota(I32, (nsp, 1), 0)
        cur = t_row // SEL_BLOCK
        valid = (jrow * SEL_BLOCK <= t_row) & (jrow < n_sel)
        forced = (jrow == 0) | (jrow == cur) | (jrow == cur - 1)
        val = jnp.where(valid, imp_t + jnp.where(forced, FORCE_BONUS, 0.0), -1.0)
        val = jnp.where(jrow < n_sel, val, -jnp.inf)
        sel_t = jnp.zeros((nsp, Q_BLOCK), F32)
        for _ in range(min(SEL_TOP, n_sel)):
            mx = jnp.max(val, axis=0, keepdims=True)
            jmin = jnp.min(jnp.where(val == mx, jrow, nsp), axis=0, keepdims=True)
            hit = jrow == jmin
            sel_t = jnp.where(hit, 1.0, sel_t)
            val = jnp.where(hit, -jnp.inf, val)
        sel_bs.append(jnp.concatenate([sel_t[:n_sel], jnp.ones((SUBLANES, Q_BLOCK), F32),
                                       jnp.zeros((LANES - n_sel - SUBLANES, Q_BLOCK), F32)], axis=0).astype(BF16))

    def sel_scores(c):
        keys = ks_ref[pl.ds(pl.multiple_of(c * SEL_CHUNK, SEL_CHUNK), SEL_CHUNK), :]
        expand = exp_ref[c]
        return tuple(_dot_nt(keys, q4s[g]) + tile_heads(_dot(expand, sel_bs[g])) for g in range(NSA_GROUPS))

    def sel_reduce(c, state, scores):
        out = []
        for g in range(NSA_GROUPS):
            out.extend(_softmax_chunk(state[2 * g], state[2 * g + 1], scores[g], vst_ref[c, g]))
        return tuple(out)

    def sel_step(c, carry):
        state, scores = carry
        return sel_reduce(c, state, scores), sel_scores(c + 1)

    state, scores = lax.fori_loop(0, last_c, sel_step, (init * NSA_GROUPS, sel_scores(0)))
    diag = tile_heads(bias_diag)
    state = sel_reduce(last_c, state, tuple(s + diag for s in scores))
    o_ss = [finish(state[2 * g + 1], g) for g in range(NSA_GROUPS)]
    s_ws = [[] for _ in range(NSA_GROUPS)]
    kbs = []
    for w in range(n_win + 1):
        wb = qb - n_win + w
        kb = jnp.maximum(wb, 0)
        kbs.append(kb)
        keys = kw_ref[pl.ds(pl.multiple_of(kb * Q_BLOCK, Q_BLOCK), Q_BLOCK), :]
        if w == n_win:
            bias = bias_win_last
        else:
            before_start = jnp.where(wb >= 0, 0.0, -MASK_BIG)
            bias = (bias_win_first + before_start) if w == 0 else jnp.full((Q_BLOCK, Q_BLOCK), before_start)
        for g in range(NSA_GROUPS):
            s_ws[g].append(_dot_nt(keys, q4s[g]) + tile_heads(bias))
    o_ws = []
    for g in range(NSA_GROUPS):
        m_w = jnp.full((1, cols), -MASK_FLOOR, F32)
        for s in s_ws[g]:
            m_w = jnp.maximum(m_w, jnp.max(s, axis=0, keepdims=True))
        acc = None
        for kb, s in zip(kbs, s_ws[g]):
            pv = _dot(vwt_ref[kb, g], jnp.exp(s - m_w).astype(BF16))
            acc = pv if acc is None else acc + pv
        o_ws.append(finish(acc, g))
    heads_out = []
    for g in range(NSA_GROUPS):
        o_c, o_s, o_w = o_cs[g], o_ss[g], o_ws[g]
        for r in range(NSA_Q_PER_KV):
            hh = NSA_Q_PER_KV * g + r
            cs = slice(r * Q_BLOCK, (r + 1) * Q_BLOCK)
            o = (gates_t[3 * hh:3 * hh + 1, :] * o_c[:, cs] + gates_t[3 * hh + 1:3 * hh + 2, :] * o_s[:, cs]
                 + gates_t[3 * hh + 2:3 * hh + 3, :] * o_w[:, cs]).T
            if (hh % 2) != g:
                o = pltpu.roll(o, NSA_HEAD_DIM, 1)
            heads_out.append(o)
    blocks = [jnp.where(lane < NSA_HEAD_DIM, heads_out[2 * i], heads_out[2 * i + 1])
              for i in range(NSA_HEADS // 2)]
    ss = sum(jnp.sum(b * b, axis=-1, keepdims=True) for b in blocks)
    inv = lax.rsqrt(ss / (NSA_HEADS * NSA_HEAD_DIM) + RMS_EPS)
    for i, b in enumerate(blocks):
        sl = slice(i * LANES, (i + 1) * LANES)
        o_ref[:, sl] = (b * inv * g_ref[:, sl]).astype(BF16)


def _nsa(q, kcmp, vcmp_t, kk, vsel_t, vwin_t, gates, g_nsa, bsz, seq):
    t = bsz * seq
    nq = seq // Q_BLOCK
    n_cmp_pad = seq // CMP_STRIDE
    n_cmp = (seq - CMP_BLOCK) // CMP_STRIDE + 1
    n_sel = seq // SEL_BLOCK
    assert n_sel % SUBLANES == 0 and n_sel + SUBLANES <= LANES, "selection mask needs a spare expansion row"
    nsp = n_sel
    cs = np.arange(n_cmp_pad) * CMP_STRIDE
    ss = np.arange(nsp) * SEL_BLOCK
    ov = ((cs[None, :] < ss[:, None] + SEL_BLOCK) & (cs[None, :] + CMP_BLOCK > ss[:, None])
          & (np.arange(n_cmp_pad)[None, :] < n_cmp) & (np.arange(nsp)[:, None] < n_sel))
    ovt = jnp.asarray(ov, F32)
    nch = seq // SEL_CHUNK
    kp = np.arange(seq).reshape(nch, SEL_CHUNK, 1)
    col = np.arange(LANES).reshape(1, 1, LANES)
    expand = jnp.asarray(np.where(kp // SEL_BLOCK == col, MASK_BIG, 0.0) + np.where(col == n_sel, -MASK_BIG, 0.0),
                         BF16)
    seqcol = lambda c: pl.BlockSpec((seq, LANES), lambda b, i: (b, c))
    per_b = lambda a: pl.BlockSpec((a.shape[0] // bsz,) + a.shape[1:], lambda b, i: (b,) + (0,) * (a.ndim - 1))
    return pl.pallas_call(
        functools.partial(_nsa_kernel, seq=seq),
        grid=(bsz, nq),
        in_specs=[pl.BlockSpec((Q_BLOCK, NSA_HEADS * LANES), lambda b, i: (b * nq + i, 0)),
                  per_b(kcmp), per_b(vcmp_t), seqcol(0), seqcol(1), per_b(vsel_t), per_b(vwin_t),
                  pl.BlockSpec((Q_BLOCK, LANES), lambda b, i: (b * nq + i, 0)),
                  pl.BlockSpec(ovt.shape, lambda b, i: (0, 0)),
                  pl.BlockSpec(expand.shape, lambda b, i: (0, 0, 0)),
                  pl.BlockSpec((1, NSA_HEADS * NSA_HEAD_DIM), lambda b, i: (0, 0))],
        out_specs=pl.BlockSpec((Q_BLOCK, NSA_HEADS * NSA_HEAD_DIM), lambda b, i: (b * nq + i, 0)),
        out_shape=jax.ShapeDtypeStruct((t, NSA_HEADS * NSA_HEAD_DIM), BF16),
        compiler_params=_cparams(("arbitrary", "arbitrary")),
        name="nsa",
    )(q, kcmp, vcmp_t, kk, kk, vsel_t, vwin_t, gates, ovt, expand, g_nsa)


def _ret_kernel(q_ref, k_ref, v_ref, gr_ref, dec_ref, xi_ref, zeta_ref, gch_ref, g_ref, o_ref, st_ref):
    @pl.when(pl.program_id(1) == 0)
    def _():
        st_ref[...] = jnp.zeros_like(st_ref)

    for hh in range(RET_HEADS):
        sl = slice(hh * LANES, (hh + 1) * LANES)
        q, k, v = q_ref[:, sl], k_ref[:, sl], v_ref[:, sl]
        sc = _dot_nt(q, k) * dec_ref[hh]
        inner = _dot(sc.astype(BF16), v)
        st = st_ref[hh]
        cross = _dot((q.astype(F32) * xi_ref[hh]).astype(BF16), st.astype(BF16))
        kz = (k.astype(F32) * zeta_ref[hh]).T.astype(BF16)
        st_ref[hh] = st * gch_ref[hh] + _dot(kz, v)
        o = inner + cross
        y = o * lax.rsqrt(jnp.mean(o * o, axis=-1, keepdims=True) + RMS_EPS) * g_ref[hh:hh + 1, :]
        o_ref[:, sl] = (_silu(gr_ref[:, sl]) * y).astype(BF16)


def _retention(qr, kr, vr, gr, g_ret, bsz, seq):
    t = bsz * seq
    c = RET_CHUNK
    n = seq // c
    log_g = jnp.log(1.0 - 2.0 ** (-5.0 - jnp.arange(RET_HEADS, dtype=F32)))
    i = jnp.arange(c, dtype=F32)
    diff = i[:, None] - i[None, :]
    causal = diff >= 0
    dec = jnp.where(causal, jnp.exp(log_g[:, None, None] * jnp.where(causal, diff, 0.0)), 0.0)
    xi = jnp.broadcast_to(jnp.exp(log_g[:, None] * (i + 1.0))[:, :, None], (RET_HEADS, c, LANES))
    zeta = jnp.broadcast_to(jnp.exp(log_g[:, None] * (c - 1.0 - i))[:, :, None], (RET_HEADS, c, LANES))
    gch = jnp.broadcast_to(jnp.exp(log_g * c)[:, None, None], (RET_HEADS, 1, LANES))
    w = RET_HEADS * RET_HEAD_DIM
    row = pl.BlockSpec((c, w), lambda b, j: (b * n + j, 0))
    full = lambda a: pl.BlockSpec(a.shape, lambda b, j: (0,) * a.ndim)
    return pl.pallas_call(
        _ret_kernel,
        grid=(bsz, n),
        in_specs=[row, row, row, row, full(dec), full(xi), full(zeta), full(gch), full(g_ret)],
        out_specs=row,
        out_shape=jax.ShapeDtypeStruct((t, w), BF16),
        scratch_shapes=[pltpu.VMEM((RET_HEADS, RET_HEAD_DIM, RET_HEAD_DIM), F32)],
        compiler_params=_cparams(("arbitrary", "arbitrary")),
        name="retention",
    )(qr, kr, vr, gr, dec, xi, zeta, gch, g_ret)


def _post_kernel(x_ref, onsa_ref, oret_ref, mod_ref, wo1_ref, wo2_ref, gffn_ref, wrt_ref, rb_ref,
                 wgs_ref, wus_ref, wds_ref, acc_ref, h2_ref, idx_ref, wt_ref):
    mix = _dot(onsa_ref[...], wo1_ref[...]) + _dot(oret_ref[...], wo2_ref[...])
    x1 = x_ref[...] + mod_ref[0, 2:3, :] * mix
    ms = jnp.mean(x1 * x1, axis=-1, keepdims=True)
    h2 = x1 * lax.rsqrt(ms + RMS_EPS) * gffn_ref[...] * (1.0 + mod_ref[0, 4:5, :]) + mod_ref[0, 3:4, :]
    _rows_to_tiles(h2_ref, h2)
    hb = h2.astype(BF16)
    hid = (_silu(_dot(hb, wgs_ref[...])) * _dot(hb, wus_ref[...])).astype(BF16)
    acc_ref[...] = x1 + mod_ref[0, 5:6, :] * _dot(hid, wds_ref[...])
    s = _sigmoid(_dot_nt(wrt_ref[...], h2, precision=HIGHEST))
    sb = s + rb_ref[...]
    per = N_EXPERTS // N_EXPERT_GROUPS
    ridx = lax.broadcasted_iota(I32, (per, 1), 0)
    blks, grp = [], []
    for gi in range(N_EXPERT_GROUPS):
        blk = sb[gi * per:(gi + 1) * per]
        m1 = jnp.max(blk, axis=0, keepdims=True)
        first = jnp.min(jnp.where(blk == m1, ridx, per), axis=0, keepdims=True)
        m2 = jnp.max(jnp.where(ridx == first, -jnp.inf, blk), axis=0, keepdims=True)
        blks.append(blk)
        grp.append(m1 + m2)
    masked = []
    for gi in range(N_EXPERT_GROUPS):
        rank = jnp.zeros_like(grp[gi])
        for gj in range(N_EXPERT_GROUPS):
            if gj < gi:
                rank = rank + (grp[gj] >= grp[gi]).astype(F32)
            elif gj > gi:
                rank = rank + (grp[gj] > grp[gi]).astype(F32)
        masked.append(jnp.where(rank < TOPK_GROUPS, blks[gi], NEG))
    val = jnp.concatenate(masked, axis=0)
    eidx = lax.broadcasted_iota(I32, (N_EXPERTS, 1), 0)
    ids, ws = [], []
    for _ in range(TOP_K):
        mx = jnp.max(val, axis=0, keepdims=True)
        emin = jnp.min(jnp.where(val == mx, eidx, N_EXPERTS), axis=0, keepdims=True)
        hit = eidx == emin
        ids.append(emin)
        ws.append(jnp.sum(jnp.where(hit, s, 0.0), axis=0, keepdims=True))
        val = jnp.where(hit, -jnp.inf, val)
    wsum = ws[0]
    for wk in ws[1:]:
        wsum = wsum + wk
    idx_ref[...] = jnp.concatenate(ids, axis=0)
    wt_ref[...] = jnp.concatenate(ws, axis=0) / wsum * ROUTED_SCALE


def _post(x2, onsa, oret, mod3, w_out, g_ffn, w_router, router_bias, wgs, wus, wds, seq):
    t, d = x2.shape
    tm = min(512, seq)
    hw = onsa.shape[1]
    wo1 = w_out[:hw].astype(BF16)
    wo2 = w_out[hw:].astype(BF16)
    wrt = w_router.T
    rb = jnp.broadcast_to(router_bias.reshape(N_EXPERTS, 1), (N_EXPERTS, tm))
    row = lambda n: pl.BlockSpec((tm, n), lambda i: (i, 0))
    full = lambda a: pl.BlockSpec(a.shape, lambda i: (0,) * a.ndim)
    col = pl.BlockSpec((TOP_K, tm), lambda i: (0, i))
    ops = (wo1, wo2, g_ffn, wrt, rb, wgs.astype(BF16), wus.astype(BF16), wds.astype(BF16))
    return pl.pallas_call(
        _post_kernel,
        grid=(t // tm,),
        in_specs=[row(d), row(hw), row(oret.shape[1]),
                  pl.BlockSpec((1, 6, d), lambda i: ((i * tm) // seq, 0, 0))] + [full(a) for a in ops],
        out_specs=[row(d), pl.BlockSpec((tm * SUBLANES, LANES), lambda i: (i, 0)), col, col],
        out_shape=[jax.ShapeDtypeStruct((t, d), F32), jax.ShapeDtypeStruct((t * SUBLANES, LANES), F32),
                   jax.ShapeDtypeStruct((TOP_K, t), I32), jax.ShapeDtypeStruct((TOP_K, t), F32)],
        compiler_params=_cparams(("arbitrary",)),
        name="post",
    )(x2, onsa, oret, mod3, *ops)


def _expert_select(idx_row, table, eidx):
    return jnp.sum(jnp.where(eidx == idx_row, table, 0.0), axis=0, keepdims=True)


def _rank_kernel(idx_ref, tri_ref, rank_ref, cnt_ref, carry):
    @pl.when(pl.program_id(0) == 0)
    def _():
        carry[...] = jnp.zeros_like(carry)

    idx = idx_ref[...]
    eidx = lax.broadcasted_iota(I32, (N_EXPERTS, 1), 0)
    member = jnp.zeros((N_EXPERTS, idx.shape[1]), F32)
    for k in range(TOP_K):
        member = member + (eidx == idx[k:k + 1, :]).astype(F32)
    before = _dot(member.astype(BF16), tri_ref[...]) + carry[:, 0:1]
    rank_ref[...] = jnp.concatenate(
        [_expert_select(idx[k:k + 1, :], before, eidx) for k in range(TOP_K)], axis=0).astype(I32)
    carry[...] = carry[...] + jnp.sum(member, axis=1, keepdims=True)
    cnt_ref[...] = carry[...]


def _dest_kernel(idx_ref, rank_ref, start_ref, pos_ref):
    idx = idx_ref[...]
    eidx = lax.broadcasted_iota(I32, (N_EXPERTS, 1), 0)
    start = start_ref[:, 0:1]
    base = jnp.concatenate([_expert_select(idx[k:k + 1, :], start, eidx) for k in range(TOP_K)], axis=0)
    pos_ref[...] = rank_ref[...] + base.astype(I32)


def _route_plan(idx_t, n_tok):
    tm = min(512, n_tok)
    tri = jnp.asarray(np.triu(np.ones((tm, tm), np.float32), 1), BF16)
    col = pl.BlockSpec((TOP_K, tm), lambda i: (0, i))
    rank, cnt = pl.pallas_call(
        _rank_kernel,
        grid=(n_tok // tm,),
        in_specs=[col, pl.BlockSpec((tm, tm), lambda i: (0, 0))],
        out_specs=[col, pl.BlockSpec((N_EXPERTS, LANES), lambda i: (0, 0))],
        out_shape=[jax.ShapeDtypeStruct((TOP_K, n_tok), I32), jax.ShapeDtypeStruct((N_EXPERTS, LANES), F32)],
        scratch_shapes=[pltpu.VMEM((N_EXPERTS, LANES), F32)],
        compiler_params=_cparams(("arbitrary",)),
        name="route_rank",
    )(idx_t, tri)
    counts = cnt[:, 0].astype(I32)
    padded = ((counts + MOE_TILE - 1) // MOE_TILE) * MOE_TILE
    ends = jnp.cumsum(padded)
    starts = ends - padded
    start_b = jnp.broadcast_to(starts.astype(F32)[:, None], (N_EXPERTS, LANES))
    pos = pl.pallas_call(
        _dest_kernel,
        grid=(n_tok // tm,),
        in_specs=[col, col, pl.BlockSpec((N_EXPERTS, LANES), lambda i: (0, 0))],
        out_specs=col,
        out_shape=jax.ShapeDtypeStruct((TOP_K, n_tok), I32),
        compiler_params=_cparams(("arbitrary",)),
        name="route_dest",
    )(idx_t, rank, start_b)
    n_tiles = TOP_K * n_tok // MOE_TILE + N_EXPERTS
    n_used = (ends[-1] // MOE_TILE).astype(I32)
    tile_start = jnp.minimum(jnp.arange(n_tiles, dtype=I32), n_used - 1) * MOE_TILE
    tile_e = jnp.minimum(jnp.sum((ends[None, :] <= tile_start[:, None]).astype(I32), axis=1), N_EXPERTS - 1)
    first = jnp.concatenate([jnp.ones((1,), I32), (tile_e[1:] != tile_e[:-1]).astype(I32)])
    eidx = jnp.arange(N_EXPERTS, dtype=I32)
    used = counts > 0
    later = jnp.where((eidx[None, :] > eidx[:, None]) & used[None, :], eidx[None, :], N_EXPERTS)
    nxt = jnp.min(later, axis=1)
    next_e = jnp.where(nxt == N_EXPERTS, -1, nxt).astype(I32)
    parity = ((jnp.cumsum(used.astype(I32)) - 1) % 2).astype(I32)
    plan = (tile_e, first, n_used.reshape(1), next_e, parity)
    return pos, plan, (starts + counts).astype(I32), (padded - counts).astype(I32)


def _pad_chunks(n):
    out = []
    b = MOE_TILE // 2
    while b >= 1:
        out.append(((n // (2 * b)) * (2 * b), b))
        b //= 2
    return out


def _rows_to_tiles(ref, val):
    n = val.shape[0]
    for s in range(SUBLANES):
        ref[pl.ds(s, n, stride=SUBLANES), :] = val[:, s * LANES:(s + 1) * LANES]


def _tiles_to_rows(ref, n, *lead):
    return jnp.concatenate([ref[(*lead, pl.ds(s, n, stride=SUBLANES), slice(None))] for s in range(SUBLANES)],
                           axis=1)


def _tile_rows(ref, row, n=1):
    start = pl.multiple_of(row * SUBLANES, SUBLANES)
    return ref.at[pl.ds(start, n * SUBLANES), :]


def _dispatch_kernel(pad_start, pad_n, n_used, pos_ref, h2_ref, xs_hbm, zbuf, sem, zsem, *,
                     experts_per_step, n_tiles):
    i = pl.program_id(0)
    tc = h2_ref.shape[0] // SUBLANES

    @pl.when(i == 0)
    def _():
        zbuf[...] = jnp.zeros_like(zbuf)

    for k in range(TOP_K):
        def body(jj, carry):
            for u in range(8):
                j = jj * 8 + u
                pltpu.make_async_copy(_tile_rows(h2_ref, j), _tile_rows(xs_hbm, pos_ref[0, k, j]), sem).start()
            return carry
        lax.fori_loop(0, tc // 8, body, 0)

    def pad_copies(e, fn):
        n = pad_n[e]
        for off, cnt in _pad_chunks(n):
            @pl.when((n & cnt) != 0)
            def _():
                fn(pltpu.make_async_copy(_tile_rows(zbuf, 0, cnt), _tile_rows(xs_hbm, pad_start[e] + off, cnt),
                                         zsem))

    def tail_copies(tile, fn):
        for half in range(2):
            fn(pltpu.make_async_copy(zbuf, _tile_rows(xs_hbm, tile * MOE_TILE + half * (MOE_TILE // 2),
                                                      MOE_TILE // 2), zsem))

    for s in range(experts_per_step):
        e = i * experts_per_step + s
        tile = n_used[0] + e

        @pl.when(e < N_EXPERTS)
        def _():
            pad_copies(e, lambda cp: cp.start())

        @pl.when(tile < n_tiles)
        def _():
            tail_copies(tile, lambda cp: cp.start())

        @pl.when(e < N_EXPERTS)
        def _():
            pad_copies(e, lambda cp: cp.wait())

        @pl.when(tile < n_tiles)
        def _():
            tail_copies(tile, lambda cp: cp.wait())

    for k in range(TOP_K):
        pltpu.make_async_copy(h2_ref, _tile_rows(xs_hbm, 0, tc), sem).wait()


def _dispatch(h2, pos3, pad_start, pad_n, n_used, n_rows):
    t = h2.shape[0] // SUBLANES
    tc = pos3.shape[2]
    nt = t // tc
    eps = -(-N_EXPERTS // nt)
    grid_spec = pltpu.PrefetchScalarGridSpec(
        num_scalar_prefetch=3,
        grid=(nt,),
        in_specs=[pl.BlockSpec((1, TOP_K, tc), lambda i, ps, pn, nu: (i, 0, 0), memory_space=pltpu.SMEM),
                  pl.BlockSpec((tc * SUBLANES, LANES), lambda i, ps, pn, nu: (i, 0))],
        out_specs=pl.BlockSpec(memory_space=pl.ANY),
        scratch_shapes=[pltpu.VMEM((MOE_TILE // 2 * SUBLANES, LANES), F32), pltpu.SemaphoreType.DMA(()),
                        pltpu.SemaphoreType.DMA(())])
    return pl.pallas_call(
        functools.partial(_dispatch_kernel, experts_per_step=eps, n_tiles=n_rows // MOE_TILE),
        grid_spec=grid_spec,
        out_shape=jax.ShapeDtypeStruct((n_rows * SUBLANES, LANES), F32),
        compiler_params=_cparams(("arbitrary",)),
        name="dispatch",
    )(pad_start, pad_n, n_used, pos3, h2)


def _expert_kernel(tile_e, first, n_used, next_e, parity, x_ref, wg_hbm, wu_hbm, wd_hbm, y_ref,
                   wgf, wuf, wdf, wgb, wub, wdb, sem):
    i = pl.program_id(0)
    n = n_used[0]

    def weight_copies(e, slot):
        return [pltpu.make_async_copy(src.at[e], dst.at[slot], sem.at[slot])
                for src, dst in ((wg_hbm, wgf), (wu_hbm, wuf), (wd_hbm, wdf))]

    @pl.when(i == 0)
    def _():
        for cp in weight_copies(tile_e[0], parity[tile_e[0]]):
            cp.start()

    @pl.when(i < n)
    def _():
        @pl.when(first[i] == 1)
        def _():
            e = tile_e[i]
            slot = parity[e]
            for cp in weight_copies(e, slot):
                cp.wait()

            @pl.when(next_e[e] >= 0)
            def _():
                for cp in weight_copies(next_e[e], 1 - slot):
                    cp.start()

            wgb[...] = wgf[slot].astype(BF16)
            wub[...] = wuf[slot].astype(BF16)
            wdb[...] = wdf[slot].astype(BF16)

        xb = _tiles_to_rows(x_ref, MOE_TILE).astype(BF16)
        hid = (_silu(_dot(xb, wgb[...])) * _dot(xb, wub[...])).astype(BF16)
        _rows_to_tiles(y_ref, _dot(hid, wdb[...]))

    @pl.when(i >= n)
    def _():
        y_ref[...] = jnp.zeros_like(y_ref)


def _experts(xs, plan, wg, wu, wd):
    n_rows = xs.shape[0] // SUBLANES
    n_tiles = n_rows // MOE_TILE
    d, de = wg.shape[1], wg.shape[2]
    blk = (MOE_TILE * SUBLANES, LANES)
    hbm = pl.BlockSpec(memory_space=pl.ANY)
    grid_spec = pltpu.PrefetchScalarGridSpec(
        num_scalar_prefetch=len(plan),
        grid=(n_tiles,),
        in_specs=[pl.BlockSpec(blk, lambda i, te, fi, nu, ne, pa: (jnp.minimum(i, nu[0] - 1), 0)), hbm, hbm, hbm],
        out_specs=pl.BlockSpec(blk, lambda i, te, fi, nu, ne, pa: (i, 0)),
        scratch_shapes=[pltpu.VMEM((2, d, de), F32), pltpu.VMEM((2, d, de), F32), pltpu.VMEM((2, de, d), F32),
                        pltpu.VMEM((d, de), BF16), pltpu.VMEM((d, de), BF16), pltpu.VMEM((de, d), BF16),
                        pltpu.SemaphoreType.DMA((2,))])
    return pl.pallas_call(
        _expert_kernel,
        grid_spec=grid_spec,
        out_shape=jax.ShapeDtypeStruct(xs.shape, F32),
        compiler_params=_cparams(("arbitrary",)),
        name="experts",
    )(*plan, xs, wg, wu, wd)


def _combine_kernel(posc_ref, posn_ref, ys_hbm, acc_ref, w_ref, mod_ref, g_ref, o_ref, buf, sem):
    i = pl.program_id(0)
    n = pl.num_programs(0)
    tc = acc_ref.shape[0]

    def issue(pos_ref, slot):
        for k in range(TOP_K):
            def body(jj, carry):
                for u in range(8):
                    j = jj * 8 + u
                    pltpu.make_async_copy(_tile_rows(ys_hbm, pos_ref[0, k, j]), _tile_rows(buf.at[slot, k], j),
                                          sem.at[slot]).start()
                return carry
            lax.fori_loop(0, tc // 8, body, 0)

    @pl.when(i == 0)
    def _():
        issue(posc_ref, 0)

    @pl.when(i + 1 < n)
    def _():
        issue(posn_ref, (i + 1) % 2)

    slot = i % 2
    for k in range(TOP_K):
        pltpu.make_async_copy(_tile_rows(ys_hbm, 0, tc), buf.at[slot, k], sem.at[slot]).wait()
    w = w_ref[...]
    routed = w[:, 0:1] * _tiles_to_rows(buf, tc, slot, 0)
    for k in range(1, TOP_K):
        routed = routed + w[:, k:k + 1] * _tiles_to_rows(buf, tc, slot, k)
    x2 = acc_ref[...] + mod_ref[0, 5:6, :] * routed
    o_ref[...] = x2 * lax.rsqrt(jnp.mean(x2 * x2, axis=-1, keepdims=True) + RMS_EPS) * g_ref[...]


def _tile_pos(pos_t, tc):
    return pos_t.reshape(TOP_K, pos_t.shape[1] // tc, tc).transpose(1, 0, 2)


def _combine(ys, pos3, acc0, w_tok, mod3, g_final, seq):
    t, d = acc0.shape
    tc = pos3.shape[2]
    nt = t // tc
    return pl.pallas_call(
        _combine_kernel,
        grid=(nt,),
        in_specs=[pl.BlockSpec((1, TOP_K, tc), lambda i: (i, 0, 0), memory_space=pltpu.SMEM),
                  pl.BlockSpec((1, TOP_K, tc), lambda i: (jnp.minimum(i + 1, nt - 1), 0, 0),
                               memory_space=pltpu.SMEM),
                  pl.BlockSpec(memory_space=pl.ANY),
                  pl.BlockSpec((tc, d), lambda i: (i, 0)),
                  pl.BlockSpec((tc, TOP_K), lambda i: (i, 0)),
                  pl.BlockSpec((1, 6, d), lambda i: ((i * tc) // seq, 0, 0)),
                  pl.BlockSpec((1, d), lambda i: (0, 0))],
        out_specs=pl.BlockSpec((tc, d), lambda i: (i, 0)),
        out_shape=jax.ShapeDtypeStruct((t, d), F32),
        scratch_shapes=[pltpu.VMEM((2, TOP_K, tc * SUBLANES, LANES), F32), pltpu.SemaphoreType.DMA((2,))],
        compiler_params=_cparams(("arbitrary",)),
        name="combine",
    )(pos3, pos3, ys, acc0, w_tok, mod3, g_final)


def kernel(x, c, positions, w_ada, b_ada, g_norm_mix, w_in, cmp_pos_k, cmp_w1_k, cmp_w2_k, cmp_pos_v,
           cmp_w1_v, cmp_w2_v, g_nsa_out, g_ret_out, w_out, g_norm_ffn, w_router, router_bias,
           w_gate_e, w_up_e, w_down_e, w_gate_s, w_up_s, w_down_s, g_norm_final):
    bsz, seq, d = x.shape
    assert d == SUBLANES * LANES, "MoE rows are moved as one (8, 128) tile each"
    t = bsz * seq
    x2 = x.reshape(t, d)
    cos_t, sin_t = _rope_tables(positions.reshape(t, 1).astype(F32))
    n_cmp_pad = seq // CMP_STRIDE

    def cmp_rows(tab):
        rows = tab[:, :LANES].reshape(bsz, seq, LANES)[:, CMP_BLOCK - 1::CMP_STRIDE]
        return jnp.pad(rows, ((0, 0), (0, n_cmp_pad - rows.shape[1]), (0, 0)))

    cos_c, sin_c = cmp_rows(cos_t), cmp_rows(sin_t)
    for l in range(w_in.shape[0]):
        mod3 = _ada(c, w_ada[l], b_ada[l]).reshape(bsz, 6, d)
        q, kc, vc, kk, gates, qr, kr, vr, gr, vsel_t, vwin_t = _inproj(
            x2, mod3, g_norm_mix[l].reshape(1, d), _arrange_w_in(w_in[l]), cos_t, sin_t, seq)
        kcmp, vcmp = _compress(kc, vc, cos_c, sin_c,
                               _arrange_cmp_weights(cmp_pos_k[l], cmp_w1_k[l], cmp_w2_k[l]),
                               _arrange_cmp_weights(cmp_pos_v[l], cmp_w1_v[l], cmp_w2_v[l]), bsz, seq)
        onsa = _nsa(q, kcmp, vcmp, kk, vsel_t, vwin_t, gates, g_nsa_out[l].reshape(1, -1), bsz, seq)
        oret = _retention(qr, kr, vr, gr, g_ret_out[l], bsz, seq)
        acc0, h2, idx_t, w_t = _post(x2, onsa, oret, mod3, w_out[l], g_norm_ffn[l].reshape(1, d),
                                     w_router[l], router_bias[l], w_gate_s[l], w_up_s[l], w_down_s[l], seq)
        pos_t, plan, pad_start, pad_n = _route_plan(idx_t, t)
        n_rows = TOP_K * t + N_EXPERTS * MOE_TILE
        xs = _dispatch(h2, _tile_pos(pos_t, min(256, seq)), pad_start, pad_n, plan[2], n_rows)
        ys = _experts(xs, plan, w_gate_e[l], w_up_e[l], w_down_e[l])
        last = l == w_in.shape[0] - 1
        gfin = g_norm_final.reshape(1, d)
        x2 = _combine(ys, _tile_pos(pos_t, 128), acc0, w_t.T, mod3, gfin, seq)
        assert last, "final norm is fused into the combine stage; depth 1 only"
    return x2.reshape(bsz, seq, d)
```

```python
import functools
import math

import numpy as np
import jax
import jax.numpy as jnp
from jax import lax
from jax.experimental import pallas as pl
from jax.experimental.pallas import tpu as pltpu

F32 = jnp.float32
BF16 = jnp.bfloat16
I32 = jnp.int32
HIGHEST = lax.Precision.HIGHEST

LANES = 128
SUBLANES = 8
BF16_TILE_ROWS = 16
NSA_HEAD_DIM = 64
NSA_HEADS = 8
NSA_GROUPS = 2
NSA_Q_PER_KV = NSA_HEADS // NSA_GROUPS
CMP_BLOCK = 32
CMP_STRIDE = 16
CMP_HIDDEN = 128
SEL_BLOCK = 64
SEL_TOP = 8
WINDOW = 512
Q_BLOCK = 128
FORCE_BONUS = 1.0e4
RET_HEADS = 4
RET_HEAD_DIM = 128
RET_CHUNK = 128
ROPE_THETA = 10000.0
RMS_EPS = 1e-6
N_EXPERTS = 256
N_EXPERT_GROUPS = 8
TOPK_GROUPS = 4
TOP_K = 8
ROUTED_SCALE = 2.5
MOE_TILE = 256
SEL_CHUNK = 256
MASK_BIG = 2.0 ** 100
MASK_FLOOR = 2.0 ** 99
NEG = -1e30
VMEM_LIMIT = 48 * 1024 * 1024
EXPERTS_VMEM_LIMIT = 56 * 1024 * 1024


def _cparams(sem):
    return pltpu.CompilerParams(dimension_semantics=sem, vmem_limit_bytes=VMEM_LIMIT)


def _dot(a, b, **kw):
    return jnp.dot(a, b, preferred_element_type=F32, **kw)


def _dot_nt(a, b, **kw):
    return lax.dot_general(a, b, (((1,), (1,)), ((), ())), preferred_element_type=F32, **kw)


def _sigmoid(v):
    return 1.0 / (1.0 + jnp.exp(-v))


def _silu(v):
    return v * _sigmoid(v)


def _ada_kernel(c_ref, w_ref, b_ref, o_ref):
    o_ref[...] = _dot(_silu(c_ref[...]), w_ref[...], precision=HIGHEST) + b_ref[...]


def _ada(c, w, b):
    bsz, d = c.shape
    n = w.shape[1]
    tn = 1536
    cp = jnp.zeros((8, d), F32).at[:bsz].set(c)
    out = pl.pallas_call(
        _ada_kernel,
        grid=(n // tn,),
        in_specs=[pl.BlockSpec((8, d), lambda j: (0, 0)),
                  pl.BlockSpec((d, tn), lambda j: (0, j)),
                  pl.BlockSpec((1, tn), lambda j: (0, j))],
        out_specs=pl.BlockSpec((8, tn), lambda j: (0, j)),
        out_shape=jax.ShapeDtypeStruct((8, n), F32),
        compiler_params=_cparams(("arbitrary",)),
        name="ada",
    )(cp, w, b.reshape(1, n))
    return out[:bsz]


def _rope_table_kernel(pos_ref, inv_ref, sgn_ref, cos_ref, sin_ref):
    ang = pos_ref[...] * inv_ref[...]
    cos_ref[...] = jnp.cos(ang)
    sin_ref[...] = jnp.sin(ang) * sgn_ref[...]


def _rope_tables(posf):
    t = posf.shape[0]
    inv_n = ROPE_THETA ** (-jnp.arange(0, NSA_HEAD_DIM, 2, dtype=F32) / NSA_HEAD_DIM)
    inv_r = ROPE_THETA ** (-jnp.arange(0, RET_HEAD_DIM, 2, dtype=F32) / RET_HEAD_DIM)
    inv = jnp.concatenate([jnp.tile(inv_n, 4), jnp.tile(inv_r, 2)]).reshape(1, 2 * LANES)
    sgn_n = np.where((np.arange(LANES) % NSA_HEAD_DIM) < NSA_HEAD_DIM // 2, -1.0, 1.0)
    sgn_r = np.where(np.arange(LANES) < RET_HEAD_DIM // 2, -1.0, 1.0)
    sgn = jnp.asarray(np.concatenate([sgn_n, sgn_r]).reshape(1, 2 * LANES), F32)
    tm = min(t, 1024)
    return pl.pallas_call(
        _rope_table_kernel,
        grid=(t // tm,),
        in_specs=[pl.BlockSpec((tm, 1), lambda i: (i, 0)),
                  pl.BlockSpec((1, 2 * LANES), lambda i: (0, 0)),
                  pl.BlockSpec((1, 2 * LANES), lambda i: (0, 0))],
        out_specs=[pl.BlockSpec((tm, 2 * LANES), lambda i: (i, 0))] * 2,
        out_shape=[jax.ShapeDtypeStruct((t, 2 * LANES), F32)] * 2,
        compiler_params=_cparams(("arbitrary",)),
        name="rope_tables",
    )(posf, inv, sgn)


def _rope64(p, cos, sin, first_half):
    rot = jnp.where(first_half, pltpu.roll(p, 96, 1), pltpu.roll(p, 32, 1))
    return p * cos + rot * sin


def _rope128(p, cos, sin):
    return p * cos + pltpu.roll(p, 64, 1) * sin


_C_Q = 0
_C_KV = _C_Q + NSA_HEADS * LANES
_C_GATE = _C_KV + 6 * LANES
_C_RET = _C_GATE + LANES
_C_END = _C_RET + 4 * RET_HEADS * RET_HEAD_DIM


def _arrange_w_in(w_in):
    d = w_in.shape[0]
    nw = NSA_HEADS * NSA_HEAD_DIM
    q = w_in[:, :nw].reshape(d, NSA_HEADS, NSA_HEAD_DIM)
    z = jnp.zeros_like(q)
    grp = (jnp.arange(NSA_HEADS) // NSA_Q_PER_KV)[None, :, None]
    qpad = jnp.where(grp == 0, jnp.concatenate([q, z], -1), jnp.concatenate([z, q], -1))
    qpad = qpad.reshape(d, NSA_HEADS * LANES)
    kv = w_in[:, nw:nw + 6 * LANES]
    g0 = nw + 6 * LANES
    gate = jnp.pad(w_in[:, g0:g0 + 3 * NSA_HEADS], ((0, 0), (0, LANES - 3 * NSA_HEADS)))
    ret = w_in[:, g0 + 3 * NSA_HEADS:]
    return jnp.concatenate([qpad, kv, gate, ret], axis=1).astype(BF16)


def _inproj_kernel(x_ref, mod_ref, g_ref, w_ref, cos_ref, sin_ref,
                   q_ref, kc_ref, vc_ref, kk_ref, gate_ref, qr_ref, kr_ref, vr_ref, gr_ref, vst_ref, vwt_ref):
    x = x_ref[...]
    tm = x.shape[0]
    ms = jnp.mean(x * x, axis=-1, keepdims=True)
    y = x * lax.rsqrt(ms + RMS_EPS) * g_ref[...]
    h = y * (1.0 + mod_ref[0, 1:2, :]) + mod_ref[0, 0:1, :]
    hb = h.astype(BF16)
    cos_n, sin_n = cos_ref[:, 0:LANES], sin_ref[:, 0:LANES]
    cos_r, sin_r = cos_ref[:, LANES:], sin_ref[:, LANES:]
    lane = lax.broadcasted_iota(I32, (tm, LANES), 1)
    first_half = (lane % NSA_HEAD_DIM) < (NSA_HEAD_DIM // 2)
    scale_n = NSA_HEAD_DIM ** -0.5
    scale_r = RET_HEAD_DIM ** -0.5

    def proj(c0, n):
        return _dot(hb, w_ref[:, c0:c0 + n])

    for hh in range(NSA_HEADS):
        p = proj(_C_Q + hh * LANES, LANES)
        q_ref[:, hh * LANES:(hh + 1) * LANES] = (_rope64(p, cos_n, sin_n, first_half) * scale_n).astype(BF16)
    kv = proj(_C_KV, 6 * LANES)
    kc_ref[...] = kv[:, 0:LANES].astype(BF16)
    vc_ref[...] = kv[:, LANES:2 * LANES].astype(BF16)
    kk_ref[:, 0:LANES] = _rope64(kv[:, 2 * LANES:3 * LANES], cos_n, sin_n, first_half).astype(BF16)
    kk_ref[:, LANES:2 * LANES] = _rope64(kv[:, 4 * LANES:5 * LANES], cos_n, sin_n, first_half).astype(BF16)
    group0 = lane < NSA_HEAD_DIM
    for c0, vt_ref, chunk in ((3 * LANES, vst_ref, SEL_CHUNK), (5 * LANES, vwt_ref, Q_BLOCK)):
        v = kv[:, c0:c0 + LANES]
        for g, vg in enumerate((jnp.where(group0, v, 1.0), jnp.where(group0, 1.0, v))):
            for cc in range(tm // chunk):
                vt_ref[cc, g] = vg[cc * chunk:(cc + 1) * chunk].T.astype(BF16)
    gate_ref[...] = _sigmoid(proj(_C_GATE, LANES))
    rw = RET_HEADS * RET_HEAD_DIM
    for hh in range(RET_HEADS):
        sl = slice(hh * LANES, (hh + 1) * LANES)
        pq = proj(_C_RET + hh * LANES, LANES)
        qr_ref[:, sl] = _rope128(pq, cos_r, sin_r).astype(BF16)
        pk = proj(_C_RET + rw + hh * LANES, LANES)
        kr_ref[:, sl] = (_rope128(pk, cos_r, sin_r) * scale_r).astype(BF16)
    vr_ref[...] = proj(_C_RET + 2 * rw, rw).astype(BF16)
    gr_ref[...] = proj(_C_RET + 3 * rw, rw)


def _inproj(x2, mod3, g_mix, w_in_p, cos_t, sin_t, seq):
    t, d = x2.shape
    tm = min(512, seq)
    rw = RET_HEADS * RET_HEAD_DIM
    row = lambda n: pl.BlockSpec((tm, n), lambda i: (i, 0))
    outs = [(NSA_HEADS * LANES, BF16), (LANES, BF16), (LANES, BF16), (2 * LANES, BF16), (LANES, F32),
            (rw, BF16), (rw, BF16), (rw, BF16), (rw, F32)]
    vt_specs, vt_shapes = [], []
    for chunk in (SEL_CHUNK, Q_BLOCK):
        vt_specs.append(pl.BlockSpec((tm // chunk, NSA_GROUPS, LANES, chunk), lambda i: (i, 0, 0, 0)))
        vt_shapes.append(jax.ShapeDtypeStruct((t // chunk, NSA_GROUPS, LANES, chunk), BF16))
    return pl.pallas_call(
        _inproj_kernel,
        grid=(t // tm,),
        in_specs=[row(d),
                  pl.BlockSpec((1, 6, d), lambda i: ((i * tm) // seq, 0, 0)),
                  pl.BlockSpec((1, d), lambda i: (0, 0)),
                  pl.BlockSpec((d, _C_END), lambda i: (0, 0)),
                  row(2 * LANES), row(2 * LANES)],
        out_specs=[row(n) for n, _ in outs] + vt_specs,
        out_shape=[jax.ShapeDtypeStruct((t, n), dt) for n, dt in outs] + vt_shapes,
        compiler_params=_cparams(("arbitrary",)),
        name="inproj",
    )(x2, mod3, g_mix, w_in_p, cos_t, sin_t)


def _gelu_tanh(v):
    return v * (0.5 * (1.0 + jnp.tanh(math.sqrt(2.0 / math.pi) * (v + 0.044715 * (v * v * v)))))


def _compress_kernel(kc_ref, vc_ref, wkt_ref, wkb_ref, wk2_ref, pk_ref, wvt_ref, wvb_ref, wv2_ref, pv_ref,
                     cos_ref, sin_ref, ko_ref, vo_ref):
    def one(x_ref, wt_ref, wb_ref, w2_ref, p_ref):
        xx = x_ref[0]
        a = _dot(xx, wt_ref[...])
        b = _dot(xx, wb_ref[...])
        pb = p_ref[...].astype(BF16)
        bias = _dot(pb, wt_ref[...])[0:1] + _dot(pb, wb_ref[...])[1:2]
        hid = a + pltpu.roll(b, b.shape[0] - 1, 0) + bias
        return _dot(_gelu_tanh(hid).astype(BF16), w2_ref[...])

    k = one(kc_ref, wkt_ref, wkb_ref, wk2_ref, pk_ref)
    lane = lax.broadcasted_iota(I32, k.shape, 1)
    first_half = (lane % NSA_HEAD_DIM) < (NSA_HEAD_DIM // 2)
    ko_ref[0] = _rope64(k, cos_ref[0], sin_ref[0], first_half).astype(BF16)
    v = one(vc_ref, wvt_ref, wvb_ref, wv2_ref, pv_ref)
    group0 = lane < NSA_HEAD_DIM
    vo_ref[0, 0] = jnp.where(group0, v, 1.0).T.astype(BF16)
    vo_ref[0, 1] = jnp.where(group0, 1.0, v).T.astype(BF16)


def _arrange_cmp_weights(pos, w1, w2):
    half = CMP_BLOCK // 2
    dh = NSA_HEAD_DIM
    w1r = w1.reshape(CMP_BLOCK, dh, CMP_HIDDEN)

    def block(wpart):
        z = jnp.zeros_like(wpart)
        g0 = jnp.concatenate([wpart, z], axis=-1)
        g1 = jnp.concatenate([z, wpart], axis=-1)
        return jnp.stack([g0, g1], axis=1).reshape(half * 2 * dh, 2 * CMP_HIDDEN)

    wt, wb = block(w1r[:half]), block(w1r[half:])
    z2 = jnp.zeros_like(w2)
    w2b = jnp.concatenate([jnp.concatenate([w2, z2], 1), jnp.concatenate([z2, w2], 1)], 0)
    ptop = jnp.tile(pos[:half], (1, 2)).reshape(1, -1)
    pbot = jnp.tile(pos[half:], (1, 2)).reshape(1, -1)
    prow = jnp.concatenate([ptop, pbot, jnp.zeros((6, ptop.shape[1]), F32)], 0)
    return wt.astype(BF16), wb.astype(BF16), w2b.astype(BF16), prow


def _compress(kc, vc, cos_c, sin_c, kparams, vparams, bsz, seq):
    nchunk = seq // CMP_STRIDE
    width = CMP_STRIDE * LANES
    kc16 = kc.reshape(bsz, nchunk, width)
    vc16 = vc.reshape(bsz, nchunk, width)
    full = lambda a: pl.BlockSpec(a.shape, lambda b: (0,) * a.ndim)
    per_b = lambda n: pl.BlockSpec((1, nchunk, n), lambda b: (b, 0, 0))
    return pl.pallas_call(
        _compress_kernel,
        grid=(bsz,),
        in_specs=[per_b(width), per_b(width)] + [full(a) for a in kparams] + [full(a) for a in vparams]
                 + [per_b(LANES), per_b(LANES)],
        out_specs=[per_b(LANES), pl.BlockSpec((1, NSA_GROUPS, LANES, nchunk), lambda b: (b, 0, 0, 0))],
        out_shape=[jax.ShapeDtypeStruct((bsz, nchunk, LANES), BF16),
                   jax.ShapeDtypeStruct((bsz, NSA_GROUPS, LANES, nchunk), BF16)],
        compiler_params=_cparams(("arbitrary",)),
        name="compress",
    )(kc16, vc16, *kparams, *vparams, cos_c, sin_c)


def _softmax_chunk(m, acc, s, vt):
    m_new = jnp.maximum(m, jnp.max(s, axis=0, keepdims=True))
    e = jnp.exp(s - m_new).astype(BF16)
    acc = jnp.exp(m - m_new) * acc + _dot(vt, e)
    return m_new, acc


def _nsa_kernel(q_ref, kc_ref, vct_ref, ks_ref, kw_ref, vst_ref, vwt_ref, gate_ref,
                ovt_ref, exp_ref, g_ref, o_ref, *, seq):
    qb = pl.program_id(1)
    t0 = qb * Q_BLOCK
    cols = NSA_Q_PER_KV * Q_BLOCK
    n_cmp_pad = kc_ref.shape[1]
    n_sel = seq // SEL_BLOCK
    t_row = t0 + lax.broadcasted_iota(I32, (1, Q_BLOCK), 1)
    gates_t = gate_ref[...].T
    lane = lax.broadcasted_iota(I32, (Q_BLOCK, LANES), 1)
    key = lax.broadcasted_iota(I32, (Q_BLOCK, Q_BLOCK), 0)
    tok = lax.broadcasted_iota(I32, (Q_BLOCK, Q_BLOCK), 1)
    cend = lax.broadcasted_iota(I32, (n_cmp_pad, 1), 0) * CMP_STRIDE + (CMP_BLOCK - 1)
    bias_cmp = jnp.where(cend <= t_row, 0.0, -MASK_BIG)
    last_c = qb // (SEL_CHUNK // Q_BLOCK)
    kpos_last = last_c * SEL_CHUNK + lax.broadcasted_iota(I32, (SEL_CHUNK, 1), 0)
    bias_diag = jnp.where(kpos_last <= t_row, 0.0, -MASK_BIG)
    n_win = WINDOW // Q_BLOCK
    bias_win_first = jnp.where(key > tok, 0.0, -MASK_BIG)
    bias_win_last = jnp.where(key <= tok, 0.0, -MASK_BIG)
    init = (jnp.full((1, cols), -MASK_FLOOR, F32), jnp.zeros((LANES, cols), F32))
    tile_heads = lambda b: jnp.concatenate([b] * NSA_Q_PER_KV, axis=1)

    def finish(acc, g):
        ones_row = NSA_HEAD_DIM * (1 - g)
        return acc * (1.0 / jnp.maximum(acc[ones_row:ones_row + 1, :], 1e-20))

    q4s, o_cs, sel_bs = [], [], []
    for g in range(NSA_GROUPS):
        q4 = jnp.concatenate([q_ref[:, (NSA_Q_PER_KV * g + r) * LANES:(NSA_Q_PER_KV * g + r + 1) * LANES]
                              for r in range(NSA_Q_PER_KV)], axis=0)
        q4s.append(q4)
        s_c = _dot_nt(kc_ref[0], q4) + tile_heads(bias_cmp)
        e_c = jnp.exp(s_c - jnp.maximum(jnp.max(s_c, axis=0, keepdims=True), -MASK_FLOOR))
        p_c = e_c * (1.0 / jnp.maximum(jnp.sum(e_c, axis=0, keepdims=True), 1e-20))
        psum = p_c[:, 0:Q_BLOCK]
        for r in range(1, NSA_Q_PER_KV):
            psum = psum + p_c[:, r * Q_BLOCK:(r + 1) * Q_BLOCK]
        o_cs.append(finish(_dot(vct_ref[0, g], e_c.astype(BF16)), g))
        imp_t = _dot(ovt_ref[...], psum, precision=HIGHEST)
        nsp = imp_t.shape[0]
        jrow = lax.broadcasted_iota(I32, (nsp, 1), 0)
        cur = t_row // SEL_BLOCK
        valid = (jrow * SEL_BLOCK <= t_row) & (jrow < n_sel)
        forced = (jrow == 0) | (jrow == cur) | (jrow == cur - 1)
        val = jnp.where(valid, imp_t + jnp.where(forced, FORCE_BONUS, 0.0), -1.0)
        val = jnp.where(jrow < n_sel, val, -jnp.inf)
        sel_t = jnp.zeros((nsp, Q_BLOCK), F32)
        for _ in range(min(SEL_TOP, n_sel)):
            mx = jnp.max(val, axis=0, keepdims=True)
            jmin = jnp.min(jnp.where(val == mx, jrow, nsp), axis=0, keepdims=True)
            hit = jrow == jmin
            sel_t = jnp.where(hit, 1.0, sel_t)
            val = jnp.where(hit, -jnp.inf, val)
        sel_bs.append(jnp.concatenate([sel_t[:n_sel], jnp.ones((SUBLANES, Q_BLOCK), F32),
                                       jnp.zeros((LANES - n_sel - SUBLANES, Q_BLOCK), F32)], axis=0).astype(BF16))

    def sel_scores(c):
        keys = ks_ref[pl.ds(pl.multiple_of(c * SEL_CHUNK, SEL_CHUNK), SEL_CHUNK), :]
        expand = exp_ref[c]
        return tuple(_dot_nt(keys, q4s[g]) + tile_heads(_dot(expand, sel_bs[g])) for g in range(NSA_GROUPS))

    def sel_reduce(c, state, scores):
        out = []
        for g in range(NSA_GROUPS):
            out.extend(_softmax_chunk(state[2 * g], state[2 * g + 1], scores[g], vst_ref[c, g]))
        return tuple(out)

    def sel_step(c, carry):
        state, scores = carry
        return sel_reduce(c, state, scores), sel_scores(c + 1)

    state, scores = lax.fori_loop(0, last_c, sel_step, (init * NSA_GROUPS, sel_scores(0)))
    diag = tile_heads(bias_diag)
    state = sel_reduce(last_c, state, tuple(s + diag for s in scores))
    o_ss = [finish(state[2 * g + 1], g) for g in range(NSA_GROUPS)]
    s_ws = [[] for _ in range(NSA_GROUPS)]
    kbs = []
    for w in range(n_win + 1):
        wb = qb - n_win + w
        kb = jnp.maximum(wb, 0)
        kbs.append(kb)
        keys = kw_ref[pl.ds(pl.multiple_of(kb * Q_BLOCK, Q_BLOCK), Q_BLOCK), :]
        if w == n_win:
            bias = bias_win_last
        else:
            before_start = jnp.where(wb >= 0, 0.0, -MASK_BIG)
            bias = (bias_win_first + before_start) if w == 0 else jnp.full((Q_BLOCK, Q_BLOCK), before_start)
        for g in range(NSA_GROUPS):
            s_ws[g].append(_dot_nt(keys, q4s[g]) + tile_heads(bias))
    o_ws = []
    for g in range(NSA_GROUPS):
        m_w = jnp.full((1, cols), -MASK_FLOOR, F32)
        for s in s_ws[g]:
            m_w = jnp.maximum(m_w, jnp.max(s, axis=0, keepdims=True))
        acc = None
        for kb, s in zip(kbs, s_ws[g]):
            pv = _dot(vwt_ref[kb, g], jnp.exp(s - m_w).astype(BF16))
            acc = pv if acc is None else acc + pv
        o_ws.append(finish(acc, g))
    heads_out = []
    for g in range(NSA_GROUPS):
        o_c, o_s, o_w = o_cs[g], o_ss[g], o_ws[g]
        for r in range(NSA_Q_PER_KV):
            hh = NSA_Q_PER_KV * g + r
            cs = slice(r * Q_BLOCK, (r + 1) * Q_BLOCK)
            o = (gates_t[3 * hh:3 * hh + 1, :] * o_c[:, cs] + gates_t[3 * hh + 1:3 * hh + 2, :] * o_s[:, cs]
                 + gates_t[3 * hh + 2:3 * hh + 3, :] * o_w[:, cs]).T
            if (hh % 2) != g:
                o = pltpu.roll(o, NSA_HEAD_DIM, 1)
            heads_out.append(o)
    blocks = [jnp.where(lane < NSA_HEAD_DIM, heads_out[2 * i], heads_out[2 * i + 1])
              for i in range(NSA_HEADS // 2)]
    ss = sum(jnp.sum(b * b, axis=-1, keepdims=True) for b in blocks)
    inv = lax.rsqrt(ss / (NSA_HEADS * NSA_HEAD_DIM) + RMS_EPS)
    for i, b in enumerate(blocks):
        sl = slice(i * LANES, (i + 1) * LANES)
        o_ref[:, sl] = (b * inv * g_ref[:, sl]).astype(BF16)


def _nsa(q, kcmp, vcmp_t, kk, vsel_t, vwin_t, gates, g_nsa, bsz, seq):
    t = bsz * seq
    nq = seq // Q_BLOCK
    n_cmp_pad = seq // CMP_STRIDE
    n_cmp = (seq - CMP_BLOCK) // CMP_STRIDE + 1
    n_sel = seq // SEL_BLOCK
    assert n_sel % SUBLANES == 0 and n_sel + SUBLANES <= LANES, "selection mask needs a spare expansion row"
    nsp = n_sel
    cs = np.arange(n_cmp_pad) * CMP_STRIDE
    ss = np.arange(nsp) * SEL_BLOCK
    ov = ((cs[None, :] < ss[:, None] + SEL_BLOCK) & (cs[None, :] + CMP_BLOCK > ss[:, None])
          & (np.arange(n_cmp_pad)[None, :] < n_cmp) & (np.arange(nsp)[:, None] < n_sel))
    ovt = jnp.asarray(ov, F32)
    nch = seq // SEL_CHUNK
    kp = np.arange(seq).reshape(nch, SEL_CHUNK, 1)
    col = np.arange(LANES).reshape(1, 1, LANES)
    expand = jnp.asarray(np.where(kp // SEL_BLOCK == col, MASK_BIG, 0.0) + np.where(col == n_sel, -MASK_BIG, 0.0),
                         BF16)
    seqcol = lambda c: pl.BlockSpec((seq, LANES), lambda b, i: (b, c))
    per_b = lambda a: pl.BlockSpec((a.shape[0] // bsz,) + a.shape[1:], lambda b, i: (b,) + (0,) * (a.ndim - 1))
    return pl.pallas_call(
        functools.partial(_nsa_kernel, seq=seq),
        grid=(bsz, nq),
        in_specs=[pl.BlockSpec((Q_BLOCK, NSA_HEADS * LANES), lambda b, i: (b * nq + i, 0)),
                  per_b(kcmp), per_b(vcmp_t), seqcol(0), seqcol(1), per_b(vsel_t), per_b(vwin_t),
                  pl.BlockSpec((Q_BLOCK, LANES), lambda b, i: (b * nq + i, 0)),
                  pl.BlockSpec(ovt.shape, lambda b, i: (0, 0)),
                  pl.BlockSpec(expand.shape, lambda b, i: (0, 0, 0)),
                  pl.BlockSpec((1, NSA_HEADS * NSA_HEAD_DIM), lambda b, i: (0, 0))],
        out_specs=pl.BlockSpec((Q_BLOCK, NSA_HEADS * NSA_HEAD_DIM), lambda b, i: (b * nq + i, 0)),
        out_shape=jax.ShapeDtypeStruct((t, NSA_HEADS * NSA_HEAD_DIM), BF16),
        compiler_params=_cparams(("arbitrary", "arbitrary")),
        name="nsa",
    )(q, kcmp, vcmp_t, kk, kk, vsel_t, vwin_t, gates, ovt, expand, g_nsa)


def _ret_kernel(q_ref, k_ref, v_ref, gr_ref, dec_ref, xi_ref, zeta_ref, gch_ref, g_ref, o_ref, st_ref):
    @pl.when(pl.program_id(1) == 0)
    def _():
        st_ref[...] = jnp.zeros_like(st_ref)

    for hh in range(RET_HEADS):
        sl = slice(hh * LANES, (hh + 1) * LANES)
        q, k, v = q_ref[:, sl], k_ref[:, sl], v_ref[:, sl]
        sc = _dot_nt(q, k) * dec_ref[hh]
        inner = _dot(sc.astype(BF16), v)
        st = st_ref[hh]
        cross = _dot((q.astype(F32) * xi_ref[hh]).astype(BF16), st.astype(BF16))
        kz = (k.astype(F32) * zeta_ref[hh]).T.astype(BF16)
        st_ref[hh] = st * gch_ref[hh] + _dot(kz, v)
        o = inner + cross
        y = o * lax.rsqrt(jnp.mean(o * o, axis=-1, keepdims=True) + RMS_EPS) * g_ref[hh:hh + 1, :]
        o_ref[:, sl] = (_silu(gr_ref[:, sl]) * y).astype(BF16)


def _retention(qr, kr, vr, gr, g_ret, bsz, seq):
    t = bsz * seq
    c = RET_CHUNK
    n = seq // c
    log_g = jnp.log(1.0 - 2.0 ** (-5.0 - jnp.arange(RET_HEADS, dtype=F32)))
    i = jnp.arange(c, dtype=F32)
    diff = i[:, None] - i[None, :]
    causal = diff >= 0
    dec = jnp.where(causal, jnp.exp(log_g[:, None, None] * jnp.where(causal, diff, 0.0)), 0.0)
    xi = jnp.broadcast_to(jnp.exp(log_g[:, None] * (i + 1.0))[:, :, None], (RET_HEADS, c, LANES))
    zeta = jnp.broadcast_to(jnp.exp(log_g[:, None] * (c - 1.0 - i))[:, :, None], (RET_HEADS, c, LANES))
    gch = jnp.broadcast_to(jnp.exp(log_g * c)[:, None, None], (RET_HEADS, 1, LANES))
    w = RET_HEADS * RET_HEAD_DIM
    row = pl.BlockSpec((c, w), lambda b, j: (b * n + j, 0))
    full = lambda a: pl.BlockSpec(a.shape, lambda b, j: (0,) * a.ndim)
    return pl.pallas_call(
        _ret_kernel,
        grid=(bsz, n),
        in_specs=[row, row, row, row, full(dec), full(xi), full(zeta), full(gch), full(g_ret)],
        out_specs=row,
        out_shape=jax.ShapeDtypeStruct((t, w), BF16),
        scratch_shapes=[pltpu.VMEM((RET_HEADS, RET_HEAD_DIM, RET_HEAD_DIM), F32)],
        compiler_params=_cparams(("arbitrary", "arbitrary")),
        name="retention",
    )(qr, kr, vr, gr, dec, xi, zeta, gch, g_ret)


def _post_kernel(x_ref, onsa_ref, oret_ref, mod_ref, wo1_ref, wo2_ref, gffn_ref, wrt_ref, rb_ref,
                 wgs_ref, wus_ref, wds_ref, acc_ref, h2_ref, idx_ref, wt_ref):
    mix = _dot(onsa_ref[...], wo1_ref[...]) + _dot(oret_ref[...], wo2_ref[...])
    x1 = x_ref[...] + mod_ref[0, 2:3, :] * mix
    ms = jnp.mean(x1 * x1, axis=-1, keepdims=True)
    h2 = x1 * lax.rsqrt(ms + RMS_EPS) * gffn_ref[...] * (1.0 + mod_ref[0, 4:5, :]) + mod_ref[0, 3:4, :]
    _rows_to_tiles(h2_ref, h2)
    hb = h2.astype(BF16)
    hid = (_silu(_dot(hb, wgs_ref[...])) * _dot(hb, wus_ref[...])).astype(BF16)
    acc_ref[...] = x1 + mod_ref[0, 5:6, :] * _dot(hid, wds_ref[...])
    s = _sigmoid(_dot_nt(wrt_ref[...], h2, precision=HIGHEST))
    sb = s + rb_ref[...]
    per = N_EXPERTS // N_EXPERT_GROUPS
    ridx = lax.broadcasted_iota(I32, (per, 1), 0)
    blks, grp = [], []
    for gi in range(N_EXPERT_GROUPS):
        blk = sb[gi * per:(gi + 1) * per]
        m1 = jnp.max(blk, axis=0, keepdims=True)
        first = jnp.min(jnp.where(blk == m1, ridx, per), axis=0, keepdims=True)
        m2 = jnp.max(jnp.where(ridx == first, -jnp.inf, blk), axis=0, keepdims=True)
        blks.append(blk)
        grp.append(m1 + m2)
    masked = []
    for gi in range(N_EXPERT_GROUPS):
        rank = jnp.zeros_like(grp[gi])
        for gj in range(N_EXPERT_GROUPS):
            if gj < gi:
                rank = rank + (grp[gj] >= grp[gi]).astype(F32)
            elif gj > gi:
                rank = rank + (grp[gj] > grp[gi]).astype(F32)
        masked.append(jnp.where(rank < TOPK_GROUPS, blks[gi], NEG))
    val = jnp.concatenate(masked, axis=0)
    eidx = lax.broadcasted_iota(I32, (N_EXPERTS, 1), 0)
    ids, ws = [], []
    for _ in range(TOP_K):
        mx = jnp.max(val, axis=0, keepdims=True)
        emin = jnp.min(jnp.where(val == mx, eidx, N_EXPERTS), axis=0, keepdims=True)
        hit = eidx == emin
        ids.append(emin)
        ws.append(jnp.sum(jnp.where(hit, s, 0.0), axis=0, keepdims=True))
        val = jnp.where(hit, -jnp.inf, val)
    wsum = ws[0]
    for wk in ws[1:]:
        wsum = wsum + wk
    idx_ref[...] = jnp.concatenate(ids, axis=0)
    wt_ref[...] = jnp.concatenate(ws, axis=0) / wsum * ROUTED_SCALE


def _post(x2, onsa, oret, mod3, w_out, g_ffn, w_router, router_bias, wgs, wus, wds, seq):
    t, d = x2.shape
    tm = min(512, seq)
    hw = onsa.shape[1]
    wo1 = w_out[:hw].astype(BF16)
    wo2 = w_out[hw:].astype(BF16)
    wrt = w_router.T
    rb = jnp.broadcast_to(router_bias.reshape(N_EXPERTS, 1), (N_EXPERTS, tm))
    row = lambda n: pl.BlockSpec((tm, n), lambda i: (i, 0))
    full = lambda a: pl.BlockSpec(a.shape, lambda i: (0,) * a.ndim)
    col = pl.BlockSpec((TOP_K, tm), lambda i: (0, i))
    ops = (wo1, wo2, g_ffn, wrt, rb, wgs.astype(BF16), wus.astype(BF16), wds.astype(BF16))
    return pl.pallas_call(
        _post_kernel,
        grid=(t // tm,),
        in_specs=[row(d), row(hw), row(oret.shape[1]),
                  pl.BlockSpec((1, 6, d), lambda i: ((i * tm) // seq, 0, 0))] + [full(a) for a in ops],
        out_specs=[row(d), pl.BlockSpec((tm * SUBLANES, LANES), lambda i: (i, 0)), col, col],
        out_shape=[jax.ShapeDtypeStruct((t, d), F32), jax.ShapeDtypeStruct((t * SUBLANES, LANES), F32),
                   jax.ShapeDtypeStruct((TOP_K, t), I32), jax.ShapeDtypeStruct((TOP_K, t), F32)],
        compiler_params=_cparams(("arbitrary",)),
        name="post",
    )(x2, onsa, oret, mod3, *ops)


def _expert_select(idx_row, table, eidx):
    return jnp.sum(jnp.where(eidx == idx_row, table, 0.0), axis=0, keepdims=True)


def _rank_kernel(idx_ref, tri_ref, rank_ref, cnt_ref, carry):
    @pl.when(pl.program_id(0) == 0)
    def _():
        carry[...] = jnp.zeros_like(carry)

    idx = idx_ref[...]
    eidx = lax.broadcasted_iota(I32, (N_EXPERTS, 1), 0)
    member = jnp.zeros((N_EXPERTS, idx.shape[1]), F32)
    for k in range(TOP_K):
        member = member + (eidx == idx[k:k + 1, :]).astype(F32)
    before = _dot(member.astype(BF16), tri_ref[...]) + carry[:, 0:1]
    rank_ref[...] = jnp.concatenate(
        [_expert_select(idx[k:k + 1, :], before, eidx) for k in range(TOP_K)], axis=0).astype(I32)
    carry[...] = carry[...] + jnp.sum(member, axis=1, keepdims=True)
    cnt_ref[...] = carry[...]


def _dest_kernel(idx_ref, rank_ref, start_ref, pos_ref):
    idx = idx_ref[...]
    eidx = lax.broadcasted_iota(I32, (N_EXPERTS, 1), 0)
    start = start_ref[:, 0:1]
    base = jnp.concatenate([_expert_select(idx[k:k + 1, :], start, eidx) for k in range(TOP_K)], axis=0)
    pos_ref[...] = rank_ref[...] + base.astype(I32)


def _route_plan(idx_t, n_tok):
    tm = min(512, n_tok)
    tri = jnp.asarray(np.triu(np.ones((tm, tm), np.float32), 1), BF16)
    col = pl.BlockSpec((TOP_K, tm), lambda i: (0, i))
    rank, cnt = pl.pallas_call(
        _rank_kernel,
        grid=(n_tok // tm,),
        in_specs=[col, pl.BlockSpec((tm, tm), lambda i: (0, 0))],
        out_specs=[col, pl.BlockSpec((N_EXPERTS, LANES), lambda i: (0, 0))],
        out_shape=[jax.ShapeDtypeStruct((TOP_K, n_tok), I32), jax.ShapeDtypeStruct((N_EXPERTS, LANES), F32)],
        scratch_shapes=[pltpu.VMEM((N_EXPERTS, LANES), F32)],
        compiler_params=_cparams(("arbitrary",)),
        name="route_rank",
    )(idx_t, tri)
    counts = cnt[:, 0].astype(I32)
    padded = ((counts + MOE_TILE - 1) // MOE_TILE) * MOE_TILE
    ends = jnp.cumsum(padded)
    starts = ends - padded
    start_b = jnp.broadcast_to(starts.astype(F32)[:, None], (N_EXPERTS, LANES))
    pos = pl.pallas_call(
        _dest_kernel,
        grid=(n_tok // tm,),
        in_specs=[col, col, pl.BlockSpec((N_EXPERTS, LANES), lambda i: (0, 0))],
        out_specs=col,
        out_shape=jax.ShapeDtypeStruct((TOP_K, n_tok), I32),
        compiler_params=_cparams(("arbitrary",)),
        name="route_dest",
    )(idx_t, rank, start_b)
    n_tiles = TOP_K * n_tok // MOE_TILE + N_EXPERTS
    n_used = (ends[-1] // MOE_TILE).astype(I32)
    tile_start = jnp.minimum(jnp.arange(n_tiles, dtype=I32), n_used - 1) * MOE_TILE
    tile_e = jnp.minimum(jnp.sum((ends[None, :] <= tile_start[:, None]).astype(I32), axis=1), N_EXPERTS - 1)
    first = jnp.concatenate([jnp.ones((1,), I32), (tile_e[1:] != tile_e[:-1]).astype(I32)])
    eidx = jnp.arange(N_EXPERTS, dtype=I32)
    used = counts > 0
    later = jnp.where((eidx[None, :] > eidx[:, None]) & used[None, :], eidx[None, :], N_EXPERTS)
    nxt = jnp.min(later, axis=1)
    next_e = jnp.where(nxt == N_EXPERTS, -1, nxt).astype(I32)
    parity = ((jnp.cumsum(used.astype(I32)) - 1) % 2).astype(I32)
    plan = (tile_e, first, n_used.reshape(1), next_e, parity)
    slot = jnp.arange(MOE_TILE, dtype=I32)[None, :]
    rem = (counts % MOE_TILE)[:, None]
    pad_key = jnp.where((rem != 0) & (slot >= rem),
                        (starts + (counts // MOE_TILE) * MOE_TILE)[:, None] + slot,
                        n_tiles * MOE_TILE + eidx[:, None] * MOE_TILE + slot)
    keys = jnp.concatenate([pos.reshape(-1), pad_key.reshape(-1).astype(I32)])
    toks = jnp.concatenate([jnp.tile(jnp.arange(n_tok, dtype=I32), TOP_K), jnp.zeros((N_EXPERTS * MOE_TILE,), I32)])
    row_tok = lax.sort((keys, toks), num_keys=1)[1].reshape(n_tiles, 1, MOE_TILE)
    return pos, plan, row_tok


def _rows_to_tiles(ref, val):
    n = val.shape[0]
    for s in range(SUBLANES):
        ref[pl.ds(s, n, stride=SUBLANES), :] = val[:, s * LANES:(s + 1) * LANES]


def _tiles_to_rows(ref, n, *lead):
    return jnp.concatenate([ref[(*lead, pl.ds(s, n, stride=SUBLANES), slice(None))] for s in range(SUBLANES)],
                           axis=1)


def _tile_rows(ref, row, n=1):
    start = pl.multiple_of(row * SUBLANES, SUBLANES)
    return ref.at[pl.ds(start, n * SUBLANES), :]


def _expert_kernel(tile_e, first, n_used, next_e, parity, tok_ref, h2_hbm, wg_hbm, wu_hbm, wd_hbm, y_ref,
                   hbuf, conv, stage, wgf, wuf, wdf, wgb, wub, wdb, hsem, sem):
    i = pl.program_id(0)
    n = n_used[0]

    def weight_copies(e, slot):
        return [pltpu.make_async_copy(src.at[e], dst.at[slot], sem.at[slot])
                for src, dst in ((wg_hbm, wgf), (wu_hbm, wuf), (wd_hbm, wdf))]

    @pl.when(i == 0)
    def _():
        for cp in weight_copies(tile_e[0], parity[tile_e[0]]):
            cp.start()
        rows = conv.shape[1]
        n_conv = h2_hbm.shape[0] // rows

        def chunk_copy(c, slot):
            return pltpu.make_async_copy(h2_hbm.at[pl.ds(pl.multiple_of(c * rows, rows), rows), :], conv.at[slot],
                                         hsem.at[slot])

        chunk_copy(0, 0).start()

        def convert(c, carry):
            slot = c % 2
            chunk_copy(c, slot).wait()

            @pl.when(c + 1 < n_conv)
            def _():
                chunk_copy(c + 1, 1 - slot).start()

            hbuf[pl.ds(pl.multiple_of(c * rows, rows), rows), :] = conv[slot].astype(BF16)
            return carry
        lax.fori_loop(0, n_conv, convert, 0)

    @pl.when(i < n)
    def _():
        @pl.when(first[i] == 1)
        def _():
            e = tile_e[i]
            slot = parity[e]
            for cp in weight_copies(e, slot):
                cp.wait()

            @pl.when(next_e[e] >= 0)
            def _():
                for cp in weight_copies(next_e[e], 1 - slot):
                    cp.start()

            wgb[...] = wgf[slot].astype(BF16)
            wub[...] = wuf[slot].astype(BF16)
            wdb[...] = wdf[slot].astype(BF16)

        def gather(jj, carry):
            for u in range(SUBLANES):
                j = jj * SUBLANES + u
                t = tok_ref[0, 0, j]
                pair = hbuf[pl.ds(pl.multiple_of((t >> 1) * BF16_TILE_ROWS, BF16_TILE_ROWS), BF16_TILE_ROWS), :]
                pair = pair.astype(F32)
                stage[pl.ds(pl.multiple_of(j * SUBLANES, SUBLANES), SUBLANES), :] = jnp.where(
                    (t & 1) == 1, pair[SUBLANES:], pair[:SUBLANES])
            return carry
        lax.fori_loop(0, MOE_TILE // SUBLANES, gather, 0)
        xb = _tiles_to_rows(stage, MOE_TILE).astype(BF16)
        hid = (_silu(_dot(xb, wgb[...])) * _dot(xb, wub[...])).astype(BF16)
        _rows_to_tiles(y_ref, _dot(hid, wdb[...]))

    @pl.when(i >= n)
    def _():
        y_ref[...] = jnp.zeros_like(y_ref)


def _experts(h2, row_tok, plan, wg, wu, wd):
    n_tiles = row_tok.shape[0]
    d, de = wg.shape[1], wg.shape[2]
    blk = (MOE_TILE * SUBLANES, LANES)
    conv_rows = min(4096, h2.shape[0])
    assert h2.shape[0] % conv_rows == 0 and h2.shape[0] % BF16_TILE_ROWS == 0
    hbm = pl.BlockSpec(memory_space=pl.ANY)
    grid_spec = pltpu.PrefetchScalarGridSpec(
        num_scalar_prefetch=len(plan),
        grid=(n_tiles,),
        in_specs=[pl.BlockSpec((1, 1, MOE_TILE), lambda i, te, fi, nu, ne, pa: (jnp.minimum(i, nu[0] - 1), 0, 0),
                               memory_space=pltpu.SMEM), hbm, hbm, hbm, hbm],
        out_specs=pl.BlockSpec(blk, lambda i, te, fi, nu, ne, pa: (i, 0)),
        scratch_shapes=[pltpu.VMEM(h2.shape, BF16), pltpu.VMEM((2, conv_rows, LANES), F32), pltpu.VMEM(blk, F32),
                        pltpu.VMEM((2, d, de), F32), pltpu.VMEM((2, d, de), F32), pltpu.VMEM((2, de, d), F32),
                        pltpu.VMEM((d, de), BF16), pltpu.VMEM((d, de), BF16), pltpu.VMEM((de, d), BF16),
                        pltpu.SemaphoreType.DMA((2,)), pltpu.SemaphoreType.DMA((2,))])
    return pl.pallas_call(
        _expert_kernel,
        grid_spec=grid_spec,
        out_shape=jax.ShapeDtypeStruct((n_tiles * MOE_TILE * SUBLANES, LANES), F32),
        compiler_params=pltpu.CompilerParams(dimension_semantics=("arbitrary",),
                                             vmem_limit_bytes=EXPERTS_VMEM_LIMIT),
        name="experts",
    )(*plan, row_tok, h2, wg, wu, wd)


def _combine_kernel(posc_ref, posn_ref, ys_hbm, acc_ref, w_ref, mod_ref, g_ref, o_ref, buf, sem):
    i = pl.program_id(0)
    n = pl.num_programs(0)
    tc = acc_ref.shape[0]

    def issue(pos_ref, slot):
        for k in range(TOP_K):
            def body(jj, carry):
                for u in range(8):
                    j = jj * 8 + u
                    pltpu.make_async_copy(_tile_rows(ys_hbm, pos_ref[0, k, j]), _tile_rows(buf.at[slot, k], j),
                                          sem.at[slot]).start()
                return carry
            lax.fori_loop(0, tc // 8, body, 0)

    @pl.when(i == 0)
    def _():
        issue(posc_ref, 0)

    @pl.when(i + 1 < n)
    def _():
        issue(posn_ref, (i + 1) % 2)

    slot = i % 2
    for k in range(TOP_K):
        pltpu.make_async_copy(_tile_rows(ys_hbm, 0, tc), buf.at[slot, k], sem.at[slot]).wait()
    w = w_ref[...]
    routed = w[:, 0:1] * _tiles_to_rows(buf, tc, slot, 0)
    for k in range(1, TOP_K):
        routed = routed + w[:, k:k + 1] * _tiles_to_rows(buf, tc, slot, k)
    x2 = acc_ref[...] + mod_ref[0, 5:6, :] * routed
    o_ref[...] = x2 * lax.rsqrt(jnp.mean(x2 * x2, axis=-1, keepdims=True) + RMS_EPS) * g_ref[...]


def _tile_pos(pos_t, tc):
    return pos_t.reshape(TOP_K, pos_t.shape[1] // tc, tc).transpose(1, 0, 2)


def _combine(ys, pos3, acc0, w_tok, mod3, g_final, seq):
    t, d = acc0.shape
    tc = pos3.shape[2]
    nt = t // tc
    return pl.pallas_call(
        _combine_kernel,
        grid=(nt,),
        in_specs=[pl.BlockSpec((1, TOP_K, tc), lambda i: (i, 0, 0), memory_space=pltpu.SMEM),
                  pl.BlockSpec((1, TOP_K, tc), lambda i: (jnp.minimum(i + 1, nt - 1), 0, 0),
                               memory_space=pltpu.SMEM),
                  pl.BlockSpec(memory_space=pl.ANY),
                  pl.BlockSpec((tc, d), lambda i: (i, 0)),
                  pl.BlockSpec((tc, TOP_K), lambda i: (i, 0)),
                  pl.BlockSpec((1, 6, d), lambda i: ((i * tc) // seq, 0, 0)),
                  pl.BlockSpec((1, d), lambda i: (0, 0))],
        out_specs=pl.BlockSpec((tc, d), lambda i: (i, 0)),
        out_shape=jax.ShapeDtypeStruct((t, d), F32),
        scratch_shapes=[pltpu.VMEM((2, TOP_K, tc * SUBLANES, LANES), F32), pltpu.SemaphoreType.DMA((2,))],
        compiler_params=_cparams(("arbitrary",)),
        name="combine",
    )(pos3, pos3, ys, acc0, w_tok, mod3, g_final)


def kernel(x, c, positions, w_ada, b_ada, g_norm_mix, w_in, cmp_pos_k, cmp_w1_k, cmp_w2_k, cmp_pos_v,
           cmp_w1_v, cmp_w2_v, g_nsa_out, g_ret_out, w_out, g_norm_ffn, w_router, router_bias,
           w_gate_e, w_up_e, w_down_e, w_gate_s, w_up_s, w_down_s, g_norm_final):
    bsz, seq, d = x.shape
    assert d == SUBLANES * LANES, "MoE rows are moved as one (8, 128) tile each"
    t = bsz * seq
    x2 = x.reshape(t, d)
    cos_t, sin_t = _rope_tables(positions.reshape(t, 1).astype(F32))
    n_cmp_pad = seq // CMP_STRIDE

    def cmp_rows(tab):
        rows = tab[:, :LANES].reshape(bsz, seq, LANES)[:, CMP_BLOCK - 1::CMP_STRIDE]
        return jnp.pad(rows, ((0, 0), (0, n_cmp_pad - rows.shape[1]), (0, 0)))

    cos_c, sin_c = cmp_rows(cos_t), cmp_rows(sin_t)
    for l in range(w_in.shape[0]):
        mod3 = _ada(c, w_ada[l], b_ada[l]).reshape(bsz, 6, d)
        q, kc, vc, kk, gates, qr, kr, vr, gr, vsel_t, vwin_t = _inproj(
            x2, mod3, g_norm_mix[l].reshape(1, d), _arrange_w_in(w_in[l]), cos_t, sin_t, seq)
        kcmp, vcmp = _compress(kc, vc, cos_c, sin_c,
                               _arrange_cmp_weights(cmp_pos_k[l], cmp_w1_k[l], cmp_w2_k[l]),
                               _arrange_cmp_weights(cmp_pos_v[l], cmp_w1_v[l], cmp_w2_v[l]), bsz, seq)
        onsa = _nsa(q, kcmp, vcmp, kk, vsel_t, vwin_t, gates, g_nsa_out[l].reshape(1, -1), bsz, seq)
        oret = _retention(qr, kr, vr, gr, g_ret_out[l], bsz, seq)
        acc0, h2, idx_t, w_t = _post(x2, onsa, oret, mod3, w_out[l], g_norm_ffn[l].reshape(1, d),
                                     w_router[l], router_bias[l], w_gate_s[l], w_up_s[l], w_down_s[l], seq)
        pos_t, plan, row_tok = _route_plan(idx_t, t)
        ys = _experts(h2, row_tok, plan, w_gate_e[l], w_up_e[l], w_down_e[l])
        last = l == w_in.shape[0] - 1
        gfin = g_norm_final.reshape(1, d)
        x2 = _combine(ys, _tile_pos(pos_t, 128), acc0, w_t.T, mod3, gfin, seq)
        assert last, "final norm is fused into the combine stage; depth 1 only"
    return x2.reshape(bsz, seq, d)
```

```python
import functools
import math

import numpy as np
import jax
import jax.numpy as jnp
from jax import lax
from jax.experimental import pallas as pl
from jax.experimental.pallas import tpu as pltpu

F32 = jnp.float32
BF16 = jnp.bfloat16
I32 = jnp.int32
HIGHEST = lax.Precision.HIGHEST

LANES = 128
SUBLANES = 8
BF16_TILE_ROWS = 16
NSA_HEAD_DIM = 64
NSA_HEADS = 8
NSA_GROUPS = 2
NSA_Q_PER_KV = NSA_HEADS // NSA_GROUPS
CMP_BLOCK = 32
CMP_STRIDE = 16
CMP_HIDDEN = 128
SEL_BLOCK = 64
SEL_TOP = 8
WINDOW = 512
Q_BLOCK = 128
FORCE_BONUS = 1.0e4
RET_HEADS = 4
RET_HEAD_DIM = 128
RET_CHUNK = 128
ROPE_THETA = 10000.0
RMS_EPS = 1e-6
N_EXPERTS = 256
N_EXPERT_GROUPS = 8
TOPK_GROUPS = 4
TOP_K = 8
ROUTED_SCALE = 2.5
MOE_TILE = 256
SEL_CHUNK = 256
MASK_BIG = 2.0 ** 100
MASK_FLOOR = 2.0 ** 99
NEG = -1e30
VMEM_LIMIT = 48 * 1024 * 1024
EXPERTS_VMEM_LIMIT = 56 * 1024 * 1024


def _cparams(sem):
    return pltpu.CompilerParams(dimension_semantics=sem, vmem_limit_bytes=VMEM_LIMIT)


def _dot(a, b, **kw):
    return jnp.dot(a, b, preferred_element_type=F32, **kw)


def _dot_nt(a, b, **kw):
    return lax.dot_general(a, b, (((1,), (1,)), ((), ())), preferred_element_type=F32, **kw)


def _sigmoid(v):
    return 1.0 / (1.0 + jnp.exp(-v))


def _silu(v):
    return v * _sigmoid(v)


def _ada_kernel(c_ref, w_ref, b_ref, o_ref):
    o_ref[...] = _dot(_silu(c_ref[...]), w_ref[...], precision=HIGHEST) + b_ref[...]


def _ada(c, w, b):
    bsz, d = c.shape
    n = w.shape[1]
    tn = 1536
    cp = jnp.zeros((8, d), F32).at[:bsz].set(c)
    out = pl.pallas_call(
        _ada_kernel,
        grid=(n // tn,),
        in_specs=[pl.BlockSpec((8, d), lambda j: (0, 0)),
                  pl.BlockSpec((d, tn), lambda j: (0, j)),
                  pl.BlockSpec((1, tn), lambda j: (0, j))],
        out_specs=pl.BlockSpec((8, tn), lambda j: (0, j)),
        out_shape=jax.ShapeDtypeStruct((8, n), F32),
        compiler_params=_cparams(("arbitrary",)),
        name="ada",
    )(cp, w, b.reshape(1, n))
    return out[:bsz]


def _rope_table_kernel(pos_ref, inv_ref, sgn_ref, cos_ref, sin_ref):
    ang = pos_ref[...] * inv_ref[...]
    cos_ref[...] = jnp.cos(ang)
    sin_ref[...] = jnp.sin(ang) * sgn_ref[...]


def _rope_tables(posf):
    t = posf.shape[0]
    inv_n = ROPE_THETA ** (-jnp.arange(0, NSA_HEAD_DIM, 2, dtype=F32) / NSA_HEAD_DIM)
    inv_r = ROPE_THETA ** (-jnp.arange(0, RET_HEAD_DIM, 2, dtype=F32) / RET_HEAD_DIM)
    inv = jnp.concatenate([jnp.tile(inv_n, 4), jnp.tile(inv_r, 2)]).reshape(1, 2 * LANES)
    sgn_n = np.where((np.arange(LANES) % NSA_HEAD_DIM) < NSA_HEAD_DIM // 2, -1.0, 1.0)
    sgn_r = np.where(np.arange(LANES) < RET_HEAD_DIM // 2, -1.0, 1.0)
    sgn = jnp.asarray(np.concatenate([sgn_n, sgn_r]).reshape(1, 2 * LANES), F32)
    tm = min(t, 1024)
    return pl.pallas_call(
        _rope_table_kernel,
        grid=(t // tm,),
        in_specs=[pl.BlockSpec((tm, 1), lambda i: (i, 0)),
                  pl.BlockSpec((1, 2 * LANES), lambda i: (0, 0)),
                  pl.BlockSpec((1, 2 * LANES), lambda i: (0, 0))],
        out_specs=[pl.BlockSpec((tm, 2 * LANES), lambda i: (i, 0))] * 2,
        out_shape=[jax.ShapeDtypeStruct((t, 2 * LANES), F32)] * 2,
        compiler_params=_cparams(("arbitrary",)),
        name="rope_tables",
    )(posf, inv, sgn)


def _rope64(p, cos, sin, first_half):
    rot = jnp.where(first_half, pltpu.roll(p, 96, 1), pltpu.roll(p, 32, 1))
    return p * cos + rot * sin


def _rope128(p, cos, sin):
    return p * cos + pltpu.roll(p, 64, 1) * sin


_C_Q = 0
_C_KV = _C_Q + NSA_HEADS * LANES
_C_GATE = _C_KV + 6 * LANES
_C_RET = _C_GATE + LANES
_C_END = _C_RET + 4 * RET_HEADS * RET_HEAD_DIM


def _arrange_w_in(w_in):
    d = w_in.shape[0]
    nw = NSA_HEADS * NSA_HEAD_DIM
    q = w_in[:, :nw].reshape(d, NSA_HEADS, NSA_HEAD_DIM)
    z = jnp.zeros_like(q)
    grp = (jnp.arange(NSA_HEADS) // NSA_Q_PER_KV)[None, :, None]
    qpad = jnp.where(grp == 0, jnp.concatenate([q, z], -1), jnp.concatenate([z, q], -1))
    qpad = qpad.reshape(d, NSA_HEADS * LANES)
    kv = w_in[:, nw:nw + 6 * LANES]
    g0 = nw + 6 * LANES
    gate = jnp.pad(w_in[:, g0:g0 + 3 * NSA_HEADS], ((0, 0), (0, LANES - 3 * NSA_HEADS)))
    ret = w_in[:, g0 + 3 * NSA_HEADS:]
    return jnp.concatenate([qpad, kv, gate, ret], axis=1).astype(BF16)


def _inproj_kernel(x_ref, mod_ref, g_ref, w_ref, cos_ref, sin_ref,
                   q_ref, kc_ref, vc_ref, kk_ref, gate_ref, qr_ref, kr_ref, vr_ref, gr_ref, vst_ref, vwt_ref):
    x = x_ref[...]
    tm = x.shape[0]
    ms = jnp.mean(x * x, axis=-1, keepdims=True)
    y = x * lax.rsqrt(ms + RMS_EPS) * g_ref[...]
    h = y * (1.0 + mod_ref[0, 1:2, :]) + mod_ref[0, 0:1, :]
    hb = h.astype(BF16)
    cos_n, sin_n = cos_ref[:, 0:LANES], sin_ref[:, 0:LANES]
    cos_r, sin_r = cos_ref[:, LANES:], sin_ref[:, LANES:]
    lane = lax.broadcasted_iota(I32, (tm, LANES), 1)
    first_half = (lane % NSA_HEAD_DIM) < (NSA_HEAD_DIM // 2)
    scale_n = NSA_HEAD_DIM ** -0.5
    scale_r = RET_HEAD_DIM ** -0.5

    def proj(c0, n):
        return _dot(hb, w_ref[:, c0:c0 + n])

    for hh in range(NSA_HEADS):
        p = proj(_C_Q + hh * LANES, LANES)
        q_ref[:, hh * LANES:(hh + 1) * LANES] = (_rope64(p, cos_n, sin_n, first_half) * scale_n).astype(BF16)
    kv = proj(_C_KV, 6 * LANES)
    kc_ref[...] = kv[:, 0:LANES].astype(BF16)
    vc_ref[...] = kv[:, LANES:2 * LANES].astype(BF16)
    kk_ref[:, 0:LANES] = _rope64(kv[:, 2 * LANES:3 * LANES], cos_n, sin_n, first_half).astype(BF16)
    kk_ref[:, LANES:2 * LANES] = _rope64(kv[:, 4 * LANES:5 * LANES], cos_n, sin_n, first_half).astype(BF16)
    group0 = lane < NSA_HEAD_DIM
    for c0, vt_ref, chunk in ((3 * LANES, vst_ref, SEL_CHUNK), (5 * LANES, vwt_ref, Q_BLOCK)):
        v = kv[:, c0:c0 + LANES]
        for g, vg in enumerate((jnp.where(group0, v, 1.0), jnp.where(group0, 1.0, v))):
            for cc in range(tm // chunk):
                vt_ref[cc, g] = vg[cc * chunk:(cc + 1) * chunk].T.astype(BF16)
    gate_ref[...] = _sigmoid(proj(_C_GATE, LANES))
    rw = RET_HEADS * RET_HEAD_DIM
    for hh in range(RET_HEADS):
        sl = slice(hh * LANES, (hh + 1) * LANES)
        pq = proj(_C_RET + hh * LANES, LANES)
        qr_ref[:, sl] = _rope128(pq, cos_r, sin_r).astype(BF16)
        pk = proj(_C_RET + rw + hh * LANES, LANES)
        kr_ref[:, sl] = (_rope128(pk, cos_r, sin_r) * scale_r).astype(BF16)
    vr_ref[...] = proj(_C_RET + 2 * rw, rw).astype(BF16)
    gr_ref[...] = proj(_C_RET + 3 * rw, rw)


def _inproj(x2, mod3, g_mix, w_in_p, cos_t, sin_t, seq):
    t, d = x2.shape
    tm = min(512, seq)
    rw = RET_HEADS * RET_HEAD_DIM
    row = lambda n: pl.BlockSpec((tm, n), lambda i: (i, 0))
    outs = [(NSA_HEADS * LANES, BF16), (LANES, BF16), (LANES, BF16), (2 * LANES, BF16), (LANES, F32),
            (rw, BF16), (rw, BF16), (rw, BF16), (rw, F32)]
    vt_specs, vt_shapes = [], []
    for chunk in (SEL_CHUNK, Q_BLOCK):
        vt_specs.append(pl.BlockSpec((tm // chunk, NSA_GROUPS, LANES, chunk), lambda i: (i, 0, 0, 0)))
        vt_shapes.append(jax.ShapeDtypeStruct((t // chunk, NSA_GROUPS, LANES, chunk), BF16))
    return pl.pallas_call(
        _inproj_kernel,
        grid=(t // tm,),
        in_specs=[row(d),
                  pl.BlockSpec((1, 6, d), lambda i: ((i * tm) // seq, 0, 0)),
                  pl.BlockSpec((1, d), lambda i: (0, 0)),
                  pl.BlockSpec((d, _C_END), lambda i: (0, 0)),
                  row(2 * LANES), row(2 * LANES)],
        out_specs=[row(n) for n, _ in outs] + vt_specs,
        out_shape=[jax.ShapeDtypeStruct((t, n), dt) for n, dt in outs] + vt_shapes,
        compiler_params=_cparams(("arbitrary",)),
        name="inproj",
    )(x2, mod3, g_mix, w_in_p, cos_t, sin_t)


def _gelu_tanh(v):
    return v * (0.5 * (1.0 + jnp.tanh(math.sqrt(2.0 / math.pi) * (v + 0.044715 * (v * v * v)))))


def _compress_kernel(kc_ref, vc_ref, wkt_ref, wkb_ref, wk2_ref, pk_ref, wvt_ref, wvb_ref, wv2_ref, pv_ref,
                     cos_ref, sin_ref, ko_ref, vo_ref):
    def one(x_ref, wt_ref, wb_ref, w2_ref, p_ref):
        xx = x_ref[0]
        a = _dot(xx, wt_ref[...])
        b = _dot(xx, wb_ref[...])
        pb = p_ref[...].astype(BF16)
        bias = _dot(pb, wt_ref[...])[0:1] + _dot(pb, wb_ref[...])[1:2]
        hid = a + pltpu.roll(b, b.shape[0] - 1, 0) + bias
        return _dot(_gelu_tanh(hid).astype(BF16), w2_ref[...])

    k = one(kc_ref, wkt_ref, wkb_ref, wk2_ref, pk_ref)
    lane = lax.broadcasted_iota(I32, k.shape, 1)
    first_half = (lane % NSA_HEAD_DIM) < (NSA_HEAD_DIM // 2)
    ko_ref[0] = _rope64(k, cos_ref[0], sin_ref[0], first_half).astype(BF16)
    v = one(vc_ref, wvt_ref, wvb_ref, wv2_ref, pv_ref)
    group0 = lane < NSA_HEAD_DIM
    vo_ref[0, 0] = jnp.where(group0, v, 1.0).T.astype(BF16)
    vo_ref[0, 1] = jnp.where(group0, 1.0, v).T.astype(BF16)


def _arrange_cmp_weights(pos, w1, w2):
    half = CMP_BLOCK // 2
    dh = NSA_HEAD_DIM
    w1r = w1.reshape(CMP_BLOCK, dh, CMP_HIDDEN)

    def block(wpart):
        z = jnp.zeros_like(wpart)
        g0 = jnp.concatenate([wpart, z], axis=-1)
        g1 = jnp.concatenate([z, wpart], axis=-1)
        return jnp.stack([g0, g1], axis=1).reshape(half * 2 * dh, 2 * CMP_HIDDEN)

    wt, wb = block(w1r[:half]), block(w1r[half:])
    z2 = jnp.zeros_like(w2)
    w2b = jnp.concatenate([jnp.concatenate([w2, z2], 1), jnp.concatenate([z2, w2], 1)], 0)
    ptop = jnp.tile(pos[:half], (1, 2)).reshape(1, -1)
    pbot = jnp.tile(pos[half:], (1, 2)).reshape(1, -1)
    prow = jnp.concatenate([ptop, pbot, jnp.zeros((6, ptop.shape[1]), F32)], 0)
    return wt.astype(BF16), wb.astype(BF16), w2b.astype(BF16), prow


def _compress(kc, vc, cos_c, sin_c, kparams, vparams, bsz, seq):
    nchunk = seq // CMP_STRIDE
    width = CMP_STRIDE * LANES
    kc16 = kc.reshape(bsz, nchunk, width)
    vc16 = vc.reshape(bsz, nchunk, width)
    full = lambda a: pl.BlockSpec(a.shape, lambda b: (0,) * a.ndim)
    per_b = lambda n: pl.BlockSpec((1, nchunk, n), lambda b: (b, 0, 0))
    return pl.pallas_call(
        _compress_kernel,
        grid=(bsz,),
        in_specs=[per_b(width), per_b(width)] + [full(a) for a in kparams] + [full(a) for a in vparams]
                 + [per_b(LANES), per_b(LANES)],
        out_specs=[per_b(LANES), pl.BlockSpec((1, NSA_GROUPS, LANES, nchunk), lambda b: (b, 0, 0, 0))],
        out_shape=[jax.ShapeDtypeStruct((bsz, nchunk, LANES), BF16),
                   jax.ShapeDtypeStruct((bsz, NSA_GROUPS, LANES, nchunk), BF16)],
        compiler_params=_cparams(("arbitrary",)),
        name="compress",
    )(kc16, vc16, *kparams, *vparams, cos_c, sin_c)


def _softmax_chunk(m, acc, s, vt):
    m_new = jnp.maximum(m, jnp.max(s, axis=0, keepdims=True))
    e = jnp.exp(s - m_new).astype(BF16)
    acc = jnp.exp(m - m_new) * acc + _dot(vt, e)
    return m_new, acc


def _nsa_kernel(q_ref, kc_ref, vct_ref, ks_ref, kw_ref, vst_ref, vwt_ref, gate_ref,
                ovt_ref, exp_ref, g_ref, o_ref, *, seq):
    qb = pl.program_id(1)
    t0 = qb * Q_BLOCK
    cols = NSA_Q_PER_KV * Q_BLOCK
    n_cmp_pad = kc_ref.shape[1]
    n_sel = seq // SEL_BLOCK
    t_row = t0 + lax.broadcasted_iota(I32, (1, Q_BLOCK), 1)
    gates_t = gate_ref[...].T
    lane = lax.broadcasted_iota(I32, (Q_BLOCK, LANES), 1)
    key = lax.broadcasted_iota(I32, (Q_BLOCK, Q_BLOCK), 0)
    tok = lax.broadcasted_iota(I32, (Q_BLOCK, Q_BLOCK), 1)
    cend = lax.broadcasted_iota(I32, (n_cmp_pad, 1), 0) * CMP_STRIDE + (CMP_BLOCK - 1)
    bias_cmp = jnp.where(cend <= t_row, 0.0, -MASK_BIG)
    last_c = qb // (SEL_CHUNK // Q_BLOCK)
    kpos_last = last_c * SEL_CHUNK + lax.broadcasted_iota(I32, (SEL_CHUNK, 1), 0)
    bias_diag = jnp.where(kpos_last <= t_row, 0.0, -MASK_BIG)
    n_win = WINDOW // Q_BLOCK
    bias_win_first = jnp.where(key > tok, 0.0, -MASK_BIG)
    bias_win_last = jnp.where(key <= tok, 0.0, -MASK_BIG)
    init = (jnp.full((1, cols), -MASK_FLOOR, F32), jnp.zeros((LANES, cols), F32))
    tile_heads = lambda b: jnp.concatenate([b] * NSA_Q_PER_KV, axis=1)

    def finish(acc, g):
        ones_row = NSA_HEAD_DIM * (1 - g)
        return acc * (1.0 / jnp.maximum(acc[ones_row:ones_row + 1, :], 1e-20))

    q4s, o_cs, sel_bs = [], [], []
    for g in range(NSA_GROUPS):
        q4 = jnp.concatenate([q_ref[:, (NSA_Q_PER_KV * g + r) * LANES:(NSA_Q_PER_KV * g + r + 1) * LANES]
                              for r in range(NSA_Q_PER_KV)], axis=0)
        q4s.append(q4)
        s_c = _dot_nt(kc_ref[0], q4) + tile_heads(bias_cmp)
        e_c = jnp.exp(s_c - jnp.maximum(jnp.max(s_c, axis=0, keepdims=True), -MASK_FLOOR))
        p_c = e_c * (1.0 / jnp.maximum(jnp.sum(e_c, axis=0, keepdims=True), 1e-20))
        psum = p_c[:, 0:Q_BLOCK]
        for r in range(1, NSA_Q_PER_KV):
            psum = psum + p_c[:, r * Q_BLOCK:(r + 1) * Q_BLOCK]
        o_cs.append(finish(_dot(vct_ref[0, g], e_c.astype(BF16)), g))
        imp_t = _dot(ovt_ref[...], psum, precision=HIGHEST)
        nsp = imp_t.shape[0]
        jrow = lax.broadcasted_iota(I32, (nsp, 1), 0)
        cur = t_row // SEL_BLOCK
        valid = (jrow * SEL_BLOCK <= t_row) & (jrow < n_sel)
        forced = (jrow == 0) | (jrow == cur) | (jrow == cur - 1)
        val = jnp.where(valid, imp_t + jnp.where(forced, FORCE_BONUS, 0.0), -1.0)
        val = jnp.where(jrow < n_sel, val, -jnp.inf)
        sel_t = jnp.zeros((nsp, Q_BLOCK), F32)
        for _ in range(min(SEL_TOP, n_sel)):
            mx = jnp.max(val, axis=0, keepdims=True)
            jmin = jnp.min(jnp.where(val == mx, jrow, nsp), axis=0, keepdims=True)
            hit = jrow == jmin
            sel_t = jnp.where(hit, 1.0, sel_t)
            val = jnp.where(hit, -jnp.inf, val)
        sel_bs.append(jnp.concatenate([sel_t[:n_sel], jnp.ones((SUBLANES, Q_BLOCK), F32),
                                       jnp.zeros((LANES - n_sel - SUBLANES, Q_BLOCK), F32)], axis=0).astype(BF16))

    def sel_scores(c):
        keys = ks_ref[pl.ds(pl.multiple_of(c * SEL_CHUNK, SEL_CHUNK), SEL_CHUNK), :]
        expand = exp_ref[c]
        return tuple(_dot_nt(keys, q4s[g]) + tile_heads(_dot(expand, sel_bs[g])) for g in range(NSA_GROUPS))

    def sel_reduce(c, state, scores):
        out = []
        for g in range(NSA_GROUPS):
            out.extend(_softmax_chunk(state[2 * g], state[2 * g + 1], scores[g], vst_ref[c, g]))
        return tuple(out)

    def sel_step(c, carry):
        state, scores = carry
        return sel_reduce(c, state, scores), sel_scores(c + 1)

    state, scores = lax.fori_loop(0, last_c, sel_step, (init * NSA_GROUPS, sel_scores(0)))
    diag = tile_heads(bias_diag)
    state = sel_reduce(last_c, state, tuple(s + diag for s in scores))
    o_ss = [finish(state[2 * g + 1], g) for g in range(NSA_GROUPS)]
    s_ws = [[] for _ in range(NSA_GROUPS)]
    kbs = []
    for w in range(n_win + 1):
        wb = qb - n_win + w
        kb = jnp.maximum(wb, 0)
        kbs.append(kb)
        keys = kw_ref[pl.ds(pl.multiple_of(kb * Q_BLOCK, Q_BLOCK), Q_BLOCK), :]
        if w == n_win:
            bias = bias_win_last
        else:
            before_start = jnp.where(wb >= 0, 0.0, -MASK_BIG)
            bias = (bias_win_first + before_start) if w == 0 else jnp.full((Q_BLOCK, Q_BLOCK), before_start)
        for g in range(NSA_GROUPS):
            s_ws[g].append(_dot_nt(keys, q4s[g]) + tile_heads(bias))
    o_ws = []
    for g in range(NSA_GROUPS):
        m_w = jnp.full((1, cols), -MASK_FLOOR, F32)
        for s in s_ws[g]:
            m_w = jnp.maximum(m_w, jnp.max(s, axis=0, keepdims=True))
        acc = None
        for kb, s in zip(kbs, s_ws[g]):
            pv = _dot(vwt_ref[kb, g], jnp.exp(s - m_w).astype(BF16))
            acc = pv if acc is None else acc + pv
        o_ws.append(finish(acc, g))
    heads_out = []
    for g in range(NSA_GROUPS):
        o_c, o_s, o_w = o_cs[g], o_ss[g], o_ws[g]
        for r in range(NSA_Q_PER_KV):
            hh = NSA_Q_PER_KV * g + r
            cs = slice(r * Q_BLOCK, (r + 1) * Q_BLOCK)
            o = (gates_t[3 * hh:3 * hh + 1, :] * o_c[:, cs] + gates_t[3 * hh + 1:3 * hh + 2, :] * o_s[:, cs]
                 + gates_t[3 * hh + 2:3 * hh + 3, :] * o_w[:, cs]).T
            if (hh % 2) != g:
                o = pltpu.roll(o, NSA_HEAD_DIM, 1)
            heads_out.append(o)
    blocks = [jnp.where(lane < NSA_HEAD_DIM, heads_out[2 * i], heads_out[2 * i + 1])
              for i in range(NSA_HEADS // 2)]
    ss = sum(jnp.sum(b * b, axis=-1, keepdims=True) for b in blocks)
    inv = lax.rsqrt(ss / (NSA_HEADS * NSA_HEAD_DIM) + RMS_EPS)
    for i, b in enumerate(blocks):
        sl = slice(i * LANES, (i + 1) * LANES)
        o_ref[:, sl] = (b * inv * g_ref[:, sl]).astype(BF16)


def _nsa(q, kcmp, vcmp_t, kk, vsel_t, vwin_t, gates, g_nsa, bsz, seq):
    t = bsz * seq
    nq = seq // Q_BLOCK
    n_cmp_pad = seq // CMP_STRIDE
    n_cmp = (seq - CMP_BLOCK) // CMP_STRIDE + 1
    n_sel = seq // SEL_BLOCK
    assert n_sel % SUBLANES == 0 and n_sel + SUBLANES <= LANES, "selection mask needs a spare expansion row"
    nsp = n_sel
    cs = np.arange(n_cmp_pad) * CMP_STRIDE
    ss = np.arange(nsp) * SEL_BLOCK
    ov = ((cs[None, :] < ss[:, None] + SEL_BLOCK) & (cs[None, :] + CMP_BLOCK > ss[:, None])
          & (np.arange(n_cmp_pad)[None, :] < n_cmp) & (np.arange(nsp)[:, None] < n_sel))
    ovt = jnp.asarray(ov, F32)
    nch = seq // SEL_CHUNK
    kp = np.arange(seq).reshape(nch, SEL_CHUNK, 1)
    col = np.arange(LANES).reshape(1, 1, LANES)
    expand = jnp.asarray(np.where(kp // SEL_BLOCK == col, MASK_BIG, 0.0) + np.where(col == n_sel, -MASK_BIG, 0.0),
                         BF16)
    seqcol = lambda c: pl.BlockSpec((seq, LANES), lambda b, i: (b, c))
    per_b = lambda a: pl.BlockSpec((a.shape[0] // bsz,) + a.shape[1:], lambda b, i: (b,) + (0,) * (a.ndim - 1))
    return pl.pallas_call(
        functools.partial(_nsa_kernel, seq=seq),
        grid=(bsz, nq),
        in_specs=[pl.BlockSpec((Q_BLOCK, NSA_HEADS * LANES), lambda b, i: (b * nq + i, 0)),
                  per_b(kcmp), per_b(vcmp_t), seqcol(0), seqcol(1), per_b(vsel_t), per_b(vwin_t),
                  pl.BlockSpec((Q_BLOCK, LANES), lambda b, i: (b * nq + i, 0)),
                  pl.BlockSpec(ovt.shape, lambda b, i: (0, 0)),
                  pl.BlockSpec(expand.shape, lambda b, i: (0, 0, 0)),
                  pl.BlockSpec((1, NSA_HEADS * NSA_HEAD_DIM), lambda b, i: (0, 0))],
        out_specs=pl.BlockSpec((Q_BLOCK, NSA_HEADS * NSA_HEAD_DIM), lambda b, i: (b * nq + i, 0)),
        out_shape=jax.ShapeDtypeStruct((t, NSA_HEADS * NSA_HEAD_DIM), BF16),
        compiler_params=_cparams(("arbitrary", "arbitrary")),
        name="nsa",
    )(q, kcmp, vcmp_t, kk, kk, vsel_t, vwin_t, gates, ovt, expand, g_nsa)


def _ret_kernel(q_ref, k_ref, v_ref, gr_ref, dec_ref, xi_ref, zeta_ref, gch_ref, g_ref, o_ref, st_ref):
    @pl.when(pl.program_id(1) == 0)
    def _():
        st_ref[...] = jnp.zeros_like(st_ref)

    for hh in range(RET_HEADS):
        sl = slice(hh * LANES, (hh + 1) * LANES)
        q, k, v = q_ref[:, sl], k_ref[:, sl], v_ref[:, sl]
        sc = _dot_nt(q, k) * dec_ref[hh]
        inner = _dot(sc.astype(BF16), v)
        st = st_ref[hh]
        cross = _dot((q.astype(F32) * xi_ref[hh]).astype(BF16), st.astype(BF16))
        kz = (k.astype(F32) * zeta_ref[hh]).T.astype(BF16)
        st_ref[hh] = st * gch_ref[hh] + _dot(kz, v)
        o = inner + cross
        y = o * lax.rsqrt(jnp.mean(o * o, axis=-1, keepdims=True) + RMS_EPS) * g_ref[hh:hh + 1, :]
        o_ref[:, sl] = (_silu(gr_ref[:, sl]) * y).astype(BF16)


def _retention(qr, kr, vr, gr, g_ret, bsz, seq):
    t = bsz * seq
    c = RET_CHUNK
    n = seq // c
    log_g = jnp.log(1.0 - 2.0 ** (-5.0 - jnp.arange(RET_HEADS, dtype=F32)))
    i = jnp.arange(c, dtype=F32)
    diff = i[:, None] - i[None, :]
    causal = diff >= 0
    dec = jnp.where(causal, jnp.exp(log_g[:, None, None] * jnp.where(causal, diff, 0.0)), 0.0)
    xi = jnp.broadcast_to(jnp.exp(log_g[:, None] * (i + 1.0))[:, :, None], (RET_HEADS, c, LANES))
    zeta = jnp.broadcast_to(jnp.exp(log_g[:, None] * (c - 1.0 - i))[:, :, None], (RET_HEADS, c, LANES))
    gch = jnp.broadcast_to(jnp.exp(log_g * c)[:, None, None], (RET_HEADS, 1, LANES))
    w = RET_HEADS * RET_HEAD_DIM
    row = pl.BlockSpec((c, w), lambda b, j: (b * n + j, 0))
    full = lambda a: pl.BlockSpec(a.shape, lambda b, j: (0,) * a.ndim)
    return pl.pallas_call(
        _ret_kernel,
        grid=(bsz, n),
        in_specs=[row, row, row, row, full(dec), full(xi), full(zeta), full(gch), full(g_ret)],
        out_specs=row,
        out_shape=jax.ShapeDtypeStruct((t, w), BF16),
        scratch_shapes=[pltpu.VMEM((RET_HEADS, RET_HEAD_DIM, RET_HEAD_DIM), F32)],
        compiler_params=_cparams(("arbitrary", "arbitrary")),
        name="retention",
    )(qr, kr, vr, gr, dec, xi, zeta, gch, g_ret)


def _post_kernel(x_ref, onsa_ref, oret_ref, mod_ref, wo1_ref, wo2_ref, gffn_ref, wrt_ref, rb_ref,
                 wgs_ref, wus_ref, wds_ref, acc_ref, h2_ref, idx_ref, wt_ref):
    mix = _dot(onsa_ref[...], wo1_ref[...]) + _dot(oret_ref[...], wo2_ref[...])
    x1 = x_ref[...] + mod_ref[0, 2:3, :] * mix
    ms = jnp.mean(x1 * x1, axis=-1, keepdims=True)
    h2 = x1 * lax.rsqrt(ms + RMS_EPS) * gffn_ref[...] * (1.0 + mod_ref[0, 4:5, :]) + mod_ref[0, 3:4, :]
    _rows_to_tiles(h2_ref, h2)
    hb = h2.astype(BF16)
    hid = (_silu(_dot(hb, wgs_ref[...])) * _dot(hb, wus_ref[...])).astype(BF16)
    acc_ref[...] = x1 + mod_ref[0, 5:6, :] * _dot(hid, wds_ref[...])
    s = _sigmoid(_dot_nt(wrt_ref[...], h2, precision=HIGHEST))
    sb = s + rb_ref[...]
    per = N_EXPERTS // N_EXPERT_GROUPS
    ridx = lax.broadcasted_iota(I32, (per, 1), 0)
    blks, grp = [], []
    for gi in range(N_EXPERT_GROUPS):
        blk = sb[gi * per:(gi + 1) * per]
        m1 = jnp.max(blk, axis=0, keepdims=True)
        first = jnp.min(jnp.where(blk == m1, ridx, per), axis=0, keepdims=True)
        m2 = jnp.max(jnp.where(ridx == first, -jnp.inf, blk), axis=0, keepdims=True)
        blks.append(blk)
        grp.append(m1 + m2)
    masked = []
    for gi in range(N_EXPERT_GROUPS):
        rank = jnp.zeros_like(grp[gi])
        for gj in range(N_EXPERT_GROUPS):
            if gj < gi:
                rank = rank + (grp[gj] >= grp[gi]).astype(F32)
            elif gj > gi:
                rank = rank + (grp[gj] > grp[gi]).astype(F32)
        masked.append(jnp.where(rank < TOPK_GROUPS, blks[gi], NEG))
    val = jnp.concatenate(masked, axis=0)
    eidx = lax.broadcasted_iota(I32, (N_EXPERTS, 1), 0)
    ids, ws = [], []
    for _ in range(TOP_K):
        mx = jnp.max(val, axis=0, keepdims=True)
        emin = jnp.min(jnp.where(val == mx, eidx, N_EXPERTS), axis=0, keepdims=True)
        hit = eidx == emin
        ids.append(emin)
        ws.append(jnp.sum(jnp.where(hit, s, 0.0), axis=0, keepdims=True))
        val = jnp.where(hit, -jnp.inf, val)
    wsum = ws[0]
    for wk in ws[1:]:
        wsum = wsum + wk
    idx_ref[...] = jnp.concatenate(ids, axis=0)
    wt_ref[...] = jnp.concatenate(ws, axis=0) / wsum * ROUTED_SCALE


def _post(x2, onsa, oret, mod3, w_out, g_ffn, w_router, router_bias, wgs, wus, wds, seq):
    t, d = x2.shape
    tm = min(512, seq)
    hw = onsa.shape[1]
    wo1 = w_out[:hw].astype(BF16)
    wo2 = w_out[hw:].astype(BF16)
    wrt = w_router.T
    rb = jnp.broadcast_to(router_bias.reshape(N_EXPERTS, 1), (N_EXPERTS, tm))
    row = lambda n: pl.BlockSpec((tm, n), lambda i: (i, 0))
    full = lambda a: pl.BlockSpec(a.shape, lambda i: (0,) * a.ndim)
    col = pl.BlockSpec((TOP_K, tm), lambda i: (0, i))
    ops = (wo1, wo2, g_ffn, wrt, rb, wgs.astype(BF16), wus.astype(BF16), wds.astype(BF16))
    return pl.pallas_call(
        _post_kernel,
        grid=(t // tm,),
        in_specs=[row(d), row(hw), row(oret.shape[1]),
                  pl.BlockSpec((1, 6, d), lambda i: ((i * tm) // seq, 0, 0))] + [full(a) for a in ops],
        out_specs=[row(d), pl.BlockSpec((tm * SUBLANES, LANES), lambda i: (i, 0)), col, col],
        out_shape=[jax.ShapeDtypeStruct((t, d), F32), jax.ShapeDtypeStruct((t * SUBLANES, LANES), F32),
                   jax.ShapeDtypeStruct((TOP_K, t), I32), jax.ShapeDtypeStruct((TOP_K, t), F32)],
        compiler_params=_cparams(("arbitrary",)),
        name="post",
    )(x2, onsa, oret, mod3, *ops)


def _expert_select(idx_row, table, eidx):
    return jnp.sum(jnp.where(eidx == idx_row, table, 0.0), axis=0, keepdims=True)


def _rank_kernel(idx_ref, tri_ref, rank_ref, cnt_ref, carry):
    @pl.when(pl.program_id(0) == 0)
    def _():
        carry[...] = jnp.zeros_like(carry)

    idx = idx_ref[...]
    eidx = lax.broadcasted_iota(I32, (N_EXPERTS, 1), 0)
    member = jnp.zeros((N_EXPERTS, idx.shape[1]), F32)
    for k in range(TOP_K):
        member = member + (eidx == idx[k:k + 1, :]).astype(F32)
    before = _dot(member.astype(BF16), tri_ref[...]) + carry[:, 0:1]
    rank_ref[...] = jnp.concatenate(
        [_expert_select(idx[k:k + 1, :], before, eidx) for k in range(TOP_K)], axis=0).astype(I32)
    carry[...] = carry[...] + jnp.sum(member, axis=1, keepdims=True)
    cnt_ref[...] = carry[...]


def _dest_kernel(idx_ref, rank_ref, start_ref, pos_ref):
    idx = idx_ref[...]
    eidx = lax.broadcasted_iota(I32, (N_EXPERTS, 1), 0)
    start = start_ref[:, 0:1]
    base = jnp.concatenate([_expert_select(idx[k:k + 1, :], start, eidx) for k in range(TOP_K)], axis=0)
    pos_ref[...] = rank_ref[...] + base.astype(I32)


def _route_plan(idx_t, n_tok):
    tm = min(512, n_tok)
    tri = jnp.asarray(np.triu(np.ones((tm, tm), np.float32), 1), BF16)
    col = pl.BlockSpec((TOP_K, tm), lambda i: (0, i))
    rank, cnt = pl.pallas_call(
        _rank_kernel,
        grid=(n_tok // tm,),
        in_specs=[col, pl.BlockSpec((tm, tm), lambda i: (0, 0))],
        out_specs=[col, pl.BlockSpec((N_EXPERTS, LANES), lambda i: (0, 0))],
        out_shape=[jax.ShapeDtypeStruct((TOP_K, n_tok), I32), jax.ShapeDtypeStruct((N_EXPERTS, LANES), F32)],
        scratch_shapes=[pltpu.VMEM((N_EXPERTS, LANES), F32)],
        compiler_params=_cparams(("arbitrary",)),
        name="route_rank",
    )(idx_t, tri)
    counts = cnt[:, 0].astype(I32)
    padded = ((counts + MOE_TILE - 1) // MOE_TILE) * MOE_TILE
    ends = jnp.cumsum(padded)
    starts = ends - padded
    start_b = jnp.broadcast_to(starts.astype(F32)[:, None], (N_EXPERTS, LANES))
    pos = pl.pallas_call(
        _dest_kernel,
        grid=(n_tok // tm,),
        in_specs=[col, col, pl.BlockSpec((N_EXPERTS, LANES), lambda i: (0, 0))],
        out_specs=col,
        out_shape=jax.ShapeDtypeStruct((TOP_K, n_tok), I32),
        compiler_params=_cparams(("arbitrary",)),
        name="route_dest",
    )(idx_t, rank, start_b)
    n_tiles = TOP_K * n_tok // MOE_TILE + N_EXPERTS
    n_used = (ends[-1] // MOE_TILE).astype(I32)
    tile_start = jnp.minimum(jnp.arange(n_tiles, dtype=I32), n_used - 1) * MOE_TILE
    tile_e = jnp.minimum(jnp.sum((ends[None, :] <= tile_start[:, None]).astype(I32), axis=1), N_EXPERTS - 1)
    first = jnp.concatenate([jnp.ones((1,), I32), (tile_e[1:] != tile_e[:-1]).astype(I32)])
    eidx = jnp.arange(N_EXPERTS, dtype=I32)
    used = counts > 0
    later = jnp.where((eidx[None, :] > eidx[:, None]) & used[None, :], eidx[None, :], N_EXPERTS)
    nxt = jnp.min(later, axis=1)
    next_e = jnp.where(nxt == N_EXPERTS, -1, nxt).astype(I32)
    parity = ((jnp.cumsum(used.astype(I32)) - 1) % 2).astype(I32)
    plan = (tile_e, first, n_used.reshape(1), next_e, parity)
    slot = jnp.arange(MOE_TILE, dtype=I32)[None, :]
    rem = (counts % MOE_TILE)[:, None]
    n_rows = n_tiles * MOE_TILE
    pad_key = jnp.where((rem != 0) & (slot >= rem),
                        (starts + (counts // MOE_TILE) * MOE_TILE)[:, None] + slot, n_rows)
    keys = jnp.concatenate([pos.reshape(-1), pad_key.reshape(-1).astype(I32)]).astype(jnp.uint32)
    toks = jnp.concatenate([jnp.tile(jnp.arange(n_tok, dtype=jnp.uint32), TOP_K),
                            jnp.zeros((N_EXPERTS * MOE_TILE,), jnp.uint32)])
    assert (n_rows + 1) * n_tok <= 2 ** 32, "plan row and token id share one 32-bit sort word"
    packed = lax.sort(keys * jnp.uint32(n_tok) + toks)
    row_tok = (packed % jnp.uint32(n_tok)).astype(I32).reshape(n_tiles, 1, MOE_TILE)
    return pos, plan, row_tok


def _rows_to_tiles(ref, val):
    n = val.shape[0]
    for s in range(SUBLANES):
        ref[pl.ds(s, n, stride=SUBLANES), :] = val[:, s * LANES:(s + 1) * LANES]


def _tiles_to_rows(ref, n, *lead):
    return jnp.concatenate([ref[(*lead, pl.ds(s, n, stride=SUBLANES), slice(None))] for s in range(SUBLANES)],
                           axis=1)


def _tile_rows(ref, row, n=1):
    start = pl.multiple_of(row * SUBLANES, SUBLANES)
    return ref.at[pl.ds(start, n * SUBLANES), :]


def _gather_row(hbuf, tok_ref, stage, j):
    t = tok_ref[0, 0, j]
    pair = hbuf[pl.ds(pl.multiple_of((t >> 1) * BF16_TILE_ROWS, BF16_TILE_ROWS), BF16_TILE_ROWS), :].astype(F32)
    start = j * SUBLANES if isinstance(j, int) else pl.multiple_of(j * SUBLANES, SUBLANES)
    stage[pl.ds(start, SUBLANES), :] = jnp.where((t & 1) == 1, pair[SUBLANES:], pair[:SUBLANES])


def _expert_kernel(tile_e, first, n_used, next_e, parity, tokc_ref, tokn_ref, h2_hbm, wg_hbm, wu_hbm, wd_hbm,
                   y_ref, hbuf, conv, stage_a, stage_b, wgf, wuf, wdf, wgb, wub, wdb, hsem, sem):
    i = pl.program_id(0)
    n = n_used[0]

    def weight_copies(e, slot):
        return [pltpu.make_async_copy(src.at[e], dst.at[slot], sem.at[slot])
                for src, dst in ((wg_hbm, wgf), (wu_hbm, wuf), (wd_hbm, wdf))]

    @pl.when(i == 0)
    def _():
        for cp in weight_copies(tile_e[0], parity[tile_e[0]]):
            cp.start()
        rows = conv.shape[1]
        n_conv = h2_hbm.shape[0] // rows

        def chunk_copy(c, slot):
            return pltpu.make_async_copy(h2_hbm.at[pl.ds(pl.multiple_of(c * rows, rows), rows), :], conv.at[slot],
                                         hsem.at[slot])

        chunk_copy(0, 0).start()

        def convert(c, carry):
            slot = c % 2
            chunk_copy(c, slot).wait()

            @pl.when(c + 1 < n_conv)
            def _():
                chunk_copy(c + 1, 1 - slot).start()

            hbuf[pl.ds(pl.multiple_of(c * rows, rows), rows), :] = conv[slot].astype(BF16)
            return carry
        lax.fori_loop(0, n_conv, convert, 0)

        def first_tile(jj, carry):
            for u in range(SUBLANES):
                _gather_row(hbuf, tokc_ref, stage_a, jj * SUBLANES + u)
            return carry
        lax.fori_loop(0, MOE_TILE // SUBLANES, first_tile, 0)

    @pl.when(i < n)
    def _():
        @pl.when(first[i] == 1)
        def _():
            e = tile_e[i]
            slot = parity[e]
            for cp in weight_copies(e, slot):
                cp.wait()

            @pl.when(next_e[e] >= 0)
            def _():
                for cp in weight_copies(next_e[e], 1 - slot):
                    cp.start()

            wgb[...] = wgf[slot].astype(BF16)
            wub[...] = wuf[slot].astype(BF16)
            wdb[...] = wdf[slot].astype(BF16)

        for par, (cur, nxt) in enumerate(((stage_a, stage_b), (stage_b, stage_a))):
            @pl.when(i % 2 == par)
            def _():
                for j in range(MOE_TILE):
                    _gather_row(hbuf, tokn_ref, nxt, j)
                xb = _tiles_to_rows(cur, MOE_TILE).astype(BF16)
                hid = (_silu(_dot(xb, wgb[...])) * _dot(xb, wub[...])).astype(BF16)
                _rows_to_tiles(y_ref, _dot(hid, wdb[...]))

    @pl.when(i >= n)
    def _():
        y_ref[...] = jnp.zeros_like(y_ref)


def _experts(h2, row_tok, plan, wg, wu, wd):
    n_tiles = row_tok.shape[0]
    d, de = wg.shape[1], wg.shape[2]
    blk = (MOE_TILE * SUBLANES, LANES)
    conv_rows = min(4096, h2.shape[0])
    assert h2.shape[0] % conv_rows == 0 and h2.shape[0] % BF16_TILE_ROWS == 0
    hbm = pl.BlockSpec(memory_space=pl.ANY)
    grid_spec = pltpu.PrefetchScalarGridSpec(
        num_scalar_prefetch=len(plan),
        grid=(n_tiles,),
        in_specs=[pl.BlockSpec((1, 1, MOE_TILE), lambda i, te, fi, nu, ne, pa: (jnp.minimum(i, nu[0] - 1), 0, 0),
                               memory_space=pltpu.SMEM),
                  pl.BlockSpec((1, 1, MOE_TILE), lambda i, te, fi, nu, ne, pa: (jnp.minimum(i + 1, nu[0] - 1), 0, 0),
                               memory_space=pltpu.SMEM), hbm, hbm, hbm, hbm],
        out_specs=pl.BlockSpec(blk, lambda i, te, fi, nu, ne, pa: (i, 0)),
        scratch_shapes=[pltpu.VMEM(h2.shape, BF16), pltpu.VMEM((2, conv_rows, LANES), F32),
                        pltpu.VMEM(blk, F32), pltpu.VMEM(blk, F32),
                        pltpu.VMEM((2, d, de), F32), pltpu.VMEM((2, d, de), F32), pltpu.VMEM((2, de, d), F32),
                        pltpu.VMEM((d, de), BF16), pltpu.VMEM((d, de), BF16), pltpu.VMEM((de, d), BF16),
                        pltpu.SemaphoreType.DMA((2,)), pltpu.SemaphoreType.DMA((2,))])
    return pl.pallas_call(
        _expert_kernel,
        grid_spec=grid_spec,
        out_shape=jax.ShapeDtypeStruct((n_tiles * MOE_TILE * SUBLANES, LANES), F32),
        compiler_params=pltpu.CompilerParams(dimension_semantics=("arbitrary",),
                                             vmem_limit_bytes=EXPERTS_VMEM_LIMIT),
        name="experts",
    )(*plan, row_tok, row_tok, h2, wg, wu, wd)


def _combine_kernel(posc_ref, posn_ref, ys_hbm, acc_ref, w_ref, mod_ref, g_ref, o_ref, buf, sem):
    i = pl.program_id(0)
    n = pl.num_programs(0)
    tc = acc_ref.shape[0]

    def issue(pos_ref, slot):
        for k in range(TOP_K):
            def body(jj, carry):
                for u in range(8):
                    j = jj * 8 + u
                    pltpu.make_async_copy(_tile_rows(ys_hbm, pos_ref[0, k, j]), _tile_rows(buf.at[slot, k], j),
                                          sem.at[slot]).start()
                return carry
            lax.fori_loop(0, tc // 8, body, 0)

    @pl.when(i == 0)
    def _():
        issue(posc_ref, 0)

    @pl.when(i + 1 < n)
    def _():
        issue(posn_ref, (i + 1) % 2)

    slot = i % 2
    for k in range(TOP_K):
        pltpu.make_async_copy(_tile_rows(ys_hbm, 0, tc), buf.at[slot, k], sem.at[slot]).wait()
    w = w_ref[...]
    routed = w[:, 0:1] * _tiles_to_rows(buf, tc, slot, 0)
    for k in range(1, TOP_K):
        routed = routed + w[:, k:k + 1] * _tiles_to_rows(buf, tc, slot, k)
    x2 = acc_ref[...] + mod_ref[0, 5:6, :] * routed
    o_ref[...] = x2 * lax.rsqrt(jnp.mean(x2 * x2, axis=-1, keepdims=True) + RMS_EPS) * g_ref[...]


def _tile_pos(pos_t, tc):
    return pos_t.reshape(TOP_K, pos_t.shape[1] // tc, tc).transpose(1, 0, 2)


def _combine(ys, pos3, acc0, w_tok, mod3, g_final, seq):
    t, d = acc0.shape
    tc = pos3.shape[2]
    nt = t // tc
    return pl.pallas_call(
        _combine_kernel,
        grid=(nt,),
        in_specs=[pl.BlockSpec((1, TOP_K, tc), lambda i: (i, 0, 0), memory_space=pltpu.SMEM),
                  pl.BlockSpec((1, TOP_K, tc), lambda i: (jnp.minimum(i + 1, nt - 1), 0, 0),
                               memory_space=pltpu.SMEM),
                  pl.BlockSpec(memory_space=pl.ANY),
                  pl.BlockSpec((tc, d), lambda i: (i, 0)),
                  pl.BlockSpec((tc, TOP_K), lambda i: (i, 0)),
                  pl.BlockSpec((1, 6, d), lambda i: ((i * tc) // seq, 0, 0)),
                  pl.BlockSpec((1, d), lambda i: (0, 0))],
        out_specs=pl.BlockSpec((tc, d), lambda i: (i, 0)),
        out_shape=jax.ShapeDtypeStruct((t, d), F32),
        scratch_shapes=[pltpu.VMEM((2, TOP_K, tc * SUBLANES, LANES), F32), pltpu.SemaphoreType.DMA((2,))],
        compiler_params=_cparams(("arbitrary",)),
        name="combine",
    )(pos3, pos3, ys, acc0, w_tok, mod3, g_final)


def kernel(x, c, positions, w_ada, b_ada, g_norm_mix, w_in, cmp_pos_k, cmp_w1_k, cmp_w2_k, cmp_pos_v,
           cmp_w1_v, cmp_w2_v, g_nsa_out, g_ret_out, w_out, g_norm_ffn, w_router, router_bias,
           w_gate_e, w_up_e, w_down_e, w_gate_s, w_up_s, w_down_s, g_norm_final):
    bsz, seq, d = x.shape
    assert d == SUBLANES * LANES, "MoE rows are moved as one (8, 128) tile each"
    t = bsz * seq
    x2 = x.reshape(t, d)
    cos_t, sin_t = _rope_tables(positions.reshape(t, 1).astype(F32))
    n_cmp_pad = seq // CMP_STRIDE

    def cmp_rows(tab):
        rows = tab[:, :LANES].reshape(bsz, seq, LANES)[:, CMP_BLOCK - 1::CMP_STRIDE]
        return jnp.pad(rows, ((0, 0), (0, n_cmp_pad - rows.shape[1]), (0, 0)))

    cos_c, sin_c = cmp_rows(cos_t), cmp_rows(sin_t)
    for l in range(w_in.shape[0]):
        mod3 = _ada(c, w_ada[l], b_ada[l]).reshape(bsz, 6, d)
        q, kc, vc, kk, gates, qr, kr, vr, gr, vsel_t, vwin_t = _inproj(
            x2, mod3, g_norm_mix[l].reshape(1, d), _arrange_w_in(w_in[l]), cos_t, sin_t, seq)
        kcmp, vcmp = _compress(kc, vc, cos_c, sin_c,
                               _arrange_cmp_weights(cmp_pos_k[l], cmp_w1_k[l], cmp_w2_k[l]),
                               _arrange_cmp_weights(cmp_pos_v[l], cmp_w1_v[l], cmp_w2_v[l]), bsz, seq)
        onsa = _nsa(q, kcmp, vcmp, kk, vsel_t, vwin_t, gates, g_nsa_out[l].reshape(1, -1), bsz, seq)
        oret = _retention(qr, kr, vr, gr, g_ret_out[l], bsz, seq)
        acc0, h2, idx_t, w_t = _post(x2, onsa, oret, mod3, w_out[l], g_norm_ffn[l].reshape(1, d),
                                     w_router[l], router_bias[l], w_gate_s[l], w_up_s[l], w_down_s[l], seq)
        pos_t, plan, row_tok = _route_plan(idx_t, t)
        ys = _experts(h2, row_tok, plan, w_gate_e[l], w_up_e[l], w_down_e[l])
        last = l == w_in.shape[0] - 1
        gfin = g_norm_final.reshape(1, d)
        x2 = _combine(ys, _tile_pos(pos_t, 128), acc0, w_t.T, mod3, gfin, seq)
        assert last, "final norm is fused into the combine stage; depth 1 only"
    return x2.reshape(bsz, seq, d)
```

```python
import functools
import math

import numpy as np
import jax
import jax.numpy as jnp
from jax import lax
from jax.experimental import pallas as pl
from jax.experimental.pallas import tpu as pltpu

F32 = jnp.float32
BF16 = jnp.bfloat16
I32 = jnp.int32
HIGHEST = lax.Precision.HIGHEST

LANES = 128
SUBLANES = 8
BF16_TILE_ROWS = 16
NSA_HEAD_DIM = 64
NSA_HEADS = 8
NSA_GROUPS = 2
NSA_Q_PER_KV = NSA_HEADS // NSA_GROUPS
CMP_BLOCK = 32
CMP_STRIDE = 16
CMP_HIDDEN = 128
SEL_BLOCK = 64
SEL_TOP = 8
WINDOW = 512
Q_BLOCK = 128
FORCE_BONUS = 1.0e4
RET_HEADS = 4
RET_HEAD_DIM = 128
RET_CHUNK = 128
ROPE_THETA = 10000.0
RMS_EPS = 1e-6
N_EXPERTS = 256
N_EXPERT_GROUPS = 8
TOPK_GROUPS = 4
TOP_K = 8
ROUTED_SCALE = 2.5
MOE_TILE = 256
WEIGHT_SLOTS = 3
SEL_CHUNK = 256
MASK_BIG = 2.0 ** 100
MASK_FLOOR = 2.0 ** 99
NEG = -1e30
VMEM_LIMIT = 48 * 1024 * 1024
EXPERTS_VMEM_LIMIT = 56 * 1024 * 1024


def _cparams(sem):
    return pltpu.CompilerParams(dimension_semantics=sem, vmem_limit_bytes=VMEM_LIMIT)


def _dot(a, b, **kw):
    return jnp.dot(a, b, preferred_element_type=F32, **kw)


def _dot_nt(a, b, **kw):
    return lax.dot_general(a, b, (((1,), (1,)), ((), ())), preferred_element_type=F32, **kw)


def _sigmoid(v):
    return 1.0 / (1.0 + jnp.exp(-v))


def _silu(v):
    return v * _sigmoid(v)


def _ada_kernel(c_ref, w_ref, b_ref, o_ref):
    o_ref[...] = _dot(_silu(c_ref[...]), w_ref[...], precision=HIGHEST) + b_ref[...]


def _ada(c, w, b):
    bsz, d = c.shape
    n = w.shape[1]
    tn = 1536
    cp = jnp.zeros((8, d), F32).at[:bsz].set(c)
    out = pl.pallas_call(
        _ada_kernel,
        grid=(n // tn,),
        in_specs=[pl.BlockSpec((8, d), lambda j: (0, 0)),
                  pl.BlockSpec((d, tn), lambda j: (0, j)),
                  pl.BlockSpec((1, tn), lambda j: (0, j))],
        out_specs=pl.BlockSpec((8, tn), lambda j: (0, j)),
        out_shape=jax.ShapeDtypeStruct((8, n), F32),
        compiler_params=_cparams(("arbitrary",)),
        name="ada",
    )(cp, w, b.reshape(1, n))
    return out[:bsz]


def _rope_table_kernel(pos_ref, inv_ref, sgn_ref, cos_ref, sin_ref):
    ang = pos_ref[...] * inv_ref[...]
    cos_ref[...] = jnp.cos(ang)
    sin_ref[...] = jnp.sin(ang) * sgn_ref[...]


def _rope_tables(posf):
    t = posf.shape[0]
    inv_n = ROPE_THETA ** (-jnp.arange(0, NSA_HEAD_DIM, 2, dtype=F32) / NSA_HEAD_DIM)
    inv_r = ROPE_THETA ** (-jnp.arange(0, RET_HEAD_DIM, 2, dtype=F32) / RET_HEAD_DIM)
    inv = jnp.concatenate([jnp.tile(inv_n, 4), jnp.tile(inv_r, 2)]).reshape(1, 2 * LANES)
    sgn_n = np.where((np.arange(LANES) % NSA_HEAD_DIM) < NSA_HEAD_DIM // 2, -1.0, 1.0)
    sgn_r = np.where(np.arange(LANES) < RET_HEAD_DIM // 2, -1.0, 1.0)
    sgn = jnp.asarray(np.concatenate([sgn_n, sgn_r]).reshape(1, 2 * LANES), F32)
    tm = min(t, 1024)
    return pl.pallas_call(
        _rope_table_kernel,
        grid=(t // tm,),
        in_specs=[pl.BlockSpec((tm, 1), lambda i: (i, 0)),
                  pl.BlockSpec((1, 2 * LANES), lambda i: (0, 0)),
                  pl.BlockSpec((1, 2 * LANES), lambda i: (0, 0))],
        out_specs=[pl.BlockSpec((tm, 2 * LANES), lambda i: (i, 0))] * 2,
        out_shape=[jax.ShapeDtypeStruct((t, 2 * LANES), F32)] * 2,
        compiler_params=_cparams(("arbitrary",)),
        name="rope_tables",
    )(posf, inv, sgn)


def _rope64(p, cos, sin, first_half):
    rot = jnp.where(first_half, pltpu.roll(p, 96, 1), pltpu.roll(p, 32, 1))
    return p * cos + rot * sin


def _rope128(p, cos, sin):
    return p * cos + pltpu.roll(p, 64, 1) * sin


_C_Q = 0
_C_KV = _C_Q + NSA_HEADS * LANES
_C_GATE = _C_KV + 6 * LANES
_C_RET = _C_GATE + LANES
_C_END = _C_RET + 4 * RET_HEADS * RET_HEAD_DIM


def _arrange_w_in(w_in):
    d = w_in.shape[0]
    nw = NSA_HEADS * NSA_HEAD_DIM
    q = w_in[:, :nw].reshape(d, NSA_HEADS, NSA_HEAD_DIM)
    z = jnp.zeros_like(q)
    grp = (jnp.arange(NSA_HEADS) // NSA_Q_PER_KV)[None, :, None]
    qpad = jnp.where(grp == 0, jnp.concatenate([q, z], -1), jnp.concatenate([z, q], -1))
    qpad = qpad.reshape(d, NSA_HEADS * LANES)
    kv = w_in[:, nw:nw + 6 * LANES]
    g0 = nw + 6 * LANES
    gate = jnp.pad(w_in[:, g0:g0 + 3 * NSA_HEADS], ((0, 0), (0, LANES - 3 * NSA_HEADS)))
    ret = w_in[:, g0 + 3 * NSA_HEADS:]
    return jnp.concatenate([qpad, kv, gate, ret], axis=1).astype(BF16)


def _inproj_kernel(x_ref, mod_ref, g_ref, w_ref, cos_ref, sin_ref,
                   q_ref, kc_ref, vc_ref, kk_ref, gate_ref, qr_ref, kr_ref, vr_ref, gr_ref, vst_ref, vwt_ref):
    x = x_ref[...]
    tm = x.shape[0]
    ms = jnp.mean(x * x, axis=-1, keepdims=True)
    y = x * lax.rsqrt(ms + RMS_EPS) * g_ref[...]
    h = y * (1.0 + mod_ref[0, 1:2, :]) + mod_ref[0, 0:1, :]
    hb = h.astype(BF16)
    cos_n, sin_n = cos_ref[:, 0:LANES], sin_ref[:, 0:LANES]
    cos_r, sin_r = cos_ref[:, LANES:], sin_ref[:, LANES:]
    lane = lax.broadcasted_iota(I32, (tm, LANES), 1)
    first_half = (lane % NSA_HEAD_DIM) < (NSA_HEAD_DIM // 2)
    scale_n = NSA_HEAD_DIM ** -0.5
    scale_r = RET_HEAD_DIM ** -0.5

    def proj(c0, n):
        return _dot(hb, w_ref[:, c0:c0 + n])

    for hh in range(NSA_HEADS):
        p = proj(_C_Q + hh * LANES, LANES)
        q_ref[:, hh * LANES:(hh + 1) * LANES] = (_rope64(p, cos_n, sin_n, first_half) * scale_n).astype(BF16)
    kv = proj(_C_KV, 6 * LANES)
    kc_ref[...] = kv[:, 0:LANES].astype(BF16)
    vc_ref[...] = kv[:, LANES:2 * LANES].astype(BF16)
    kk_ref[:, 0:LANES] = _rope64(kv[:, 2 * LANES:3 * LANES], cos_n, sin_n, first_half).astype(BF16)
    kk_ref[:, LANES:2 * LANES] = _rope64(kv[:, 4 * LANES:5 * LANES], cos_n, sin_n, first_half).astype(BF16)
    group0 = lane < NSA_HEAD_DIM
    for c0, vt_ref, chunk in ((3 * LANES, vst_ref, SEL_CHUNK), (5 * LANES, vwt_ref, Q_BLOCK)):
        v = kv[:, c0:c0 + LANES]
        for g, vg in enumerate((jnp.where(group0, v, 1.0), jnp.where(group0, 1.0, v))):
            for cc in range(tm // chunk):
                vt_ref[cc, g] = vg[cc * chunk:(cc + 1) * chunk].T.astype(BF16)
    gate_ref[...] = _sigmoid(proj(_C_GATE, LANES))
    rw = RET_HEADS * RET_HEAD_DIM
    for hh in range(RET_HEADS):
        sl = slice(hh * LANES, (hh + 1) * LANES)
        pq = proj(_C_RET + hh * LANES, LANES)
        qr_ref[:, sl] = _rope128(pq, cos_r, sin_r).astype(BF16)
        pk = proj(_C_RET + rw + hh * LANES, LANES)
        kr_ref[:, sl] = (_rope128(pk, cos_r, sin_r) * scale_r).astype(BF16)
    vr_ref[...] = proj(_C_RET + 2 * rw, rw).astype(BF16)
    gr_ref[...] = proj(_C_RET + 3 * rw, rw)


def _inproj(x2, mod3, g_mix, w_in_p, cos_t, sin_t, seq):
    t, d = x2.shape
    tm = min(512, seq)
    rw = RET_HEADS * RET_HEAD_DIM
    row = lambda n: pl.BlockSpec((tm, n), lambda i: (i, 0))
    outs = [(NSA_HEADS * LANES, BF16), (LANES, BF16), (LANES, BF16), (2 * LANES, BF16), (LANES, F32),
            (rw, BF16), (rw, BF16), (rw, BF16), (rw, F32)]
    vt_specs, vt_shapes = [], []
    for chunk in (SEL_CHUNK, Q_BLOCK):
        vt_specs.append(pl.BlockSpec((tm // chunk, NSA_GROUPS, LANES, chunk), lambda i: (i, 0, 0, 0)))
        vt_shapes.append(jax.ShapeDtypeStruct((t // chunk, NSA_GROUPS, LANES, chunk), BF16))
    return pl.pallas_call(
        _inproj_kernel,
        grid=(t // tm,),
        in_specs=[row(d),
                  pl.BlockSpec((1, 6, d), lambda i: ((i * tm) // seq, 0, 0)),
                  pl.BlockSpec((1, d), lambda i: (0, 0)),
                  pl.BlockSpec((d, _C_END), lambda i: (0, 0)),
                  row(2 * LANES), row(2 * LANES)],
        out_specs=[row(n) for n, _ in outs] + vt_specs,
        out_shape=[jax.ShapeDtypeStruct((t, n), dt) for n, dt in outs] + vt_shapes,
        compiler_params=_cparams(("arbitrary",)),
        name="inproj",
    )(x2, mod3, g_mix, w_in_p, cos_t, sin_t)


def _gelu_tanh(v):
    return v * (0.5 * (1.0 + jnp.tanh(math.sqrt(2.0 / math.pi) * (v + 0.044715 * (v * v * v)))))


def _compress_kernel(kc_ref, vc_ref, wkt_ref, wkb_ref, wk2_ref, pk_ref, wvt_ref, wvb_ref, wv2_ref, pv_ref,
                     cos_ref, sin_ref, ko_ref, vo_ref):
    def one(x_ref, wt_ref, wb_ref, w2_ref, p_ref):
        xx = x_ref[0]
        a = _dot(xx, wt_ref[...])
        b = _dot(xx, wb_ref[...])
        pb = p_ref[...].astype(BF16)
        bias = _dot(pb, wt_ref[...])[0:1] + _dot(pb, wb_ref[...])[1:2]
        hid = a + pltpu.roll(b, b.shape[0] - 1, 0) + bias
        return _dot(_gelu_tanh(hid).astype(BF16), w2_ref[...])

    k = one(kc_ref, wkt_ref, wkb_ref, wk2_ref, pk_ref)
    lane = lax.broadcasted_iota(I32, k.shape, 1)
    first_half = (lane % NSA_HEAD_DIM) < (NSA_HEAD_DIM // 2)
    ko_ref[0] = _rope64(k, cos_ref[0], sin_ref[0], first_half).astype(BF16)
    v = one(vc_ref, wvt_ref, wvb_ref, wv2_ref, pv_ref)
    group0 = lane < NSA_HEAD_DIM
    vo_ref[0, 0] = jnp.where(group0, v, 1.0).T.astype(BF16)
    vo_ref[0, 1] = jnp.where(group0, 1.0, v).T.astype(BF16)


def _arrange_cmp_weights(pos, w1, w2):
    half = CMP_BLOCK // 2
    dh = NSA_HEAD_DIM
    w1r = w1.reshape(CMP_BLOCK, dh, CMP_HIDDEN)

    def block(wpart):
        z = jnp.zeros_like(wpart)
        g0 = jnp.concatenate([wpart, z], axis=-1)
        g1 = jnp.concatenate([z, wpart], axis=-1)
        return jnp.stack([g0, g1], axis=1).reshape(half * 2 * dh, 2 * CMP_HIDDEN)

    wt, wb = block(w1r[:half]), block(w1r[half:])
    z2 = jnp.zeros_like(w2)
    w2b = jnp.concatenate([jnp.concatenate([w2, z2], 1), jnp.concatenate([z2, w2], 1)], 0)
    ptop = jnp.tile(pos[:half], (1, 2)).reshape(1, -1)
    pbot = jnp.tile(pos[half:], (1, 2)).reshape(1, -1)
    prow = jnp.concatenate([ptop, pbot, jnp.zeros((6, ptop.shape[1]), F32)], 0)
    return wt.astype(BF16), wb.astype(BF16), w2b.astype(BF16), prow


def _compress(kc, vc, cos_c, sin_c, kparams, vparams, bsz, seq):
    nchunk = seq // CMP_STRIDE
    width = CMP_STRIDE * LANES
    kc16 = kc.reshape(bsz, nchunk, width)
    vc16 = vc.reshape(bsz, nchunk, width)
    full = lambda a: pl.BlockSpec(a.shape, lambda b: (0,) * a.ndim)
    per_b = lambda n: pl.BlockSpec((1, nchunk, n), lambda b: (b, 0, 0))
    return pl.pallas_call(
        _compress_kernel,
        grid=(bsz,),
        in_specs=[per_b(width), per_b(width)] + [full(a) for a in kparams] + [full(a) for a in vparams]
                 + [per_b(LANES), per_b(LANES)],
        out_specs=[per_b(LANES), pl.BlockSpec((1, NSA_GROUPS, LANES, nchunk), lambda b: (b, 0, 0, 0))],
        out_shape=[jax.ShapeDtypeStruct((bsz, nchunk, LANES), BF16),
                   jax.ShapeDtypeStruct((bsz, NSA_GROUPS, LANES, nchunk), BF16)],
        compiler_params=_cparams(("arbitrary",)),
        name="compress",
    )(kc16, vc16, *kparams, *vparams, cos_c, sin_c)


def _softmax_chunk(m, acc, s, vt):
    m_new = jnp.maximum(m, jnp.max(s, axis=0, keepdims=True))
    e = jnp.exp(s - m_new).astype(BF16)
    acc = jnp.exp(m - m_new) * acc + _dot(vt, e)
    return m_new, acc


def _nsa_kernel(q_ref, kc_ref, vct_ref, ks_ref, kw_ref, vst_ref, vwt_ref, gate_ref,
                ovt_ref, exp_ref, g_ref, o_ref, *, seq):
    qb = pl.program_id(1)
    t0 = qb * Q_BLOCK
    cols = NSA_Q_PER_KV * Q_BLOCK
    n_cmp_pad = kc_ref.shape[1]
    n_sel = seq // SEL_BLOCK
    t_row = t0 + lax.broadcasted_iota(I32, (1, Q_BLOCK), 1)
    gates_t = gate_ref[...].T
    lane = lax.broadcasted_iota(I32, (Q_BLOCK, LANES), 1)
    key = lax.broadcasted_iota(I32, (Q_BLOCK, Q_BLOCK), 0)
    tok = lax.broadcasted_iota(I32, (Q_BLOCK, Q_BLOCK), 1)
    cend = lax.broadcasted_iota(I32, (n_cmp_pad, 1), 0) * CMP_STRIDE + (CMP_BLOCK - 1)
    bias_cmp = jnp.where(cend <= t_row, 0.0, -MASK_BIG)
    last_c = qb // (SEL_CHUNK // Q_BLOCK)
    kpos_last = last_c * SEL_CHUNK + lax.broadcasted_iota(I32, (SEL_CHUNK, 1), 0)
    bias_diag = jnp.where(kpos_last <= t_row, 0.0, -MASK_BIG)
    n_win = WINDOW // Q_BLOCK
    bias_win_first = jnp.where(key > tok, 0.0, -MASK_BIG)
    bias_win_last = jnp.where(key <= tok, 0.0, -MASK_BIG)
    init = (jnp.full((1, cols), -MASK_FLOOR, F32), jnp.zeros((LANES, cols), F32))
    tile_heads = lambda b: jnp.concatenate([b] * NSA_Q_PER_KV, axis=1)

    def finish(acc, g):
        ones_row = NSA_HEAD_DIM * (1 - g)
        return acc * (1.0 / jnp.maximum(acc[ones_row:ones_row + 1, :], 1e-20))

    q4s, o_cs, sel_bs = [], [], []
    for g in range(NSA_GROUPS):
        q4 = jnp.concatenate([q_ref[:, (NSA_Q_PER_KV * g + r) * LANES:(NSA_Q_PER_KV * g + r + 1) * LANES]
                              for r in range(NSA_Q_PER_KV)], axis=0)
        q4s.append(q4)
        s_c = _dot_nt(kc_ref[0], q4) + tile_heads(bias_cmp)
        e_c = jnp.exp(s_c - jnp.maximum(jnp.max(s_c, axis=0, keepdims=True), -MASK_FLOOR))
        p_c = e_c * (1.0 / jnp.maximum(jnp.sum(e_c, axis=0, keepdims=True), 1e-20))
        psum = p_c[:, 0:Q_BLOCK]
        for r in range(1, NSA_Q_PER_KV):
            psum = psum + p_c[:, r * Q_BLOCK:(r + 1) * Q_BLOCK]
        o_cs.append(finish(_dot(vct_ref[0, g], e_c.astype(BF16)), g))
        imp_t = _dot(ovt_ref[...], psum, precision=HIGHEST)
        nsp = imp_t.shape[0]
        jrow = lax.broadcasted_iota(I32, (nsp, 1), 0)
        cur = t_row // SEL_BLOCK
        valid = (jrow * SEL_BLOCK <= t_row) & (jrow < n_sel)
        forced = (jrow == 0) | (jrow == cur) | (jrow == cur - 1)
        val = jnp.where(valid, imp_t + jnp.where(forced, FORCE_BONUS, 0.0), -1.0)
        val = jnp.where(jrow < n_sel, val, -jnp.inf)
        sel_t = jnp.zeros((nsp, Q_BLOCK), F32)
        for _ in range(min(SEL_TOP, n_sel)):
            mx = jnp.max(val, axis=0, keepdims=True)
            jmin = jnp.min(jnp.where(val == mx, jrow, nsp), axis=0, keepdims=True)
            hit = jrow == jmin
            sel_t = jnp.where(hit, 1.0, sel_t)
            val = jnp.where(hit, -jnp.inf, val)
        sel_bs.append(jnp.concatenate([sel_t[:n_sel], jnp.ones((SUBLANES, Q_BLOCK), F32),
                                       jnp.zeros((LANES - n_sel - SUBLANES, Q_BLOCK), F32)], axis=0).astype(BF16))

    def sel_scores(c):
        keys = ks_ref[pl.ds(pl.multiple_of(c * SEL_CHUNK, SEL_CHUNK), SEL_CHUNK), :]
        expand = exp_ref[c]
        return tuple(_dot_nt(keys, q4s[g]) + tile_heads(_dot(expand, sel_bs[g])) for g in range(NSA_GROUPS))

    def sel_reduce(c, state, scores):
        out = []
        for g in range(NSA_GROUPS):
            out.extend(_softmax_chunk(state[2 * g], state[2 * g + 1], scores[g], vst_ref[c, g]))
        return tuple(out)

    def sel_step(c, carry):
        state, scores = carry
        return sel_reduce(c, state, scores), sel_scores(c + 1)

    state, scores = lax.fori_loop(0, last_c, sel_step, (init * NSA_GROUPS, sel_scores(0)))
    diag = tile_heads(bias_diag)
    state = sel_reduce(last_c, state, tuple(s + diag for s in scores))
    o_ss = [finish(state[2 * g + 1], g) for g in range(NSA_GROUPS)]
    s_ws = [[] for _ in range(NSA_GROUPS)]
    kbs = []
    for w in range(n_win + 1):
        wb = qb - n_win + w
        kb = jnp.maximum(wb, 0)
        kbs.append(kb)
        keys = kw_ref[pl.ds(pl.multiple_of(kb * Q_BLOCK, Q_BLOCK), Q_BLOCK), :]
        if w == n_win:
            bias = bias_win_last
        else:
            before_start = jnp.where(wb >= 0, 0.0, -MASK_BIG)
            bias = (bias_win_first + before_start) if w == 0 else jnp.full((Q_BLOCK, Q_BLOCK), before_start)
        for g in range(NSA_GROUPS):
            s_ws[g].append(_dot_nt(keys, q4s[g]) + tile_heads(bias))
    o_ws = []
    for g in range(NSA_GROUPS):
        m_w = jnp.full((1, cols), -MASK_FLOOR, F32)
        for s in s_ws[g]:
            m_w = jnp.maximum(m_w, jnp.max(s, axis=0, keepdims=True))
        acc = None
        for kb, s in zip(kbs, s_ws[g]):
            pv = _dot(vwt_ref[kb, g], jnp.exp(s - m_w).astype(BF16))
            acc = pv if acc is None else acc + pv
        o_ws.append(finish(acc, g))
    heads_out = []
    for g in range(NSA_GROUPS):
        o_c, o_s, o_w = o_cs[g], o_ss[g], o_ws[g]
        for r in range(NSA_Q_PER_KV):
            hh = NSA_Q_PER_KV * g + r
            cs = slice(r * Q_BLOCK, (r + 1) * Q_BLOCK)
            o = (gates_t[3 * hh:3 * hh + 1, :] * o_c[:, cs] + gates_t[3 * hh + 1:3 * hh + 2, :] * o_s[:, cs]
                 + gates_t[3 * hh + 2:3 * hh + 3, :] * o_w[:, cs]).T
            if (hh % 2) != g:
                o = pltpu.roll(o, NSA_HEAD_DIM, 1)
            heads_out.append(o)
    blocks = [jnp.where(lane < NSA_HEAD_DIM, heads_out[2 * i], heads_out[2 * i + 1])
              for i in range(NSA_HEADS // 2)]
    ss = sum(jnp.sum(b * b, axis=-1, keepdims=True) for b in blocks)
    inv = lax.rsqrt(ss / (NSA_HEADS * NSA_HEAD_DIM) + RMS_EPS)
    for i, b in enumerate(blocks):
        sl = slice(i * LANES, (i + 1) * LANES)
        o_ref[:, sl] = (b * inv * g_ref[:, sl]).astype(BF16)


def _nsa(q, kcmp, vcmp_t, kk, vsel_t, vwin_t, gates, g_nsa, bsz, seq):
    t = bsz * seq
    nq = seq // Q_BLOCK
    n_cmp_pad = seq // CMP_STRIDE
    n_cmp = (seq - CMP_BLOCK) // CMP_STRIDE + 1
    n_sel = seq // SEL_BLOCK
    assert n_sel % SUBLANES == 0 and n_sel + SUBLANES <= LANES, "selection mask needs a spare expansion row"
    nsp = n_sel
    cs = np.arange(n_cmp_pad) * CMP_STRIDE
    ss = np.arange(nsp) * SEL_BLOCK
    ov = ((cs[None, :] < ss[:, None] + SEL_BLOCK) & (cs[None, :] + CMP_BLOCK > ss[:, None])
          & (np.arange(n_cmp_pad)[None, :] < n_cmp) & (np.arange(nsp)[:, None] < n_sel))
    ovt = jnp.asarray(ov, F32)
    nch = seq // SEL_CHUNK
    kp = np.arange(seq).reshape(nch, SEL_CHUNK, 1)
    col = np.arange(LANES).reshape(1, 1, LANES)
    expand = jnp.asarray(np.where(kp // SEL_BLOCK == col, MASK_BIG, 0.0) + np.where(col == n_sel, -MASK_BIG, 0.0),
                         BF16)
    seqcol = lambda c: pl.BlockSpec((seq, LANES), lambda b, i: (b, c))
    per_b = lambda a: pl.BlockSpec((a.shape[0] // bsz,) + a.shape[1:], lambda b, i: (b,) + (0,) * (a.ndim - 1))
    return pl.pallas_call(
        functools.partial(_nsa_kernel, seq=seq),
        grid=(bsz, nq),
        in_specs=[pl.BlockSpec((Q_BLOCK, NSA_HEADS * LANES), lambda b, i: (b * nq + i, 0)),
                  per_b(kcmp), per_b(vcmp_t), seqcol(0), seqcol(1), per_b(vsel_t), per_b(vwin_t),
                  pl.BlockSpec((Q_BLOCK, LANES), lambda b, i: (b * nq + i, 0)),
                  pl.BlockSpec(ovt.shape, lambda b, i: (0, 0)),
                  pl.BlockSpec(expand.shape, lambda b, i: (0, 0, 0)),
                  pl.BlockSpec((1, NSA_HEADS * NSA_HEAD_DIM), lambda b, i: (0, 0))],
        out_specs=pl.BlockSpec((Q_BLOCK, NSA_HEADS * NSA_HEAD_DIM), lambda b, i: (b * nq + i, 0)),
        out_shape=jax.ShapeDtypeStruct((t, NSA_HEADS * NSA_HEAD_DIM), BF16),
        compiler_params=_cparams(("arbitrary", "arbitrary")),
        name="nsa",
    )(q, kcmp, vcmp_t, kk, kk, vsel_t, vwin_t, gates, ovt, expand, g_nsa)


def _ret_kernel(q_ref, k_ref, v_ref, gr_ref, dec_ref, xi_ref, zeta_ref, gch_ref, g_ref, o_ref, st_ref):
    @pl.when(pl.program_id(1) == 0)
    def _():
        st_ref[...] = jnp.zeros_like(st_ref)

    for hh in range(RET_HEADS):
        sl = slice(hh * LANES, (hh + 1) * LANES)
        q, k, v = q_ref[:, sl], k_ref[:, sl], v_ref[:, sl]
        sc = _dot_nt(q, k) * dec_ref[hh]
        inner = _dot(sc.astype(BF16), v)
        st = st_ref[hh]
        cross = _dot((q.astype(F32) * xi_ref[hh]).astype(BF16), st.astype(BF16))
        kz = (k.astype(F32) * zeta_ref[hh]).T.astype(BF16)
        st_ref[hh] = st * gch_ref[hh] + _dot(kz, v)
        o = inner + cross
        y = o * lax.rsqrt(jnp.mean(o * o, axis=-1, keepdims=True) + RMS_EPS) * g_ref[hh:hh + 1, :]
        o_ref[:, sl] = (_silu(gr_ref[:, sl]) * y).astype(BF16)


def _retention(qr, kr, vr, gr, g_ret, bsz, seq):
    t = bsz * seq
    c = RET_CHUNK
    n = seq // c
    log_g = jnp.log(1.0 - 2.0 ** (-5.0 - jnp.arange(RET_HEADS, dtype=F32)))
    i = jnp.arange(c, dtype=F32)
    diff = i[:, None] - i[None, :]
    causal = diff >= 0
    dec = jnp.where(causal, jnp.exp(log_g[:, None, None] * jnp.where(causal, diff, 0.0)), 0.0)
    xi = jnp.broadcast_to(jnp.exp(log_g[:, None] * (i + 1.0))[:, :, None], (RET_HEADS, c, LANES))
    zeta = jnp.broadcast_to(jnp.exp(log_g[:, None] * (c - 1.0 - i))[:, :, None], (RET_HEADS, c, LANES))
    gch = jnp.broadcast_to(jnp.exp(log_g * c)[:, None, None], (RET_HEADS, 1, LANES))
    w = RET_HEADS * RET_HEAD_DIM
    row = pl.BlockSpec((c, w), lambda b, j: (b * n + j, 0))
    full = lambda a: pl.BlockSpec(a.shape, lambda b, j: (0,) * a.ndim)
    return pl.pallas_call(
        _ret_kernel,
        grid=(bsz, n),
        in_specs=[row, row, row, row, full(dec), full(xi), full(zeta), full(gch), full(g_ret)],
        out_specs=row,
        out_shape=jax.ShapeDtypeStruct((t, w), BF16),
        scratch_shapes=[pltpu.VMEM((RET_HEADS, RET_HEAD_DIM, RET_HEAD_DIM), F32)],
        compiler_params=_cparams(("arbitrary", "arbitrary")),
        name="retention",
    )(qr, kr, vr, gr, dec, xi, zeta, gch, g_ret)


def _post_kernel(x_ref, onsa_ref, oret_ref, mod_ref, wo1_ref, wo2_ref, gffn_ref, wrt_ref, rb_ref,
                 wgs_ref, wus_ref, wds_ref, acc_ref, h2_ref, idx_ref, wt_ref):
    mix = _dot(onsa_ref[...], wo1_ref[...]) + _dot(oret_ref[...], wo2_ref[...])
    x1 = x_ref[...] + mod_ref[0, 2:3, :] * mix
    ms = jnp.mean(x1 * x1, axis=-1, keepdims=True)
    h2 = x1 * lax.rsqrt(ms + RMS_EPS) * gffn_ref[...] * (1.0 + mod_ref[0, 4:5, :]) + mod_ref[0, 3:4, :]
    _rows_to_tiles(h2_ref, h2)
    hb = h2.astype(BF16)
    hid = (_silu(_dot(hb, wgs_ref[...])) * _dot(hb, wus_ref[...])).astype(BF16)
    acc_ref[...] = x1 + mod_ref[0, 5:6, :] * _dot(hid, wds_ref[...])
    s = _sigmoid(_dot_nt(wrt_ref[...], h2, precision=HIGHEST))
    sb = s + rb_ref[...]
    per = N_EXPERTS // N_EXPERT_GROUPS
    ridx = lax.broadcasted_iota(I32, (per, 1), 0)
    blks, grp = [], []
    for gi in range(N_EXPERT_GROUPS):
        blk = sb[gi * per:(gi + 1) * per]
        m1 = jnp.max(blk, axis=0, keepdims=True)
        first = jnp.min(jnp.where(blk == m1, ridx, per), axis=0, keepdims=True)
        m2 = jnp.max(jnp.where(ridx == first, -jnp.inf, blk), axis=0, keepdims=True)
        blks.append(blk)
        grp.append(m1 + m2)
    masked = []
    for gi in range(N_EXPERT_GROUPS):
        rank = jnp.zeros_like(grp[gi])
        for gj in range(N_EXPERT_GROUPS):
            if gj < gi:
                rank = rank + (grp[gj] >= grp[gi]).astype(F32)
            elif gj > gi:
                rank = rank + (grp[gj] > grp[gi]).astype(F32)
        masked.append(jnp.where(rank < TOPK_GROUPS, blks[gi], NEG))
    val = jnp.concatenate(masked, axis=0)
    eidx = lax.broadcasted_iota(I32, (N_EXPERTS, 1), 0)
    ids, ws = [], []
    for _ in range(TOP_K):
        mx = jnp.max(val, axis=0, keepdims=True)
        emin = jnp.min(jnp.where(val == mx, eidx, N_EXPERTS), axis=0, keepdims=True)
        hit = eidx == emin
        ids.append(emin)
        ws.append(jnp.sum(jnp.where(hit, s, 0.0), axis=0, keepdims=True))
        val = jnp.where(hit, -jnp.inf, val)
    wsum = ws[0]
    for wk in ws[1:]:
        wsum = wsum + wk
    idx_ref[...] = jnp.concatenate(ids, axis=0)
    wt_ref[...] = jnp.concatenate(ws, axis=0) / wsum * ROUTED_SCALE


def _post(x2, onsa, oret, mod3, w_out, g_ffn, w_router, router_bias, wgs, wus, wds, seq):
    t, d = x2.shape
    tm = min(512, seq)
    hw = onsa.shape[1]
    wo1 = w_out[:hw].astype(BF16)
    wo2 = w_out[hw:].astype(BF16)
    wrt = w_router.T
    rb = jnp.broadcast_to(router_bias.reshape(N_EXPERTS, 1), (N_EXPERTS, tm))
    row = lambda n: pl.BlockSpec((tm, n), lambda i: (i, 0))
    full = lambda a: pl.BlockSpec(a.shape, lambda i: (0,) * a.ndim)
    col = pl.BlockSpec((TOP_K, tm), lambda i: (0, i))
    ops = (wo1, wo2, g_ffn, wrt, rb, wgs.astype(BF16), wus.astype(BF16), wds.astype(BF16))
    return pl.pallas_call(
        _post_kernel,
        grid=(t // tm,),
        in_specs=[row(d), row(hw), row(oret.shape[1]),
                  pl.BlockSpec((1, 6, d), lambda i: ((i * tm) // seq, 0, 0))] + [full(a) for a in ops],
        out_specs=[row(d), pl.BlockSpec((tm * SUBLANES, LANES), lambda i: (i, 0)), col, col],
        out_shape=[jax.ShapeDtypeStruct((t, d), F32), jax.ShapeDtypeStruct((t * SUBLANES, LANES), F32),
                   jax.ShapeDtypeStruct((TOP_K, t), I32), jax.ShapeDtypeStruct((TOP_K, t), F32)],
        compiler_params=_cparams(("arbitrary",)),
        name="post",
    )(x2, onsa, oret, mod3, *ops)


def _expert_select(idx_row, table, eidx):
    return jnp.sum(jnp.where(eidx == idx_row, table, 0.0), axis=0, keepdims=True)


def _rank_kernel(idx_ref, tri_ref, rank_ref, cnt_ref, carry):
    @pl.when(pl.program_id(0) == 0)
    def _():
        carry[...] = jnp.zeros_like(carry)

    idx = idx_ref[...]
    eidx = lax.broadcasted_iota(I32, (N_EXPERTS, 1), 0)
    member = jnp.zeros((N_EXPERTS, idx.shape[1]), F32)
    for k in range(TOP_K):
        member = member + (eidx == idx[k:k + 1, :]).astype(F32)
    before = _dot(member.astype(BF16), tri_ref[...]) + carry[:, 0:1]
    rank_ref[...] = jnp.concatenate(
        [_expert_select(idx[k:k + 1, :], before, eidx) for k in range(TOP_K)], axis=0).astype(I32)
    carry[...] = carry[...] + jnp.sum(member, axis=1, keepdims=True)
    cnt_ref[...] = carry[...]


def _dest_kernel(idx_ref, rank_ref, start_ref, pos_ref):
    idx = idx_ref[...]
    eidx = lax.broadcasted_iota(I32, (N_EXPERTS, 1), 0)
    start = start_ref[:, 0:1]
    base = jnp.concatenate([_expert_select(idx[k:k + 1, :], start, eidx) for k in range(TOP_K)], axis=0)
    pos_ref[...] = rank_ref[...] + base.astype(I32)


def _route_plan(idx_t, n_tok):
    tm = min(512, n_tok)
    tri = jnp.asarray(np.triu(np.ones((tm, tm), np.float32), 1), BF16)
    col = pl.BlockSpec((TOP_K, tm), lambda i: (0, i))
    rank, cnt = pl.pallas_call(
        _rank_kernel,
        grid=(n_tok // tm,),
        in_specs=[col, pl.BlockSpec((tm, tm), lambda i: (0, 0))],
        out_specs=[col, pl.BlockSpec((N_EXPERTS, LANES), lambda i: (0, 0))],
        out_shape=[jax.ShapeDtypeStruct((TOP_K, n_tok), I32), jax.ShapeDtypeStruct((N_EXPERTS, LANES), F32)],
        scratch_shapes=[pltpu.VMEM((N_EXPERTS, LANES), F32)],
        compiler_params=_cparams(("arbitrary",)),
        name="route_rank",
    )(idx_t, tri)
    counts = cnt[:, 0].astype(I32)
    padded = ((counts + MOE_TILE - 1) // MOE_TILE) * MOE_TILE
    ends = jnp.cumsum(padded)
    starts = ends - padded
    start_b = jnp.broadcast_to(starts.astype(F32)[:, None], (N_EXPERTS, LANES))
    pos = pl.pallas_call(
        _dest_kernel,
        grid=(n_tok // tm,),
        in_specs=[col, col, pl.BlockSpec((N_EXPERTS, LANES), lambda i: (0, 0))],
        out_specs=col,
        out_shape=jax.ShapeDtypeStruct((TOP_K, n_tok), I32),
        compiler_params=_cparams(("arbitrary",)),
        name="route_dest",
    )(idx_t, rank, start_b)
    n_tiles = TOP_K * n_tok // MOE_TILE + N_EXPERTS
    n_used = (ends[-1] // MOE_TILE).astype(I32)
    tile_start = jnp.minimum(jnp.arange(n_tiles, dtype=I32), n_used - 1) * MOE_TILE
    tile_e = jnp.minimum(jnp.sum((ends[None, :] <= tile_start[:, None]).astype(I32), axis=1), N_EXPERTS - 1)
    first = jnp.concatenate([jnp.ones((1,), I32), (tile_e[1:] != tile_e[:-1]).astype(I32)])
    eidx = jnp.arange(N_EXPERTS, dtype=I32)
    used = counts > 0
    rank = jnp.cumsum(used.astype(I32)) - 1

    def later(j):
        hit = used[None, :] & (rank[None, :] == rank[:, None] + j)
        return jnp.max(jnp.where(hit, eidx[None, :], -1), axis=1).astype(I32)

    plan = (tile_e, first, n_used.reshape(1), later(1), later(WEIGHT_SLOTS - 1), (rank % WEIGHT_SLOTS).astype(I32))
    slot = jnp.arange(MOE_TILE, dtype=I32)[None, :]
    rem = (counts % MOE_TILE)[:, None]
    n_rows = n_tiles * MOE_TILE
    pad_key = jnp.where((rem != 0) & (slot >= rem),
                        (starts + (counts // MOE_TILE) * MOE_TILE)[:, None] + slot, n_rows)
    keys = jnp.concatenate([pos.reshape(-1), pad_key.reshape(-1).astype(I32)]).astype(jnp.uint32)
    toks = jnp.concatenate([jnp.tile(jnp.arange(n_tok, dtype=jnp.uint32), TOP_K),
                            jnp.zeros((N_EXPERTS * MOE_TILE,), jnp.uint32)])
    assert (n_rows + 1) * n_tok <= 2 ** 32, "plan row and token id share one 32-bit sort word"
    packed = lax.sort(keys * jnp.uint32(n_tok) + toks)
    row_tok = (packed % jnp.uint32(n_tok)).astype(I32).reshape(n_tiles, 1, MOE_TILE)
    return pos, plan, row_tok


def _rows_to_tiles(ref, val):
    n = val.shape[0]
    for s in range(SUBLANES):
        ref[pl.ds(s, n, stride=SUBLANES), :] = val[:, s * LANES:(s + 1) * LANES]


def _tiles_to_rows(ref, n, *lead):
    return jnp.concatenate([ref[(*lead, pl.ds(s, n, stride=SUBLANES), slice(None))] for s in range(SUBLANES)],
                           axis=1)


def _tile_rows(ref, row, n=1):
    start = row * SUBLANES if isinstance(row, int) else pl.multiple_of(row * SUBLANES, SUBLANES)
    return ref.at[pl.ds(start, n * SUBLANES), :]


def _gather_row(hbuf, tok_ref, stage, j):
    t = tok_ref[0, 0, j]
    pair = hbuf[pl.ds(pl.multiple_of((t >> 1) * BF16_TILE_ROWS, BF16_TILE_ROWS), BF16_TILE_ROWS), :].astype(F32)
    start = j * SUBLANES if isinstance(j, int) else pl.multiple_of(j * SUBLANES, SUBLANES)
    stage[pl.ds(start, SUBLANES), :] = jnp.where((t & 1) == 1, pair[SUBLANES:], pair[:SUBLANES])


def _expert_kernel(tile_e, first, n_used, next_e, ahead_e, wslot, tokc_ref, tokn_ref, h2_hbm, wg_hbm, wu_hbm,
                   wd_hbm, y_ref, hbuf, conv, stage_a, stage_b, wgf, wuf, wdf, wgb, wub, wdb, hsem, sem):
    i = pl.program_id(0)
    n = n_used[0]

    def weight_copies(e):
        slot = wslot[e]
        return [pltpu.make_async_copy(src.at[e], dst.at[slot], sem.at[slot])
                for src, dst in ((wg_hbm, wgf), (wu_hbm, wuf), (wd_hbm, wdf))]

    @pl.when(i == 0)
    def _():
        e = tile_e[0]
        for _ in range(WEIGHT_SLOTS - 1):
            @pl.when(e >= 0)
            def _():
                for cp in weight_copies(e):
                    cp.start()
            e = jnp.where(e >= 0, next_e[jnp.maximum(e, 0)], -1)
        rows = conv.shape[1]
        n_conv = h2_hbm.shape[0] // rows

        def chunk_copy(c, slot):
            return pltpu.make_async_copy(h2_hbm.at[pl.ds(pl.multiple_of(c * rows, rows), rows), :], conv.at[slot],
                                         hsem.at[slot])

        chunk_copy(0, 0).start()

        def convert(c, carry):
            slot = c % 2
            chunk_copy(c, slot).wait()

            @pl.when(c + 1 < n_conv)
            def _():
                chunk_copy(c + 1, 1 - slot).start()

            hbuf[pl.ds(pl.multiple_of(c * rows, rows), rows), :] = conv[slot].astype(BF16)
            return carry
        lax.fori_loop(0, n_conv, convert, 0)

        def first_tile(jj, carry):
            for u in range(SUBLANES):
                _gather_row(hbuf, tokc_ref, stage_a, jj * SUBLANES + u)
            return carry
        lax.fori_loop(0, MOE_TILE // SUBLANES, first_tile, 0)

    @pl.when(i < n)
    def _():
        @pl.when(first[i] == 1)
        def _():
            e = tile_e[i]
            slot = wslot[e]
            for cp in weight_copies(e):
                cp.wait()

            @pl.when(ahead_e[e] >= 0)
            def _():
                for cp in weight_copies(ahead_e[e]):
                    cp.start()

            wgb[...] = wgf[slot].astype(BF16)
            wub[...] = wuf[slot].astype(BF16)
            wdb[...] = wdf[slot].astype(BF16)

        for par, (cur, nxt) in enumerate(((stage_a, stage_b), (stage_b, stage_a))):
            @pl.when(i % 2 == par)
            def _():
                for j in range(MOE_TILE):
                    _gather_row(hbuf, tokn_ref, nxt, j)
                xb = _tiles_to_rows(cur, MOE_TILE).astype(BF16)
                hid = (_silu(_dot(xb, wgb[...])) * _dot(xb, wub[...])).astype(BF16)
                _rows_to_tiles(y_ref, _dot(hid, wdb[...]))

    @pl.when(i >= n)
    def _():
        y_ref[...] = jnp.zeros_like(y_ref)


def _experts(h2, row_tok, plan, wg, wu, wd):
    n_tiles = row_tok.shape[0]
    d, de = wg.shape[1], wg.shape[2]
    blk = (MOE_TILE * SUBLANES, LANES)
    conv_rows = min(4096, h2.shape[0])
    assert h2.shape[0] % conv_rows == 0 and h2.shape[0] % BF16_TILE_ROWS == 0
    hbm = pl.BlockSpec(memory_space=pl.ANY)
    grid_spec = pltpu.PrefetchScalarGridSpec(
        num_scalar_prefetch=len(plan),
        grid=(n_tiles,),
        in_specs=[pl.BlockSpec((1, 1, MOE_TILE), lambda i, *p: (jnp.minimum(i, p[2][0] - 1), 0, 0),
                               memory_space=pltpu.SMEM),
                  pl.BlockSpec((1, 1, MOE_TILE), lambda i, *p: (jnp.minimum(i + 1, p[2][0] - 1), 0, 0),
                               memory_space=pltpu.SMEM), hbm, hbm, hbm, hbm],
        out_specs=pl.BlockSpec(blk, lambda i, *p: (i, 0)),
        scratch_shapes=[pltpu.VMEM(h2.shape, BF16), pltpu.VMEM((2, conv_rows, LANES), F32),
                        pltpu.VMEM(blk, F32), pltpu.VMEM(blk, F32),
                        pltpu.VMEM((WEIGHT_SLOTS, d, de), F32), pltpu.VMEM((WEIGHT_SLOTS, d, de), F32),
                        pltpu.VMEM((WEIGHT_SLOTS, de, d), F32),
                        pltpu.VMEM((d, de), BF16), pltpu.VMEM((d, de), BF16), pltpu.VMEM((de, d), BF16),
                        pltpu.SemaphoreType.DMA((2,)), pltpu.SemaphoreType.DMA((WEIGHT_SLOTS,))])
    return pl.pallas_call(
        _expert_kernel,
        grid_spec=grid_spec,
        out_shape=jax.ShapeDtypeStruct((n_tiles * MOE_TILE * SUBLANES, LANES), F32),
        compiler_params=pltpu.CompilerParams(dimension_semantics=("arbitrary",),
                                             vmem_limit_bytes=EXPERTS_VMEM_LIMIT),
        name="experts",
    )(*plan, row_tok, row_tok, h2, wg, wu, wd)


def _combine_kernel(posc_ref, posn_ref, ys_hbm, acc_ref, w_ref, mod_ref, g_ref, o_ref, buf, sem):
    i = pl.program_id(0)
    n = pl.num_programs(0)
    tc = acc_ref.shape[0]

    def start_row(pos_ref, slot, k, j, priority):
        pltpu.make_async_copy(_tile_rows(ys_hbm, pos_ref[0, k, j]), _tile_rows(buf.at[slot, k], j),
                              sem.at[slot]).start(priority=priority)

    @pl.when(i == 0)
    def _():
        for k in range(TOP_K):
            def body(jj, carry):
                for u in range(8):
                    start_row(posc_ref, 0, k, jj * 8 + u, u % 2)
                return carry
            lax.fori_loop(0, tc // 8, body, 0)

    @pl.when(i + 1 < n)
    def _():
        for k in range(TOP_K):
            for j in range(tc):
                start_row(posn_ref, (i + 1) % 2, k, j, j % 2)

    slot = i % 2
    for k in range(TOP_K):
        pltpu.make_async_copy(_tile_rows(ys_hbm, 0, tc), buf.at[slot, k], sem.at[slot]).wait()
    w = w_ref[...]
    routed = w[:, 0:1] * _tiles_to_rows(buf, tc, slot, 0)
    for k in range(1, TOP_K):
        routed = routed + w[:, k:k + 1] * _tiles_to_rows(buf, tc, slot, k)
    x2 = acc_ref[...] + mod_ref[0, 5:6, :] * routed
    o_ref[...] = x2 * lax.rsqrt(jnp.mean(x2 * x2, axis=-1, keepdims=True) + RMS_EPS) * g_ref[...]


def _tile_pos(pos_t, tc):
    return pos_t.reshape(TOP_K, pos_t.shape[1] // tc, tc).transpose(1, 0, 2)


def _combine(ys, pos3, acc0, w_tok, mod3, g_final, seq):
    t, d = acc0.shape
    tc = pos3.shape[2]
    nt = t // tc
    return pl.pallas_call(
        _combine_kernel,
        grid=(nt,),
        in_specs=[pl.BlockSpec((1, TOP_K, tc), lambda i: (i, 0, 0), memory_space=pltpu.SMEM),
                  pl.BlockSpec((1, TOP_K, tc), lambda i: (jnp.minimum(i + 1, nt - 1), 0, 0),
                               memory_space=pltpu.SMEM),
                  pl.BlockSpec(memory_space=pl.ANY),
                  pl.BlockSpec((tc, d), lambda i: (i, 0)),
                  pl.BlockSpec((tc, TOP_K), lambda i: (i, 0)),
                  pl.BlockSpec((1, 6, d), lambda i: ((i * tc) // seq, 0, 0)),
                  pl.BlockSpec((1, d), lambda i: (0, 0))],
        out_specs=pl.BlockSpec((tc, d), lambda i: (i, 0)),
        out_shape=jax.ShapeDtypeStruct((t, d), F32),
        scratch_shapes=[pltpu.VMEM((2, TOP_K, tc * SUBLANES, LANES), F32), pltpu.SemaphoreType.DMA((2,))],
        compiler_params=_cparams(("arbitrary",)),
        name="combine",
    )(pos3, pos3, ys, acc0, w_tok, mod3, g_final)


def kernel(x, c, positions, w_ada, b_ada, g_norm_mix, w_in, cmp_pos_k, cmp_w1_k, cmp_w2_k, cmp_pos_v,
           cmp_w1_v, cmp_w2_v, g_nsa_out, g_ret_out, w_out, g_norm_ffn, w_router, router_bias,
           w_gate_e, w_up_e, w_down_e, w_gate_s, w_up_s, w_down_s, g_norm_final):
    bsz, seq, d = x.shape
    assert d == SUBLANES * LANES, "MoE rows are moved as one (8, 128) tile each"
    t = bsz * seq
    x2 = x.reshape(t, d)
    cos_t, sin_t = _rope_tables(positions.reshape(t, 1).astype(F32))
    n_cmp_pad = seq // CMP_STRIDE

    def cmp_rows(tab):
        rows = tab[:, :LANES].reshape(bsz, seq, LANES)[:, CMP_BLOCK - 1::CMP_STRIDE]
        return jnp.pad(rows, ((0, 0), (0, n_cmp_pad - rows.shape[1]), (0, 0)))

    cos_c, sin_c = cmp_rows(cos_t), cmp_rows(sin_t)
    for l in range(w_in.shape[0]):
        mod3 = _ada(c, w_ada[l], b_ada[l]).reshape(bsz, 6, d)
        q, kc, vc, kk, gates, qr, kr, vr, gr, vsel_t, vwin_t = _inproj(
            x2, mod3, g_norm_mix[l].reshape(1, d), _arrange_w_in(w_in[l]), cos_t, sin_t, seq)
        kcmp, vcmp = _compress(kc, vc, cos_c, sin_c,
                               _arrange_cmp_weights(cmp_pos_k[l], cmp_w1_k[l], cmp_w2_k[l]),
                               _arrange_cmp_weights(cmp_pos_v[l], cmp_w1_v[l], cmp_w2_v[l]), bsz, seq)
        onsa = _nsa(q, kcmp, vcmp, kk, vsel_t, vwin_t, gates, g_nsa_out[l].reshape(1, -1), bsz, seq)
        oret = _retention(qr, kr, vr, gr, g_ret_out[l], bsz, seq)
        acc0, h2, idx_t, w_t = _post(x2, onsa, oret, mod3, w_out[l], g_norm_ffn[l].reshape(1, d),
                                     w_router[l], router_bias[l], w_gate_s[l], w_up_s[l], w_down_s[l], seq)
        pos_t, plan, row_tok = _route_plan(idx_t, t)
        ys = _experts(h2, row_tok, plan, w_gate_e[l], w_up_e[l], w_down_e[l])
        last = l == w_in.shape[0] - 1
        gfin = g_norm_final.reshape(1, d)
        x2 = _combine(ys, _tile_pos(pos_t, 128), acc0, w_t.T, mod3, gfin, seq)
        assert last, "final norm is fused into the combine stage; depth 1 only"
    return x2.reshape(bsz, seq, d)
```

```python
import functools
import math

import numpy as np
import jax
import jax.numpy as jnp
from jax import lax
from jax.experimental import pallas as pl
from jax.experimental.pallas import tpu as pltpu

F32 = jnp.float32
BF16 = jnp.bfloat16
I32 = jnp.int32
HIGHEST = lax.Precision.HIGHEST

LANES = 128
SUBLANES = 8
BF16_TILE_ROWS = 16
NSA_HEAD_DIM = 64
NSA_HEADS = 8
NSA_GROUPS = 2
NSA_Q_PER_KV = NSA_HEADS // NSA_GROUPS
CMP_BLOCK = 32
CMP_STRIDE = 16
CMP_HIDDEN = 128
SEL_BLOCK = 64
SEL_TOP = 8
WINDOW = 512
Q_BLOCK = 128
FORCE_BONUS = 1.0e4
RET_HEADS = 4
RET_HEAD_DIM = 128
RET_CHUNK = 128
ROPE_THETA = 10000.0
RMS_EPS = 1e-6
N_EXPERTS = 256
N_EXPERT_GROUPS = 8
TOPK_GROUPS = 4
TOP_K = 8
ROUTED_SCALE = 2.5
MOE_TILE = 256
WEIGHT_SLOTS = 3
SEL_CHUNK = 256
MASK_BIG = 2.0 ** 100
MASK_FLOOR = 2.0 ** 99
NEG = -1e30
VMEM_LIMIT = 48 * 1024 * 1024
EXPERTS_VMEM_LIMIT = 56 * 1024 * 1024


def _cparams(sem):
    return pltpu.CompilerParams(dimension_semantics=sem, vmem_limit_bytes=VMEM_LIMIT)


def _dot(a, b, **kw):
    return jnp.dot(a, b, preferred_element_type=F32, **kw)


def _dot_nt(a, b, **kw):
    return lax.dot_general(a, b, (((1,), (1,)), ((), ())), preferred_element_type=F32, **kw)


def _sigmoid(v):
    return 1.0 / (1.0 + jnp.exp(-v))


def _silu(v):
    return v * _sigmoid(v)


def _ada_kernel(c_ref, w_ref, b_ref, o_ref):
    o_ref[...] = _dot(_silu(c_ref[...]), w_ref[...], precision=HIGHEST) + b_ref[...]


def _ada(c, w, b):
    bsz, d = c.shape
    n = w.shape[1]
    tn = 1536
    cp = jnp.zeros((8, d), F32).at[:bsz].set(c)
    out = pl.pallas_call(
        _ada_kernel,
        grid=(n // tn,),
        in_specs=[pl.BlockSpec((8, d), lambda j: (0, 0)),
                  pl.BlockSpec((d, tn), lambda j: (0, j)),
                  pl.BlockSpec((1, tn), lambda j: (0, j))],
        out_specs=pl.BlockSpec((8, tn), lambda j: (0, j)),
        out_shape=jax.ShapeDtypeStruct((8, n), F32),
        compiler_params=_cparams(("arbitrary",)),
        name="ada",
    )(cp, w, b.reshape(1, n))
    return out[:bsz]


def _rope_table_kernel(pos_ref, inv_ref, sgn_ref, cos_ref, sin_ref):
    ang = pos_ref[...] * inv_ref[...]
    cos_ref[...] = jnp.cos(ang)
    sin_ref[...] = jnp.sin(ang) * sgn_ref[...]


def _rope_tables(posf):
    t = posf.shape[0]
    inv_n = ROPE_THETA ** (-jnp.arange(0, NSA_HEAD_DIM, 2, dtype=F32) / NSA_HEAD_DIM)
    inv_r = ROPE_THETA ** (-jnp.arange(0, RET_HEAD_DIM, 2, dtype=F32) / RET_HEAD_DIM)
    inv = jnp.concatenate([jnp.tile(inv_n, 4), jnp.tile(inv_r, 2)]).reshape(1, 2 * LANES)
    sgn_n = np.where((np.arange(LANES) % NSA_HEAD_DIM) < NSA_HEAD_DIM // 2, -1.0, 1.0)
    sgn_r = np.where(np.arange(LANES) < RET_HEAD_DIM // 2, -1.0, 1.0)
    sgn = jnp.asarray(np.concatenate([sgn_n, sgn_r]).reshape(1, 2 * LANES), F32)
    tm = min(t, 1024)
    return pl.pallas_call(
        _rope_table_kernel,
        grid=(t // tm,),
        in_specs=[pl.BlockSpec((tm, 1), lambda i: (i, 0)),
                  pl.BlockSpec((1, 2 * LANES), lambda i: (0, 0)),
                  pl.BlockSpec((1, 2 * LANES), lambda i: (0, 0))],
        out_specs=[pl.BlockSpec((tm, 2 * LANES), lambda i: (i, 0))] * 2,
        out_shape=[jax.ShapeDtypeStruct((t, 2 * LANES), F32)] * 2,
        compiler_params=_cparams(("arbitrary",)),
        name="rope_tables",
    )(posf, inv, sgn)


def _rope64(p, cos, sin, first_half):
    rot = jnp.where(first_half, pltpu.roll(p, 96, 1), pltpu.roll(p, 32, 1))
    return p * cos + rot * sin


def _rope128(p, cos, sin):
    return p * cos + pltpu.roll(p, 64, 1) * sin


_C_Q = 0
_C_KV = _C_Q + NSA_HEADS * LANES
_C_GATE = _C_KV + 6 * LANES
_C_RET = _C_GATE + LANES
_C_END = _C_RET + 4 * RET_HEADS * RET_HEAD_DIM


def _arrange_w_in(w_in):
    d = w_in.shape[0]
    nw = NSA_HEADS * NSA_HEAD_DIM
    q = w_in[:, :nw].reshape(d, NSA_HEADS, NSA_HEAD_DIM)
    z = jnp.zeros_like(q)
    grp = (jnp.arange(NSA_HEADS) // NSA_Q_PER_KV)[None, :, None]
    qpad = jnp.where(grp == 0, jnp.concatenate([q, z], -1), jnp.concatenate([z, q], -1))
    qpad = qpad.reshape(d, NSA_HEADS * LANES)
    kv = w_in[:, nw:nw + 6 * LANES]
    g0 = nw + 6 * LANES
    gate = jnp.pad(w_in[:, g0:g0 + 3 * NSA_HEADS], ((0, 0), (0, LANES - 3 * NSA_HEADS)))
    ret = w_in[:, g0 + 3 * NSA_HEADS:]
    return jnp.concatenate([qpad, kv, gate, ret], axis=1).astype(BF16)


def _inproj_kernel(x_ref, mod_ref, g_ref, w_ref, cos_ref, sin_ref,
                   q_ref, kc_ref, vc_ref, kk_ref, gate_ref, qr_ref, kr_ref, vr_ref, gr_ref, vst_ref, vwt_ref):
    x = x_ref[...]
    tm = x.shape[0]
    ms = jnp.mean(x * x, axis=-1, keepdims=True)
    y = x * lax.rsqrt(ms + RMS_EPS) * g_ref[...]
    h = y * (1.0 + mod_ref[0, 1:2, :]) + mod_ref[0, 0:1, :]
    hb = h.astype(BF16)
    cos_n, sin_n = cos_ref[:, 0:LANES], sin_ref[:, 0:LANES]
    cos_r, sin_r = cos_ref[:, LANES:], sin_ref[:, LANES:]
    lane = lax.broadcasted_iota(I32, (tm, LANES), 1)
    first_half = (lane % NSA_HEAD_DIM) < (NSA_HEAD_DIM // 2)
    scale_n = NSA_HEAD_DIM ** -0.5
    scale_r = RET_HEAD_DIM ** -0.5

    def proj(c0, n):
        return _dot(hb, w_ref[:, c0:c0 + n])

    for hh in range(NSA_HEADS):
        p = proj(_C_Q + hh * LANES, LANES)
        q_ref[:, hh * LANES:(hh + 1) * LANES] = (_rope64(p, cos_n, sin_n, first_half) * scale_n).astype(BF16)
    kv = proj(_C_KV, 6 * LANES)
    kc_ref[...] = kv[:, 0:LANES].astype(BF16)
    vc_ref[...] = kv[:, LANES:2 * LANES].astype(BF16)
    kk_ref[:, 0:LANES] = _rope64(kv[:, 2 * LANES:3 * LANES], cos_n, sin_n, first_half).astype(BF16)
    kk_ref[:, LANES:2 * LANES] = _rope64(kv[:, 4 * LANES:5 * LANES], cos_n, sin_n, first_half).astype(BF16)
    group0 = lane < NSA_HEAD_DIM
    for c0, vt_ref, chunk in ((3 * LANES, vst_ref, SEL_CHUNK), (5 * LANES, vwt_ref, Q_BLOCK)):
        v = kv[:, c0:c0 + LANES]
        for g, vg in enumerate((jnp.where(group0, v, 1.0), jnp.where(group0, 1.0, v))):
            for cc in range(tm // chunk):
                vt_ref[cc, g] = vg[cc * chunk:(cc + 1) * chunk].T.astype(BF16)
    gate_ref[...] = _sigmoid(proj(_C_GATE, LANES))
    rw = RET_HEADS * RET_HEAD_DIM
    for hh in range(RET_HEADS):
        sl = slice(hh * LANES, (hh + 1) * LANES)
        pq = proj(_C_RET + hh * LANES, LANES)
        qr_ref[:, sl] = _rope128(pq, cos_r, sin_r).astype(BF16)
        pk = proj(_C_RET + rw + hh * LANES, LANES)
        kr_ref[:, sl] = (_rope128(pk, cos_r, sin_r) * scale_r).astype(BF16)
    vr_ref[...] = proj(_C_RET + 2 * rw, rw).astype(BF16)
    gr_ref[...] = proj(_C_RET + 3 * rw, rw)


def _inproj(x2, mod3, g_mix, w_in_p, cos_t, sin_t, seq):
    t, d = x2.shape
    tm = min(512, seq)
    rw = RET_HEADS * RET_HEAD_DIM
    row = lambda n: pl.BlockSpec((tm, n), lambda i: (i, 0))
    outs = [(NSA_HEADS * LANES, BF16), (LANES, BF16), (LANES, BF16), (2 * LANES, BF16), (LANES, F32),
            (rw, BF16), (rw, BF16), (rw, BF16), (rw, F32)]
    vt_specs, vt_shapes = [], []
    for chunk in (SEL_CHUNK, Q_BLOCK):
        vt_specs.append(pl.BlockSpec((tm // chunk, NSA_GROUPS, LANES, chunk), lambda i: (i, 0, 0, 0)))
        vt_shapes.append(jax.ShapeDtypeStruct((t // chunk, NSA_GROUPS, LANES, chunk), BF16))
    return pl.pallas_call(
        _inproj_kernel,
        grid=(t // tm,),
        in_specs=[row(d),
                  pl.BlockSpec((1, 6, d), lambda i: ((i * tm) // seq, 0, 0)),
                  pl.BlockSpec((1, d), lambda i: (0, 0)),
                  pl.BlockSpec((d, _C_END), lambda i: (0, 0)),
                  row(2 * LANES), row(2 * LANES)],
        out_specs=[row(n) for n, _ in outs] + vt_specs,
        out_shape=[jax.ShapeDtypeStruct((t, n), dt) for n, dt in outs] + vt_shapes,
        compiler_params=_cparams(("arbitrary",)),
        name="inproj",
    )(x2, mod3, g_mix, w_in_p, cos_t, sin_t)


def _gelu_tanh(v):
    return v * (0.5 * (1.0 + jnp.tanh(math.sqrt(2.0 / math.pi) * (v + 0.044715 * (v * v * v)))))


def _compress_kernel(kc_ref, vc_ref, wkt_ref, wkb_ref, wk2_ref, pk_ref, wvt_ref, wvb_ref, wv2_ref, pv_ref,
                     cos_ref, sin_ref, ko_ref, vo_ref):
    def one(x_ref, wt_ref, wb_ref, w2_ref, p_ref):
        xx = x_ref[0]
        a = _dot(xx, wt_ref[...])
        b = _dot(xx, wb_ref[...])
        pb = p_ref[...].astype(BF16)
        bias = _dot(pb, wt_ref[...])[0:1] + _dot(pb, wb_ref[...])[1:2]
        hid = a + pltpu.roll(b, b.shape[0] - 1, 0) + bias
        return _dot(_gelu_tanh(hid).astype(BF16), w2_ref[...])

    k = one(kc_ref, wkt_ref, wkb_ref, wk2_ref, pk_ref)
    lane = lax.broadcasted_iota(I32, k.shape, 1)
    first_half = (lane % NSA_HEAD_DIM) < (NSA_HEAD_DIM // 2)
    ko_ref[0] = _rope64(k, cos_ref[0], sin_ref[0], first_half).astype(BF16)
    v = one(vc_ref, wvt_ref, wvb_ref, wv2_ref, pv_ref)
    group0 = lane < NSA_HEAD_DIM
    vo_ref[0, 0] = jnp.where(group0, v, 1.0).T.astype(BF16)
    vo_ref[0, 1] = jnp.where(group0, 1.0, v).T.astype(BF16)


def _arrange_cmp_weights(pos, w1, w2):
    half = CMP_BLOCK // 2
    dh = NSA_HEAD_DIM
    w1r = w1.reshape(CMP_BLOCK, dh, CMP_HIDDEN)

    def block(wpart):
        z = jnp.zeros_like(wpart)
        g0 = jnp.concatenate([wpart, z], axis=-1)
        g1 = jnp.concatenate([z, wpart], axis=-1)
        return jnp.stack([g0, g1], axis=1).reshape(half * 2 * dh, 2 * CMP_HIDDEN)

    wt, wb = block(w1r[:half]), block(w1r[half:])
    z2 = jnp.zeros_like(w2)
    w2b = jnp.concatenate([jnp.concatenate([w2, z2], 1), jnp.concatenate([z2, w2], 1)], 0)
    ptop = jnp.tile(pos[:half], (1, 2)).reshape(1, -1)
    pbot = jnp.tile(pos[half:], (1, 2)).reshape(1, -1)
    prow = jnp.concatenate([ptop, pbot, jnp.zeros((6, ptop.shape[1]), F32)], 0)
    return wt.astype(BF16), wb.astype(BF16), w2b.astype(BF16), prow


def _compress(kc, vc, cos_c, sin_c, kparams, vparams, bsz, seq):
    nchunk = seq // CMP_STRIDE
    width = CMP_STRIDE * LANES
    kc16 = kc.reshape(bsz, nchunk, width)
    vc16 = vc.reshape(bsz, nchunk, width)
    full = lambda a: pl.BlockSpec(a.shape, lambda b: (0,) * a.ndim)
    per_b = lambda n: pl.BlockSpec((1, nchunk, n), lambda b: (b, 0, 0))
    return pl.pallas_call(
        _compress_kernel,
        grid=(bsz,),
        in_specs=[per_b(width), per_b(width)] + [full(a) for a in kparams] + [full(a) for a in vparams]
                 + [per_b(LANES), per_b(LANES)],
        out_specs=[per_b(LANES), pl.BlockSpec((1, NSA_GROUPS, LANES, nchunk), lambda b: (b, 0, 0, 0))],
        out_shape=[jax.ShapeDtypeStruct((bsz, nchunk, LANES), BF16),
                   jax.ShapeDtypeStruct((bsz, NSA_GROUPS, LANES, nchunk), BF16)],
        compiler_params=_cparams(("arbitrary",)),
        name="compress",
    )(kc16, vc16, *kparams, *vparams, cos_c, sin_c)


def _softmax_chunk(m, acc, s, vt):
    m_new = jnp.maximum(m, jnp.max(s, axis=0, keepdims=True))
    e = jnp.exp(s - m_new).astype(BF16)
    acc = jnp.exp(m - m_new) * acc + _dot(vt, e)
    return m_new, acc


def _nsa_kernel(q_ref, kc_ref, vct_ref, ks_ref, kw_ref, vst_ref, vwt_ref, gate_ref,
                ovt_ref, exp_ref, g_ref, o_ref, *, seq):
    qb = pl.program_id(1)
    t0 = qb * Q_BLOCK
    cols = NSA_Q_PER_KV * Q_BLOCK
    n_cmp_pad = kc_ref.shape[1]
    n_sel = seq // SEL_BLOCK
    t_row = t0 + lax.broadcasted_iota(I32, (1, Q_BLOCK), 1)
    gates_t = gate_ref[...].T
    lane = lax.broadcasted_iota(I32, (Q_BLOCK, LANES), 1)
    key = lax.broadcasted_iota(I32, (Q_BLOCK, Q_BLOCK), 0)
    tok = lax.broadcasted_iota(I32, (Q_BLOCK, Q_BLOCK), 1)
    cend = lax.broadcasted_iota(I32, (n_cmp_pad, 1), 0) * CMP_STRIDE + (CMP_BLOCK - 1)
    bias_cmp = jnp.where(cend <= t_row, 0.0, -MASK_BIG)
    last_c = qb // (SEL_CHUNK // Q_BLOCK)
    kpos_last = last_c * SEL_CHUNK + lax.broadcasted_iota(I32, (SEL_CHUNK, 1), 0)
    bias_diag = jnp.where(kpos_last <= t_row, 0.0, -MASK_BIG)
    n_win = WINDOW // Q_BLOCK
    bias_win_first = jnp.where(key > tok, 0.0, -MASK_BIG)
    bias_win_last = jnp.where(key <= tok, 0.0, -MASK_BIG)
    init = (jnp.full((1, cols), -MASK_FLOOR, F32), jnp.zeros((LANES, cols), F32))
    tile_heads = lambda b: jnp.concatenate([b] * NSA_Q_PER_KV, axis=1)

    def finish(acc, g):
        ones_row = NSA_HEAD_DIM * (1 - g)
        return acc * (1.0 / jnp.maximum(acc[ones_row:ones_row + 1, :], 1e-20))

    q4s, o_cs, sel_bs = [], [], []
    for g in range(NSA_GROUPS):
        q4 = jnp.concatenate([q_ref[:, (NSA_Q_PER_KV * g + r) * LANES:(NSA_Q_PER_KV * g + r + 1) * LANES]
                              for r in range(NSA_Q_PER_KV)], axis=0)
        q4s.append(q4)
        s_c = _dot_nt(kc_ref[0], q4) + tile_heads(bias_cmp)
        e_c = jnp.exp(s_c - jnp.maximum(jnp.max(s_c, axis=0, keepdims=True), -MASK_FLOOR))
        p_c = e_c * (1.0 / jnp.maximum(jnp.sum(e_c, axis=0, keepdims=True), 1e-20))
        psum = p_c[:, 0:Q_BLOCK]
        for r in range(1, NSA_Q_PER_KV):
            psum = psum + p_c[:, r * Q_BLOCK:(r + 1) * Q_BLOCK]
        o_cs.append(finish(_dot(vct_ref[0, g], e_c.astype(BF16)), g))
        imp_t = _dot(ovt_ref[...], psum, precision=HIGHEST)
        nsp = imp_t.shape[0]
        jrow = lax.broadcasted_iota(I32, (nsp, 1), 0)
        cur = t_row // SEL_BLOCK
        valid = (jrow * SEL_BLOCK <= t_row) & (jrow < n_sel)
        forced = (jrow == 0) | (jrow == cur) | (jrow == cur - 1)
        val = jnp.where(valid, imp_t + jnp.where(forced, FORCE_BONUS, 0.0), -1.0)
        val = jnp.where(jrow < n_sel, val, -jnp.inf)
        sel_t = jnp.zeros((nsp, Q_BLOCK), F32)
        for _ in range(min(SEL_TOP, n_sel)):
            mx = jnp.max(val, axis=0, keepdims=True)
            jmin = jnp.min(jnp.where(val == mx, jrow, nsp), axis=0, keepdims=True)
            hit = jrow == jmin
            sel_t = jnp.where(hit, 1.0, sel_t)
            val = jnp.where(hit, -jnp.inf, val)
        sel_bs.append(jnp.concatenate([sel_t[:n_sel], jnp.ones((SUBLANES, Q_BLOCK), F32),
                                       jnp.zeros((LANES - n_sel - SUBLANES, Q_BLOCK), F32)], axis=0).astype(BF16))

    def sel_scores(c):
        keys = ks_ref[pl.ds(pl.multiple_of(c * SEL_CHUNK, SEL_CHUNK), SEL_CHUNK), :]
        expand = exp_ref[c]
        return tuple(_dot_nt(keys, q4s[g]) + tile_heads(_dot(expand, sel_bs[g])) for g in range(NSA_GROUPS))

    def sel_reduce(c, state, scores):
        out = []
        for g in range(NSA_GROUPS):
            out.extend(_softmax_chunk(state[2 * g], state[2 * g + 1], scores[g], vst_ref[c, g]))
        return tuple(out)

    def sel_step(c, carry):
        state, scores = carry
        return sel_reduce(c, state, scores), sel_scores(c + 1)

    state, scores = lax.fori_loop(0, last_c, sel_step, (init * NSA_GROUPS, sel_scores(0)))
    diag = tile_heads(bias_diag)
    state = sel_reduce(last_c, state, tuple(s + diag for s in scores))
    o_ss = [finish(state[2 * g + 1], g) for g in range(NSA_GROUPS)]
    s_ws = [[] for _ in range(NSA_GROUPS)]
    kbs = []
    for w in range(n_win + 1):
        wb = qb - n_win + w
        kb = jnp.maximum(wb, 0)
        kbs.append(kb)
        keys = kw_ref[pl.ds(pl.multiple_of(kb * Q_BLOCK, Q_BLOCK), Q_BLOCK), :]
        if w == n_win:
            bias = bias_win_last
        else:
            before_start = jnp.where(wb >= 0, 0.0, -MASK_BIG)
            bias = (bias_win_first + before_start) if w == 0 else jnp.full((Q_BLOCK, Q_BLOCK), before_start)
        for g in range(NSA_GROUPS):
            s_ws[g].append(_dot_nt(keys, q4s[g]) + tile_heads(bias))
    o_ws = []
    for g in range(NSA_GROUPS):
        m_w = jnp.full((1, cols), -MASK_FLOOR, F32)
        for s in s_ws[g]:
            m_w = jnp.maximum(m_w, jnp.max(s, axis=0, keepdims=True))
        acc = None
        for kb, s in zip(kbs, s_ws[g]):
            pv = _dot(vwt_ref[kb, g], jnp.exp(s - m_w).astype(BF16))
            acc = pv if acc is None else acc + pv
        o_ws.append(finish(acc, g))
    heads_out = []
    for g in range(NSA_GROUPS):
        o_c, o_s, o_w = o_cs[g], o_ss[g], o_ws[g]
        for r in range(NSA_Q_PER_KV):
            hh = NSA_Q_PER_KV * g + r
            cs = slice(r * Q_BLOCK, (r + 1) * Q_BLOCK)
            o = (gates_t[3 * hh:3 * hh + 1, :] * o_c[:, cs] + gates_t[3 * hh + 1:3 * hh + 2, :] * o_s[:, cs]
                 + gates_t[3 * hh + 2:3 * hh + 3, :] * o_w[:, cs]).T
            if (hh % 2) != g:
                o = pltpu.roll(o, NSA_HEAD_DIM, 1)
            heads_out.append(o)
    blocks = [jnp.where(lane < NSA_HEAD_DIM, heads_out[2 * i], heads_out[2 * i + 1])
              for i in range(NSA_HEADS // 2)]
    ss = sum(jnp.sum(b * b, axis=-1, keepdims=True) for b in blocks)
    inv = lax.rsqrt(ss / (NSA_HEADS * NSA_HEAD_DIM) + RMS_EPS)
    for i, b in enumerate(blocks):
        sl = slice(i * LANES, (i + 1) * LANES)
        o_ref[:, sl] = (b * inv * g_ref[:, sl]).astype(BF16)


def _nsa(q, kcmp, vcmp_t, kk, vsel_t, vwin_t, gates, g_nsa, bsz, seq):
    t = bsz * seq
    nq = seq // Q_BLOCK
    n_cmp_pad = seq // CMP_STRIDE
    n_cmp = (seq - CMP_BLOCK) // CMP_STRIDE + 1
    n_sel = seq // SEL_BLOCK
    assert n_sel % SUBLANES == 0 and n_sel + SUBLANES <= LANES, "selection mask needs a spare expansion row"
    nsp = n_sel
    cs = np.arange(n_cmp_pad) * CMP_STRIDE
    ss = np.arange(nsp) * SEL_BLOCK
    ov = ((cs[None, :] < ss[:, None] + SEL_BLOCK) & (cs[None, :] + CMP_BLOCK > ss[:, None])
          & (np.arange(n_cmp_pad)[None, :] < n_cmp) & (np.arange(nsp)[:, None] < n_sel))
    ovt = jnp.asarray(ov, F32)
    nch = seq // SEL_CHUNK
    kp = np.arange(seq).reshape(nch, SEL_CHUNK, 1)
    col = np.arange(LANES).reshape(1, 1, LANES)
    expand = jnp.asarray(np.where(kp // SEL_BLOCK == col, MASK_BIG, 0.0) + np.where(col == n_sel, -MASK_BIG, 0.0),
                         BF16)
    seqcol = lambda c: pl.BlockSpec((seq, LANES), lambda b, i: (b, c))
    per_b = lambda a: pl.BlockSpec((a.shape[0] // bsz,) + a.shape[1:], lambda b, i: (b,) + (0,) * (a.ndim - 1))
    return pl.pallas_call(
        functools.partial(_nsa_kernel, seq=seq),
        grid=(bsz, nq),
        in_specs=[pl.BlockSpec((Q_BLOCK, NSA_HEADS * LANES), lambda b, i: (b * nq + i, 0)),
                  per_b(kcmp), per_b(vcmp_t), seqcol(0), seqcol(1), per_b(vsel_t), per_b(vwin_t),
                  pl.BlockSpec((Q_BLOCK, LANES), lambda b, i: (b * nq + i, 0)),
                  pl.BlockSpec(ovt.shape, lambda b, i: (0, 0)),
                  pl.BlockSpec(expand.shape, lambda b, i: (0, 0, 0)),
                  pl.BlockSpec((1, NSA_HEADS * NSA_HEAD_DIM), lambda b, i: (0, 0))],
        out_specs=pl.BlockSpec((Q_BLOCK, NSA_HEADS * NSA_HEAD_DIM), lambda b, i: (b * nq + i, 0)),
        out_shape=jax.ShapeDtypeStruct((t, NSA_HEADS * NSA_HEAD_DIM), BF16),
        compiler_params=_cparams(("arbitrary", "arbitrary")),
        name="nsa",
    )(q, kcmp, vcmp_t, kk, kk, vsel_t, vwin_t, gates, ovt, expand, g_nsa)


def _ret_kernel(q_ref, k_ref, v_ref, gr_ref, dec_ref, xi_ref, zeta_ref, gch_ref, g_ref, o_ref, st_ref):
    @pl.when(pl.program_id(1) == 0)
    def _():
        st_ref[...] = jnp.zeros_like(st_ref)

    for cc in range(q_ref.shape[0] // RET_CHUNK):
        rs = slice(cc * RET_CHUNK, (cc + 1) * RET_CHUNK)
        for hh in range(RET_HEADS):
            sl = slice(hh * LANES, (hh + 1) * LANES)
            q, k, v = q_ref[rs, sl], k_ref[rs, sl], v_ref[rs, sl]
            sc = _dot_nt(q, k) * dec_ref[hh]
            inner = _dot(sc.astype(BF16), v)
            st = st_ref[hh]
            cross = _dot((q.astype(F32) * xi_ref[hh]).astype(BF16), st.astype(BF16))
            kz = (k.astype(F32) * zeta_ref[hh]).T.astype(BF16)
            st_ref[hh] = st * gch_ref[hh] + _dot(kz, v)
            o = inner + cross
            y = o * lax.rsqrt(jnp.mean(o * o, axis=-1, keepdims=True) + RMS_EPS) * g_ref[hh:hh + 1, :]
            o_ref[rs, sl] = (_silu(gr_ref[rs, sl]) * y).astype(BF16)


def _retention(qr, kr, vr, gr, g_ret, bsz, seq):
    t = bsz * seq
    c = RET_CHUNK
    n = seq // c
    log_g = jnp.log(1.0 - 2.0 ** (-5.0 - jnp.arange(RET_HEADS, dtype=F32)))
    i = jnp.arange(c, dtype=F32)
    diff = i[:, None] - i[None, :]
    causal = diff >= 0
    dec = jnp.where(causal, jnp.exp(log_g[:, None, None] * jnp.where(causal, diff, 0.0)), 0.0)
    xi = jnp.broadcast_to(jnp.exp(log_g[:, None] * (i + 1.0))[:, :, None], (RET_HEADS, c, LANES))
    zeta = jnp.broadcast_to(jnp.exp(log_g[:, None] * (c - 1.0 - i))[:, :, None], (RET_HEADS, c, LANES))
    gch = jnp.broadcast_to(jnp.exp(log_g * c)[:, None, None], (RET_HEADS, 1, LANES))
    w = RET_HEADS * RET_HEAD_DIM
    per_step = 4 if n % 4 == 0 else 1
    steps = n // per_step
    row = pl.BlockSpec((c * per_step, w), lambda b, j: (b * steps + j, 0))
    full = lambda a: pl.BlockSpec(a.shape, lambda b, j: (0,) * a.ndim)
    return pl.pallas_call(
        _ret_kernel,
        grid=(bsz, steps),
        in_specs=[row, row, row, row, full(dec), full(xi), full(zeta), full(gch), full(g_ret)],
        out_specs=row,
        out_shape=jax.ShapeDtypeStruct((t, w), BF16),
        scratch_shapes=[pltpu.VMEM((RET_HEADS, RET_HEAD_DIM, RET_HEAD_DIM), F32)],
        compiler_params=_cparams(("arbitrary", "arbitrary")),
        name="retention",
    )(qr, kr, vr, gr, dec, xi, zeta, gch, g_ret)


def _post_kernel(x_ref, onsa_ref, oret_ref, mod_ref, wo1_ref, wo2_ref, gffn_ref, wrt_ref, rb_ref,
                 wgs_ref, wus_ref, wds_ref, acc_ref, h2_ref, idx_ref, wt_ref):
    mix = _dot(onsa_ref[...], wo1_ref[...]) + _dot(oret_ref[...], wo2_ref[...])
    x1 = x_ref[...] + mod_ref[0, 2:3, :] * mix
    ms = jnp.mean(x1 * x1, axis=-1, keepdims=True)
    h2 = x1 * lax.rsqrt(ms + RMS_EPS) * gffn_ref[...] * (1.0 + mod_ref[0, 4:5, :]) + mod_ref[0, 3:4, :]
    _rows_to_tiles(h2_ref, h2)
    hb = h2.astype(BF16)
    hid = (_silu(_dot(hb, wgs_ref[...])) * _dot(hb, wus_ref[...])).astype(BF16)
    acc_ref[...] = x1 + mod_ref[0, 5:6, :] * _dot(hid, wds_ref[...])
    s = _sigmoid(_dot_nt(wrt_ref[...], h2, precision=HIGHEST))
    sb = s + rb_ref[...]
    per = N_EXPERTS // N_EXPERT_GROUPS
    ridx = lax.broadcasted_iota(I32, (per, 1), 0)
    blks, grp = [], []
    for gi in range(N_EXPERT_GROUPS):
        blk = sb[gi * per:(gi + 1) * per]
        m1 = jnp.max(blk, axis=0, keepdims=True)
        first = jnp.min(jnp.where(blk == m1, ridx, per), axis=0, keepdims=True)
        m2 = jnp.max(jnp.where(ridx == first, -jnp.inf, blk), axis=0, keepdims=True)
        blks.append(blk)
        grp.append(m1 + m2)
    masked = []
    for gi in range(N_EXPERT_GROUPS):
        rank = jnp.zeros_like(grp[gi])
        for gj in range(N_EXPERT_GROUPS):
            if gj < gi:
                rank = rank + (grp[gj] >= grp[gi]).astype(F32)
            elif gj > gi:
                rank = rank + (grp[gj] > grp[gi]).astype(F32)
        masked.append(jnp.where(rank < TOPK_GROUPS, blks[gi], NEG))
    val = jnp.concatenate(masked, axis=0)
    eidx = lax.broadcasted_iota(I32, (N_EXPERTS, 1), 0)
    ids, ws = [], []
    for _ in range(TOP_K):
        mx = jnp.max(val, axis=0, keepdims=True)
        emin = jnp.min(jnp.where(val == mx, eidx, N_EXPERTS), axis=0, keepdims=True)
        hit = eidx == emin
        ids.append(emin)
        ws.append(jnp.sum(jnp.where(hit, s, 0.0), axis=0, keepdims=True))
        val = jnp.where(hit, -jnp.inf, val)
    wsum = ws[0]
    for wk in ws[1:]:
        wsum = wsum + wk
    idx_ref[...] = jnp.concatenate(ids, axis=0)
    wt_ref[...] = jnp.concatenate(ws, axis=0) / wsum * ROUTED_SCALE


def _post(x2, onsa, oret, mod3, w_out, g_ffn, w_router, router_bias, wgs, wus, wds, seq):
    t, d = x2.shape
    tm = min(512, seq)
    hw = onsa.shape[1]
    wo1 = w_out[:hw].astype(BF16)
    wo2 = w_out[hw:].astype(BF16)
    wrt = w_router.T
    rb = jnp.broadcast_to(router_bias.reshape(N_EXPERTS, 1), (N_EXPERTS, tm))
    row = lambda n: pl.BlockSpec((tm, n), lambda i: (i, 0))
    full = lambda a: pl.BlockSpec(a.shape, lambda i: (0,) * a.ndim)
    col = pl.BlockSpec((TOP_K, tm), lambda i: (0, i))
    ops = (wo1, wo2, g_ffn, wrt, rb, wgs.astype(BF16), wus.astype(BF16), wds.astype(BF16))
    return pl.pallas_call(
        _post_kernel,
        grid=(t // tm,),
        in_specs=[row(d), row(hw), row(oret.shape[1]),
                  pl.BlockSpec((1, 6, d), lambda i: ((i * tm) // seq, 0, 0))] + [full(a) for a in ops],
        out_specs=[row(d), pl.BlockSpec((tm * SUBLANES, LANES), lambda i: (i, 0)), col, col],
        out_shape=[jax.ShapeDtypeStruct((t, d), F32), jax.ShapeDtypeStruct((t * SUBLANES, LANES), F32),
                   jax.ShapeDtypeStruct((TOP_K, t), I32), jax.ShapeDtypeStruct((TOP_K, t), F32)],
        compiler_params=_cparams(("arbitrary",)),
        name="post",
    )(x2, onsa, oret, mod3, *ops)


def _expert_select(idx_row, table, eidx):
    return jnp.sum(jnp.where(eidx == idx_row, table, 0.0), axis=0, keepdims=True)


def _rank_kernel(idx_ref, tri_ref, rank_ref, cnt_ref, carry):
    @pl.when(pl.program_id(0) == 0)
    def _():
        carry[...] = jnp.zeros_like(carry)

    idx = idx_ref[...]
    eidx = lax.broadcasted_iota(I32, (N_EXPERTS, 1), 0)
    member = jnp.zeros((N_EXPERTS, idx.shape[1]), F32)
    for k in range(TOP_K):
        member = member + (eidx == idx[k:k + 1, :]).astype(F32)
    before = _dot(member.astype(BF16), tri_ref[...]) + carry[:, 0:1]
    rank_ref[...] = jnp.concatenate(
        [_expert_select(idx[k:k + 1, :], before, eidx) for k in range(TOP_K)], axis=0).astype(I32)
    carry[...] = carry[...] + jnp.sum(member, axis=1, keepdims=True)
    cnt_ref[...] = carry[...]


def _dest_kernel(idx_ref, rank_ref, start_ref, pos_ref):
    idx = idx_ref[...]
    eidx = lax.broadcasted_iota(I32, (N_EXPERTS, 1), 0)
    start = start_ref[:, 0:1]
    base = jnp.concatenate([_expert_select(idx[k:k + 1, :], start, eidx) for k in range(TOP_K)], axis=0)
    pos_ref[...] = rank_ref[...] + base.astype(I32)


def _route_plan(idx_t, n_tok):
    tm = min(512, n_tok)
    tri = jnp.asarray(np.triu(np.ones((tm, tm), np.float32), 1), BF16)
    col = pl.BlockSpec((TOP_K, tm), lambda i: (0, i))
    rank, cnt = pl.pallas_call(
        _rank_kernel,
        grid=(n_tok // tm,),
        in_specs=[col, pl.BlockSpec((tm, tm), lambda i: (0, 0))],
        out_specs=[col, pl.BlockSpec((N_EXPERTS, LANES), lambda i: (0, 0))],
        out_shape=[jax.ShapeDtypeStruct((TOP_K, n_tok), I32), jax.ShapeDtypeStruct((N_EXPERTS, LANES), F32)],
        scratch_shapes=[pltpu.VMEM((N_EXPERTS, LANES), F32)],
        compiler_params=_cparams(("arbitrary",)),
        name="route_rank",
    )(idx_t, tri)
    counts = cnt[:, 0].astype(I32)
    padded = ((counts + MOE_TILE - 1) // MOE_TILE) * MOE_TILE
    ends = jnp.cumsum(padded)
    starts = ends - padded
    start_b = jnp.broadcast_to(starts.astype(F32)[:, None], (N_EXPERTS, LANES))
    pos = pl.pallas_call(
        _dest_kernel,
        grid=(n_tok // tm,),
        in_specs=[col, col, pl.BlockSpec((N_EXPERTS, LANES), lambda i: (0, 0))],
        out_specs=col,
        out_shape=jax.ShapeDtypeStruct((TOP_K, n_tok), I32),
        compiler_params=_cparams(("arbitrary",)),
        name="route_dest",
    )(idx_t, rank, start_b)
    n_tiles = TOP_K * n_tok // MOE_TILE + N_EXPERTS
    n_used = (ends[-1] // MOE_TILE).astype(I32)
    tile_start = jnp.minimum(jnp.arange(n_tiles, dtype=I32), n_used - 1) * MOE_TILE
    tile_e = jnp.minimum(jnp.sum((ends[None, :] <= tile_start[:, None]).astype(I32), axis=1), N_EXPERTS - 1)
    first = jnp.concatenate([jnp.ones((1,), I32), (tile_e[1:] != tile_e[:-1]).astype(I32)])
    eidx = jnp.arange(N_EXPERTS, dtype=I32)
    used = counts > 0
    rank = jnp.cumsum(used.astype(I32)) - 1

    def later(j):
        hit = used[None, :] & (rank[None, :] == rank[:, None] + j)
        return jnp.max(jnp.where(hit, eidx[None, :], -1), axis=1).astype(I32)

    plan = (tile_e, first, n_used.reshape(1), later(1), later(WEIGHT_SLOTS - 1), (rank % WEIGHT_SLOTS).astype(I32))
    slot = jnp.arange(MOE_TILE, dtype=I32)[None, :]
    rem = (counts % MOE_TILE)[:, None]
    n_rows = n_tiles * MOE_TILE
    pad_key = jnp.where((rem != 0) & (slot >= rem),
                        (starts + (counts // MOE_TILE) * MOE_TILE)[:, None] + slot, n_rows)
    keys = jnp.concatenate([pos.reshape(-1), pad_key.reshape(-1).astype(I32)]).astype(jnp.uint32)
    toks = jnp.concatenate([jnp.tile(jnp.arange(n_tok, dtype=jnp.uint32), TOP_K),
                            jnp.zeros((N_EXPERTS * MOE_TILE,), jnp.uint32)])
    assert (n_rows + 1) * n_tok <= 2 ** 32, "plan row and token id share one 32-bit sort word"
    packed = lax.sort(keys * jnp.uint32(n_tok) + toks)
    row_tok = (packed % jnp.uint32(n_tok)).astype(I32).reshape(n_tiles, 1, MOE_TILE)
    return pos, plan, row_tok


def _rows_to_tiles(ref, val):
    n = val.shape[0]
    for s in range(SUBLANES):
        ref[pl.ds(s, n, stride=SUBLANES), :] = val[:, s * LANES:(s + 1) * LANES]


def _tiles_to_rows(ref, n, *lead):
    return jnp.concatenate([ref[(*lead, pl.ds(s, n, stride=SUBLANES), slice(None))] for s in range(SUBLANES)],
                           axis=1)


def _tile_rows(ref, row, n=1):
    start = row * SUBLANES if isinstance(row, int) else pl.multiple_of(row * SUBLANES, SUBLANES)
    return ref.at[pl.ds(start, n * SUBLANES), :]


def _gather_row(hbuf, tok_ref, stage, j):
    t = tok_ref[0, 0, j]
    pair = hbuf[pl.ds(pl.multiple_of((t >> 1) * BF16_TILE_ROWS, BF16_TILE_ROWS), BF16_TILE_ROWS), :].astype(F32)
    start = j * SUBLANES if isinstance(j, int) else pl.multiple_of(j * SUBLANES, SUBLANES)
    stage[pl.ds(start, SUBLANES), :] = jnp.where((t & 1) == 1, pair[SUBLANES:], pair[:SUBLANES])


def _expert_kernel(tile_e, first, n_used, next_e, ahead_e, wslot, tokc_ref, tokn_ref, h2_hbm, wg_hbm, wu_hbm,
                   wd_hbm, y_ref, hbuf, conv, stage_a, stage_b, wgf, wuf, wdf, wgb, wub, wdb, hsem, sem):
    i = pl.program_id(0)
    n = n_used[0]

    def weight_copies(e):
        slot = wslot[e]
        return [pltpu.make_async_copy(src.at[e], dst.at[slot], sem.at[slot])
                for src, dst in ((wg_hbm, wgf), (wu_hbm, wuf), (wd_hbm, wdf))]

    @pl.when(i == 0)
    def _():
        e = tile_e[0]
        for _ in range(WEIGHT_SLOTS - 1):
            @pl.when(e >= 0)
            def _():
                for cp in weight_copies(e):
                    cp.start(priority=1)
            e = jnp.where(e >= 0, next_e[jnp.maximum(e, 0)], -1)
        rows = conv.shape[1]
        n_conv = h2_hbm.shape[0] // rows

        def chunk_copy(c, slot):
            return pltpu.make_async_copy(h2_hbm.at[pl.ds(pl.multiple_of(c * rows, rows), rows), :], conv.at[slot],
                                         hsem.at[slot])

        chunk_copy(0, 0).start()

        def convert(c, carry):
            slot = c % 2
            chunk_copy(c, slot).wait()

            @pl.when(c + 1 < n_conv)
            def _():
                chunk_copy(c + 1, 1 - slot).start()

            hbuf[pl.ds(pl.multiple_of(c * rows, rows), rows), :] = conv[slot].astype(BF16)
            return carry
        lax.fori_loop(0, n_conv, convert, 0)

        def first_tile(jj, carry):
            for u in range(SUBLANES):
                _gather_row(hbuf, tokc_ref, stage_a, jj * SUBLANES + u)
            return carry
        lax.fori_loop(0, MOE_TILE // SUBLANES, first_tile, 0)

    @pl.when(i < n)
    def _():
        @pl.when(first[i] == 1)
        def _():
            e = tile_e[i]
            slot = wslot[e]
            for cp in weight_copies(e):
                cp.wait()

            @pl.when(ahead_e[e] >= 0)
            def _():
                for cp in weight_copies(ahead_e[e]):
                    cp.start(priority=1)

            wgb[...] = wgf[slot].astype(BF16)
            wub[...] = wuf[slot].astype(BF16)
            wdb[...] = wdf[slot].astype(BF16)

        for par, (cur, nxt) in enumerate(((stage_a, stage_b), (stage_b, stage_a))):
            @pl.when(i % 2 == par)
            def _():
                for j in range(MOE_TILE):
                    _gather_row(hbuf, tokn_ref, nxt, j)
                xb = _tiles_to_rows(cur, MOE_TILE).astype(BF16)
                hid = (_silu(_dot(xb, wgb[...])) * _dot(xb, wub[...])).astype(BF16)
                _rows_to_tiles(y_ref, _dot(hid, wdb[...]))

    @pl.when(i >= n)
    def _():
        y_ref[...] = jnp.zeros_like(y_ref)


def _experts(h2, row_tok, plan, wg, wu, wd):
    n_tiles = row_tok.shape[0]
    d, de = wg.shape[1], wg.shape[2]
    blk = (MOE_TILE * SUBLANES, LANES)
    conv_rows = min(4096, h2.shape[0])
    assert h2.shape[0] % conv_rows == 0 and h2.shape[0] % BF16_TILE_ROWS == 0
    hbm = pl.BlockSpec(memory_space=pl.ANY)
    grid_spec = pltpu.PrefetchScalarGridSpec(
        num_scalar_prefetch=len(plan),
        grid=(n_tiles,),
        in_specs=[pl.BlockSpec((1, 1, MOE_TILE), lambda i, *p: (jnp.minimum(i, p[2][0] - 1), 0, 0),
                               memory_space=pltpu.SMEM),
                  pl.BlockSpec((1, 1, MOE_TILE), lambda i, *p: (jnp.minimum(i + 1, p[2][0] - 1), 0, 0),
                               memory_space=pltpu.SMEM), hbm, hbm, hbm, hbm],
        out_specs=pl.BlockSpec(blk, lambda i, *p: (i, 0)),
        scratch_shapes=[pltpu.VMEM(h2.shape, BF16), pltpu.VMEM((2, conv_rows, LANES), F32),
                        pltpu.VMEM(blk, F32), pltpu.VMEM(blk, F32),
                        pltpu.VMEM((WEIGHT_SLOTS, d, de), F32), pltpu.VMEM((WEIGHT_SLOTS, d, de), F32),
                        pltpu.VMEM((WEIGHT_SLOTS, de, d), F32),
                        pltpu.VMEM((d, de), BF16), pltpu.VMEM((d, de), BF16), pltpu.VMEM((de, d), BF16),
                        pltpu.SemaphoreType.DMA((2,)), pltpu.SemaphoreType.DMA((WEIGHT_SLOTS,))])
    return pl.pallas_call(
        _expert_kernel,
        grid_spec=grid_spec,
        out_shape=jax.ShapeDtypeStruct((n_tiles * MOE_TILE * SUBLANES, LANES), F32),
        compiler_params=pltpu.CompilerParams(dimension_semantics=("arbitrary",),
                                             vmem_limit_bytes=EXPERTS_VMEM_LIMIT),
        name="experts",
    )(*plan, row_tok, row_tok, h2, wg, wu, wd)


def _combine_kernel(posc_ref, posn_ref, ys_hbm, acc_ref, w_ref, mod_ref, g_ref, o_ref, buf, sem):
    i = pl.program_id(0)
    n = pl.num_programs(0)
    tc = acc_ref.shape[0]

    def start_row(pos_ref, slot, k, j, priority):
        pltpu.make_async_copy(_tile_rows(ys_hbm, pos_ref[0, k, j]), _tile_rows(buf.at[slot, k], j),
                              sem.at[slot]).start(priority=priority)

    @pl.when(i == 0)
    def _():
        for k in range(TOP_K):
            def body(jj, carry):
                for u in range(8):
                    start_row(posc_ref, 0, k, jj * 8 + u, u % 2)
                return carry
            lax.fori_loop(0, tc // 8, body, 0)

    @pl.when(i + 1 < n)
    def _():
        for k in range(TOP_K):
            for j in range(tc):
                start_row(posn_ref, (i + 1) % 2, k, j, j % 2)

    slot = i % 2
    for k in range(TOP_K):
        pltpu.make_async_copy(_tile_rows(ys_hbm, 0, tc), buf.at[slot, k], sem.at[slot]).wait()
    w = w_ref[...]
    routed = w[:, 0:1] * _tiles_to_rows(buf, tc, slot, 0)
    for k in range(1, TOP_K):
        routed = routed + w[:, k:k + 1] * _tiles_to_rows(buf, tc, slot, k)
    x2 = acc_ref[...] + mod_ref[0, 5:6, :] * routed
    o_ref[...] = x2 * lax.rsqrt(jnp.mean(x2 * x2, axis=-1, keepdims=True) + RMS_EPS) * g_ref[...]


def _tile_pos(pos_t, tc):
    return pos_t.reshape(TOP_K, pos_t.shape[1] // tc, tc).transpose(1, 0, 2)


def _combine(ys, pos3, acc0, w_tok, mod3, g_final, seq):
    t, d = acc0.shape
    tc = pos3.shape[2]
    nt = t // tc
    return pl.pallas_call(
        _combine_kernel,
        grid=(nt,),
        in_specs=[pl.BlockSpec((1, TOP_K, tc), lambda i: (i, 0, 0), memory_space=pltpu.SMEM),
                  pl.BlockSpec((1, TOP_K, tc), lambda i: (jnp.minimum(i + 1, nt - 1), 0, 0),
                               memory_space=pltpu.SMEM),
                  pl.BlockSpec(memory_space=pl.ANY),
                  pl.BlockSpec((tc, d), lambda i: (i, 0)),
                  pl.BlockSpec((tc, TOP_K), lambda i: (i, 0)),
                  pl.BlockSpec((1, 6, d), lambda i: ((i * tc) // seq, 0, 0)),
                  pl.BlockSpec((1, d), lambda i: (0, 0))],
        out_specs=pl.BlockSpec((tc, d), lambda i: (i, 0)),
        out_shape=jax.ShapeDtypeStruct((t, d), F32),
        scratch_shapes=[pltpu.VMEM((2, TOP_K, tc * SUBLANES, LANES), F32), pltpu.SemaphoreType.DMA((2,))],
        compiler_params=_cparams(("arbitrary",)),
        name="combine",
    )(pos3, pos3, ys, acc0, w_tok, mod3, g_final)


def kernel(x, c, positions, w_ada, b_ada, g_norm_mix, w_in, cmp_pos_k, cmp_w1_k, cmp_w2_k, cmp_pos_v,
           cmp_w1_v, cmp_w2_v, g_nsa_out, g_ret_out, w_out, g_norm_ffn, w_router, router_bias,
           w_gate_e, w_up_e, w_down_e, w_gate_s, w_up_s, w_down_s, g_norm_final):
    bsz, seq, d = x.shape
    assert d == SUBLANES * LANES, "MoE rows are moved as one (8, 128) tile each"
    t = bsz * seq
    x2 = x.reshape(t, d)
    cos_t, sin_t = _rope_tables(positions.reshape(t, 1).astype(F32))
    n_cmp_pad = seq // CMP_STRIDE

    def cmp_rows(tab):
        rows = tab[:, :LANES].reshape(bsz, seq, LANES)[:, CMP_BLOCK - 1::CMP_STRIDE]
        return jnp.pad(rows, ((0, 0), (0, n_cmp_pad - rows.shape[1]), (0, 0)))

    cos_c, sin_c = cmp_rows(cos_t), cmp_rows(sin_t)
    for l in range(w_in.shape[0]):
        mod3 = _ada(c, w_ada[l], b_ada[l]).reshape(bsz, 6, d)
        q, kc, vc, kk, gates, qr, kr, vr, gr, vsel_t, vwin_t = _inproj(
            x2, mod3, g_norm_mix[l].reshape(1, d), _arrange_w_in(w_in[l]), cos_t, sin_t, seq)
        kcmp, vcmp = _compress(kc, vc, cos_c, sin_c,
                               _arrange_cmp_weights(cmp_pos_k[l], cmp_w1_k[l], cmp_w2_k[l]),
                               _arrange_cmp_weights(cmp_pos_v[l], cmp_w1_v[l], cmp_w2_v[l]), bsz, seq)
        onsa = _nsa(q, kcmp, vcmp, kk, vsel_t, vwin_t, gates, g_nsa_out[l].reshape(1, -1), bsz, seq)
        oret = _retention(qr, kr, vr, gr, g_ret_out[l], bsz, seq)
        acc0, h2, idx_t, w_t = _post(x2, onsa, oret, mod3, w_out[l], g_norm_ffn[l].reshape(1, d),
                                     w_router[l], router_bias[l], w_gate_s[l], w_up_s[l], w_down_s[l], seq)
        pos_t, plan, row_tok = _route_plan(idx_t, t)
        ys = _experts(h2, row_tok, plan, w_gate_e[l], w_up_e[l], w_down_e[l])
        last = l == w_in.shape[0] - 1
        gfin = g_norm_final.reshape(1, d)
        x2 = _combine(ys, _tile_pos(pos_t, 128), acc0, w_t.T, mod3, gfin, seq)
        assert last, "final norm is fused into the combine stage; depth 1 only"
    return x2.reshape(bsz, seq, d)
```

```python
import functools
import math

import numpy as np
import jax
import jax.numpy as jnp
from jax import lax
from jax.experimental import pallas as pl
from jax.experimental.pallas import tpu as pltpu

F32 = jnp.float32
BF16 = jnp.bfloat16
I32 = jnp.int32
HIGHEST = lax.Precision.HIGHEST

LANES = 128
SUBLANES = 8
BF16_TILE_ROWS = 16
NSA_HEAD_DIM = 64
NSA_HEADS = 8
NSA_GROUPS = 2
NSA_Q_PER_KV = NSA_HEADS // NSA_GROUPS
CMP_BLOCK = 32
CMP_STRIDE = 16
CMP_HIDDEN = 128
SEL_BLOCK = 64
SEL_TOP = 8
WINDOW = 512
Q_BLOCK = 128
FORCE_BONUS = 1.0e4
RET_HEADS = 4
RET_HEAD_DIM = 128
RET_CHUNK = 128
ROPE_THETA = 10000.0
RMS_EPS = 1e-6
N_EXPERTS = 256
N_EXPERT_GROUPS = 8
TOPK_GROUPS = 4
TOP_K = 8
ROUTED_SCALE = 2.5
MOE_TILE = 256
WEIGHT_SLOTS = 3
SEL_CHUNK = 256
MASK_BIG = 2.0 ** 100
MASK_FLOOR = 2.0 ** 99
NEG = -1e30
VMEM_LIMIT = 48 * 1024 * 1024
EXPERTS_VMEM_LIMIT = 56 * 1024 * 1024


def _cparams(sem):
    return pltpu.CompilerParams(dimension_semantics=sem, vmem_limit_bytes=VMEM_LIMIT)


def _dot(a, b, **kw):
    return jnp.dot(a, b, preferred_element_type=F32, **kw)


def _dot_nt(a, b, **kw):
    return lax.dot_general(a, b, (((1,), (1,)), ((), ())), preferred_element_type=F32, **kw)


def _sigmoid(v):
    return 1.0 / (1.0 + jnp.exp(-v))


def _silu(v):
    return v * _sigmoid(v)


def _ada_kernel(c_ref, w_ref, b_ref, o_ref):
    o_ref[...] = _dot(_silu(c_ref[...]), w_ref[...], precision=HIGHEST) + b_ref[...]


def _ada(c, w, b):
    bsz, d = c.shape
    n = w.shape[1]
    tn = 1536
    cp = jnp.zeros((8, d), F32).at[:bsz].set(c)
    out = pl.pallas_call(
        _ada_kernel,
        grid=(n // tn,),
        in_specs=[pl.BlockSpec((8, d), lambda j: (0, 0)),
                  pl.BlockSpec((d, tn), lambda j: (0, j)),
                  pl.BlockSpec((1, tn), lambda j: (0, j))],
        out_specs=pl.BlockSpec((8, tn), lambda j: (0, j)),
        out_shape=jax.ShapeDtypeStruct((8, n), F32),
        compiler_params=_cparams(("arbitrary",)),
        name="ada",
    )(cp, w, b.reshape(1, n))
    return out[:bsz]


def _rope_table_kernel(pos_ref, inv_ref, sgn_ref, cos_ref, sin_ref):
    ang = pos_ref[...] * inv_ref[...]
    cos_ref[...] = jnp.cos(ang)
    sin_ref[...] = jnp.sin(ang) * sgn_ref[...]


def _rope_tables(posf):
    t = posf.shape[0]
    inv_n = ROPE_THETA ** (-jnp.arange(0, NSA_HEAD_DIM, 2, dtype=F32) / NSA_HEAD_DIM)
    inv_r = ROPE_THETA ** (-jnp.arange(0, RET_HEAD_DIM, 2, dtype=F32) / RET_HEAD_DIM)
    inv = jnp.concatenate([jnp.tile(inv_n, 4), jnp.tile(inv_r, 2)]).reshape(1, 2 * LANES)
    sgn_n = np.where((np.arange(LANES) % NSA_HEAD_DIM) < NSA_HEAD_DIM // 2, -1.0, 1.0)
    sgn_r = np.where(np.arange(LANES) < RET_HEAD_DIM // 2, -1.0, 1.0)
    sgn = jnp.asarray(np.concatenate([sgn_n, sgn_r]).reshape(1, 2 * LANES), F32)
    tm = min(t, 1024)
    return pl.pallas_call(
        _rope_table_kernel,
        grid=(t // tm,),
        in_specs=[pl.BlockSpec((tm, 1), lambda i: (i, 0)),
                  pl.BlockSpec((1, 2 * LANES), lambda i: (0, 0)),
                  pl.BlockSpec((1, 2 * LANES), lambda i: (0, 0))],
        out_specs=[pl.BlockSpec((tm, 2 * LANES), lambda i: (i, 0))] * 2,
        out_shape=[jax.ShapeDtypeStruct((t, 2 * LANES), F32)] * 2,
        compiler_params=_cparams(("arbitrary",)),
        name="rope_tables",
    )(posf, inv, sgn)


def _rope64(p, cos, sin, first_half):
    rot = jnp.where(first_half, pltpu.roll(p, 96, 1), pltpu.roll(p, 32, 1))
    return p * cos + rot * sin


def _rope128(p, cos, sin):
    return p * cos + pltpu.roll(p, 64, 1) * sin


_C_Q = 0
_C_KV = _C_Q + NSA_HEADS * LANES
_C_GATE = _C_KV + 6 * LANES
_C_RET = _C_GATE + LANES
_C_END = _C_RET + 4 * RET_HEADS * RET_HEAD_DIM


def _arrange_w_in(w_in):
    d = w_in.shape[0]
    nw = NSA_HEADS * NSA_HEAD_DIM
    q = w_in[:, :nw].reshape(d, NSA_HEADS, NSA_HEAD_DIM)
    z = jnp.zeros_like(q)
    grp = (jnp.arange(NSA_HEADS) // NSA_Q_PER_KV)[None, :, None]
    qpad = jnp.where(grp == 0, jnp.concatenate([q, z], -1), jnp.concatenate([z, q], -1))
    qpad = qpad.reshape(d, NSA_HEADS * LANES)
    kv = w_in[:, nw:nw + 6 * LANES]
    g0 = nw + 6 * LANES
    gate = jnp.pad(w_in[:, g0:g0 + 3 * NSA_HEADS], ((0, 0), (0, LANES - 3 * NSA_HEADS)))
    ret = w_in[:, g0 + 3 * NSA_HEADS:]
    return jnp.concatenate([qpad, kv, gate, ret], axis=1).astype(BF16)


def _inproj_kernel(x_ref, mod_ref, g_ref, w_ref, cos_ref, sin_ref,
                   q_ref, kc_ref, vc_ref, kk_ref, gate_ref, qr_ref, kr_ref, vr_ref, gr_ref, vst_ref, vwt_ref):
    x = x_ref[...]
    tm = x.shape[0]
    ms = jnp.mean(x * x, axis=-1, keepdims=True)
    y = x * lax.rsqrt(ms + RMS_EPS) * g_ref[...]
    h = y * (1.0 + mod_ref[0, 1:2, :]) + mod_ref[0, 0:1, :]
    hb = h.astype(BF16)
    cos_n, sin_n = cos_ref[:, 0:LANES], sin_ref[:, 0:LANES]
    cos_r, sin_r = cos_ref[:, LANES:], sin_ref[:, LANES:]
    lane = lax.broadcasted_iota(I32, (tm, LANES), 1)
    first_half = (lane % NSA_HEAD_DIM) < (NSA_HEAD_DIM // 2)
    scale_n = NSA_HEAD_DIM ** -0.5
    scale_r = RET_HEAD_DIM ** -0.5

    def proj(c0, n):
        return _dot(hb, w_ref[:, c0:c0 + n])

    for hh in range(NSA_HEADS):
        p = proj(_C_Q + hh * LANES, LANES)
        q_ref[:, hh * LANES:(hh + 1) * LANES] = (_rope64(p, cos_n, sin_n, first_half) * scale_n).astype(BF16)
    kv = proj(_C_KV, 6 * LANES)
    kc_ref[...] = kv[:, 0:LANES].astype(BF16)
    vc_ref[...] = kv[:, LANES:2 * LANES].astype(BF16)
    kk_ref[:, 0:LANES] = _rope64(kv[:, 2 * LANES:3 * LANES], cos_n, sin_n, first_half).astype(BF16)
    kk_ref[:, LANES:2 * LANES] = _rope64(kv[:, 4 * LANES:5 * LANES], cos_n, sin_n, first_half).astype(BF16)
    group0 = lane < NSA_HEAD_DIM
    for c0, vt_ref, chunk in ((3 * LANES, vst_ref, SEL_CHUNK), (5 * LANES, vwt_ref, Q_BLOCK)):
        v = kv[:, c0:c0 + LANES]
        for g, vg in enumerate((jnp.where(group0, v, 1.0), jnp.where(group0, 1.0, v))):
            for cc in range(tm // chunk):
                vt_ref[cc, g] = vg[cc * chunk:(cc + 1) * chunk].T.astype(BF16)
    gate_ref[...] = _sigmoid(proj(_C_GATE, LANES))
    rw = RET_HEADS * RET_HEAD_DIM
    for hh in range(RET_HEADS):
        sl = slice(hh * LANES, (hh + 1) * LANES)
        pq = proj(_C_RET + hh * LANES, LANES)
        qr_ref[:, sl] = _rope128(pq, cos_r, sin_r).astype(BF16)
        pk = proj(_C_RET + rw + hh * LANES, LANES)
        kr_ref[:, sl] = (_rope128(pk, cos_r, sin_r) * scale_r).astype(BF16)
    vr_ref[...] = proj(_C_RET + 2 * rw, rw).astype(BF16)
    gr_ref[...] = proj(_C_RET + 3 * rw, rw)


def _inproj(x2, mod3, g_mix, w_in_p, cos_t, sin_t, seq):
    t, d = x2.shape
    tm = min(512, seq)
    rw = RET_HEADS * RET_HEAD_DIM
    row = lambda n: pl.BlockSpec((tm, n), lambda i: (i, 0))
    outs = [(NSA_HEADS * LANES, BF16), (LANES, BF16), (LANES, BF16), (2 * LANES, BF16), (LANES, F32),
            (rw, BF16), (rw, BF16), (rw, BF16), (rw, F32)]
    vt_specs, vt_shapes = [], []
    for chunk in (SEL_CHUNK, Q_BLOCK):
        vt_specs.append(pl.BlockSpec((tm // chunk, NSA_GROUPS, LANES, chunk), lambda i: (i, 0, 0, 0)))
        vt_shapes.append(jax.ShapeDtypeStruct((t // chunk, NSA_GROUPS, LANES, chunk), BF16))
    return pl.pallas_call(
        _inproj_kernel,
        grid=(t // tm,),
        in_specs=[row(d),
                  pl.BlockSpec((1, 6, d), lambda i: ((i * tm) // seq, 0, 0)),
                  pl.BlockSpec((1, d), lambda i: (0, 0)),
                  pl.BlockSpec((d, _C_END), lambda i: (0, 0)),
                  row(2 * LANES), row(2 * LANES)],
        out_specs=[row(n) for n, _ in outs] + vt_specs,
        out_shape=[jax.ShapeDtypeStruct((t, n), dt) for n, dt in outs] + vt_shapes,
        compiler_params=_cparams(("arbitrary",)),
        name="inproj",
    )(x2, mod3, g_mix, w_in_p, cos_t, sin_t)


def _gelu_tanh(v):
    return v * (0.5 * (1.0 + jnp.tanh(math.sqrt(2.0 / math.pi) * (v + 0.044715 * (v * v * v)))))


def _compress_kernel(kc_ref, vc_ref, wkt_ref, wkb_ref, wk2_ref, pk_ref, wvt_ref, wvb_ref, wv2_ref, pv_ref,
                     cos_ref, sin_ref, ko_ref, vo_ref):
    def one(x_ref, wt_ref, wb_ref, w2_ref, p_ref):
        xx = x_ref[0]
        a = _dot(xx, wt_ref[...])
        b = _dot(xx, wb_ref[...])
        pb = p_ref[...].astype(BF16)
        bias = _dot(pb, wt_ref[...])[0:1] + _dot(pb, wb_ref[...])[1:2]
        hid = a + pltpu.roll(b, b.shape[0] - 1, 0) + bias
        return _dot(_gelu_tanh(hid).astype(BF16), w2_ref[...])

    k = one(kc_ref, wkt_ref, wkb_ref, wk2_ref, pk_ref)
    lane = lax.broadcasted_iota(I32, k.shape, 1)
    first_half = (lane % NSA_HEAD_DIM) < (NSA_HEAD_DIM // 2)
    ko_ref[0] = _rope64(k, cos_ref[0], sin_ref[0], first_half).astype(BF16)
    v = one(vc_ref, wvt_ref, wvb_ref, wv2_ref, pv_ref)
    group0 = lane < NSA_HEAD_DIM
    vo_ref[0, 0] = jnp.where(group0, v, 1.0).T.astype(BF16)
    vo_ref[0, 1] = jnp.where(group0, 1.0, v).T.astype(BF16)


def _arrange_cmp_weights(pos, w1, w2):
    half = CMP_BLOCK // 2
    dh = NSA_HEAD_DIM
    w1r = w1.reshape(CMP_BLOCK, dh, CMP_HIDDEN)

    def block(wpart):
        z = jnp.zeros_like(wpart)
        g0 = jnp.concatenate([wpart, z], axis=-1)
        g1 = jnp.concatenate([z, wpart], axis=-1)
        return jnp.stack([g0, g1], axis=1).reshape(half * 2 * dh, 2 * CMP_HIDDEN)

    wt, wb = block(w1r[:half]), block(w1r[half:])
    z2 = jnp.zeros_like(w2)
    w2b = jnp.concatenate([jnp.concatenate([w2, z2], 1), jnp.concatenate([z2, w2], 1)], 0)
    ptop = jnp.tile(pos[:half], (1, 2)).reshape(1, -1)
    pbot = jnp.tile(pos[half:], (1, 2)).reshape(1, -1)
    prow = jnp.concatenate([ptop, pbot, jnp.zeros((6, ptop.shape[1]), F32)], 0)
    return wt.astype(BF16), wb.astype(BF16), w2b.astype(BF16), prow


def _compress(kc, vc, cos_c, sin_c, kparams, vparams, bsz, seq):
    nchunk = seq // CMP_STRIDE
    width = CMP_STRIDE * LANES
    kc16 = kc.reshape(bsz, nchunk, width)
    vc16 = vc.reshape(bsz, nchunk, width)
    full = lambda a: pl.BlockSpec(a.shape, lambda b: (0,) * a.ndim)
    per_b = lambda n: pl.BlockSpec((1, nchunk, n), lambda b: (b, 0, 0))
    return pl.pallas_call(
        _compress_kernel,
        grid=(bsz,),
        in_specs=[per_b(width), per_b(width)] + [full(a) for a in kparams] + [full(a) for a in vparams]
                 + [per_b(LANES), per_b(LANES)],
        out_specs=[per_b(LANES), pl.BlockSpec((1, NSA_GROUPS, LANES, nchunk), lambda b: (b, 0, 0, 0))],
        out_shape=[jax.ShapeDtypeStruct((bsz, nchunk, LANES), BF16),
                   jax.ShapeDtypeStruct((bsz, NSA_GROUPS, LANES, nchunk), BF16)],
        compiler_params=_cparams(("arbitrary",)),
        name="compress",
    )(kc16, vc16, *kparams, *vparams, cos_c, sin_c)


def _softmax_chunk(m, acc, s, vt):
    m_new = jnp.maximum(m, jnp.max(s, axis=0, keepdims=True))
    e = jnp.exp(s - m_new).astype(BF16)
    acc = jnp.exp(m - m_new) * acc + _dot(vt, e)
    return m_new, acc


def _nsa_kernel(q_ref, kc_ref, vct_ref, ks_ref, kw_ref, vst_ref, vwt_ref, gate_ref,
                ovt_ref, exp_ref, g_ref, o_ref, *, seq):
    qb = pl.program_id(1)
    t0 = qb * Q_BLOCK
    cols = NSA_Q_PER_KV * Q_BLOCK
    n_cmp_pad = kc_ref.shape[1]
    n_sel = seq // SEL_BLOCK
    t_row = t0 + lax.broadcasted_iota(I32, (1, Q_BLOCK), 1)
    gates_t = gate_ref[...].T
    lane = lax.broadcasted_iota(I32, (Q_BLOCK, LANES), 1)
    key = lax.broadcasted_iota(I32, (Q_BLOCK, Q_BLOCK), 0)
    tok = lax.broadcasted_iota(I32, (Q_BLOCK, Q_BLOCK), 1)
    cend = lax.broadcasted_iota(I32, (n_cmp_pad, 1), 0) * CMP_STRIDE + (CMP_BLOCK - 1)
    bias_cmp = jnp.where(cend <= t_row, 0.0, -MASK_BIG)
    last_c = qb // (SEL_CHUNK // Q_BLOCK)
    kpos_last = last_c * SEL_CHUNK + lax.broadcasted_iota(I32, (SEL_CHUNK, 1), 0)
    bias_diag = jnp.where(kpos_last <= t_row, 0.0, -MASK_BIG)
    n_win = WINDOW // Q_BLOCK
    bias_win_first = jnp.where(key > tok, 0.0, -MASK_BIG)
    bias_win_last = jnp.where(key <= tok, 0.0, -MASK_BIG)
    init = (jnp.full((1, cols), -MASK_FLOOR, F32), jnp.zeros((LANES, cols), F32))
    tile_heads = lambda b: jnp.concatenate([b] * NSA_Q_PER_KV, axis=1)

    def finish(acc, g):
        ones_row = NSA_HEAD_DIM * (1 - g)
        return acc * (1.0 / jnp.maximum(acc[ones_row:ones_row + 1, :], 1e-20))

    q4s, o_cs, sel_bs = [], [], []
    for g in range(NSA_GROUPS):
        q4 = jnp.concatenate([q_ref[:, (NSA_Q_PER_KV * g + r) * LANES:(NSA_Q_PER_KV * g + r + 1) * LANES]
                              for r in range(NSA_Q_PER_KV)], axis=0)
        q4s.append(q4)
        s_c = _dot_nt(kc_ref[0], q4) + tile_heads(bias_cmp)
        e_c = jnp.exp(s_c - jnp.maximum(jnp.max(s_c, axis=0, keepdims=True), -MASK_FLOOR))
        p_c = e_c * (1.0 / jnp.maximum(jnp.sum(e_c, axis=0, keepdims=True), 1e-20))
        psum = p_c[:, 0:Q_BLOCK]
        for r in range(1, NSA_Q_PER_KV):
            psum = psum + p_c[:, r * Q_BLOCK:(r + 1) * Q_BLOCK]
        o_cs.append(finish(_dot(vct_ref[0, g], e_c.astype(BF16)), g))
        imp_t = _dot(ovt_ref[...], psum, precision=HIGHEST)
        nsp = imp_t.shape[0]
        jrow = lax.broadcasted_iota(I32, (nsp, 1), 0)
        cur = t_row // SEL_BLOCK
        valid = (jrow * SEL_BLOCK <= t_row) & (jrow < n_sel)
        forced = (jrow == 0) | (jrow == cur) | (jrow == cur - 1)
        val = jnp.where(valid, imp_t + jnp.where(forced, FORCE_BONUS, 0.0), -1.0)
        val = jnp.where(jrow < n_sel, val, -jnp.inf)
        sel_t = jnp.zeros((nsp, Q_BLOCK), F32)
        for _ in range(min(SEL_TOP, n_sel)):
            mx = jnp.max(val, axis=0, keepdims=True)
            jmin = jnp.min(jnp.where(val == mx, jrow, nsp), axis=0, keepdims=True)
            hit = jrow == jmin
            sel_t = jnp.where(hit, 1.0, sel_t)
            val = jnp.where(hit, -jnp.inf, val)
        sel_bs.append(jnp.concatenate([sel_t[:n_sel], jnp.ones((SUBLANES, Q_BLOCK), F32),
                                       jnp.zeros((LANES - n_sel - SUBLANES, Q_BLOCK), F32)], axis=0).astype(BF16))

    def sel_scores(c):
        keys = ks_ref[pl.ds(pl.multiple_of(c * SEL_CHUNK, SEL_CHUNK), SEL_CHUNK), :]
        expand = exp_ref[c]
        return tuple(_dot_nt(keys, q4s[g]) + tile_heads(_dot(expand, sel_bs[g])) for g in range(NSA_GROUPS))

    def sel_reduce(c, state, scores):
        out = []
        for g in range(NSA_GROUPS):
            out.extend(_softmax_chunk(state[2 * g], state[2 * g + 1], scores[g], vst_ref[c, g]))
        return tuple(out)

    def sel_step(c, carry):
        state, scores = carry
        return sel_reduce(c, state, scores), sel_scores(c + 1)

    state, scores = lax.fori_loop(0, last_c, sel_step, (init * NSA_GROUPS, sel_scores(0)))
    diag = tile_heads(bias_diag)
    state = sel_reduce(last_c, state, tuple(s + diag for s in scores))
    o_ss = [finish(state[2 * g + 1], g) for g in range(NSA_GROUPS)]
    s_ws = [[] for _ in range(NSA_GROUPS)]
    kbs = []
    for w in range(n_win + 1):
        wb = qb - n_win + w
        kb = jnp.maximum(wb, 0)
        kbs.append(kb)
        keys = kw_ref[pl.ds(pl.multiple_of(kb * Q_BLOCK, Q_BLOCK), Q_BLOCK), :]
        if w == n_win:
            bias = bias_win_last
        else:
            before_start = jnp.where(wb >= 0, 0.0, -MASK_BIG)
            bias = (bias_win_first + before_start) if w == 0 else jnp.full((Q_BLOCK, Q_BLOCK), before_start)
        for g in range(NSA_GROUPS):
            s_ws[g].append(_dot_nt(keys, q4s[g]) + tile_heads(bias))
    o_ws = []
    for g in range(NSA_GROUPS):
        m_w = jnp.full((1, cols), -MASK_FLOOR, F32)
        for s in s_ws[g]:
            m_w = jnp.maximum(m_w, jnp.max(s, axis=0, keepdims=True))
        acc = None
        for kb, s in zip(kbs, s_ws[g]):
            pv = _dot(vwt_ref[kb, g], jnp.exp(s - m_w).astype(BF16))
            acc = pv if acc is None else acc + pv
        o_ws.append(finish(acc, g))
    heads_out = []
    for g in range(NSA_GROUPS):
        o_c, o_s, o_w = o_cs[g], o_ss[g], o_ws[g]
        for r in range(NSA_Q_PER_KV):
            hh = NSA_Q_PER_KV * g + r
            cs = slice(r * Q_BLOCK, (r + 1) * Q_BLOCK)
            o = (gates_t[3 * hh:3 * hh + 1, :] * o_c[:, cs] + gates_t[3 * hh + 1:3 * hh + 2, :] * o_s[:, cs]
                 + gates_t[3 * hh + 2:3 * hh + 3, :] * o_w[:, cs]).T
            if (hh % 2) != g:
                o = pltpu.roll(o, NSA_HEAD_DIM, 1)
            heads_out.append(o)
    blocks = [jnp.where(lane < NSA_HEAD_DIM, heads_out[2 * i], heads_out[2 * i + 1])
              for i in range(NSA_HEADS // 2)]
    ss = sum(jnp.sum(b * b, axis=-1, keepdims=True) for b in blocks)
    inv = lax.rsqrt(ss / (NSA_HEADS * NSA_HEAD_DIM) + RMS_EPS)
    for i, b in enumerate(blocks):
        sl = slice(i * LANES, (i + 1) * LANES)
        o_ref[:, sl] = (b * inv * g_ref[:, sl]).astype(BF16)


def _nsa(q, kcmp, vcmp_t, kk, vsel_t, vwin_t, gates, g_nsa, bsz, seq):
    t = bsz * seq
    nq = seq // Q_BLOCK
    n_cmp_pad = seq // CMP_STRIDE
    n_cmp = (seq - CMP_BLOCK) // CMP_STRIDE + 1
    n_sel = seq // SEL_BLOCK
    assert n_sel % SUBLANES == 0 and n_sel + SUBLANES <= LANES, "selection mask needs a spare expansion row"
    nsp = n_sel
    cs = np.arange(n_cmp_pad) * CMP_STRIDE
    ss = np.arange(nsp) * SEL_BLOCK
    ov = ((cs[None, :] < ss[:, None] + SEL_BLOCK) & (cs[None, :] + CMP_BLOCK > ss[:, None])
          & (np.arange(n_cmp_pad)[None, :] < n_cmp) & (np.arange(nsp)[:, None] < n_sel))
    ovt = jnp.asarray(ov, F32)
    nch = seq // SEL_CHUNK
    kp = np.arange(seq).reshape(nch, SEL_CHUNK, 1)
    col = np.arange(LANES).reshape(1, 1, LANES)
    expand = jnp.asarray(np.where(kp // SEL_BLOCK == col, MASK_BIG, 0.0) + np.where(col == n_sel, -MASK_BIG, 0.0),
                         BF16)
    seqcol = lambda c: pl.BlockSpec((seq, LANES), lambda b, i: (b, c))
    per_b = lambda a: pl.BlockSpec((a.shape[0] // bsz,) + a.shape[1:], lambda b, i: (b,) + (0,) * (a.ndim - 1))
    return pl.pallas_call(
        functools.partial(_nsa_kernel, seq=seq),
        grid=(bsz, nq),
        in_specs=[pl.BlockSpec((Q_BLOCK, NSA_HEADS * LANES), lambda b, i: (b * nq + i, 0)),
                  per_b(kcmp), per_b(vcmp_t), seqcol(0), seqcol(1), per_b(vsel_t), per_b(vwin_t),
                  pl.BlockSpec((Q_BLOCK, LANES), lambda b, i: (b * nq + i, 0)),
                  pl.BlockSpec(ovt.shape, lambda b, i: (0, 0)),
                  pl.BlockSpec(expand.shape, lambda b, i: (0, 0, 0)),
                  pl.BlockSpec((1, NSA_HEADS * NSA_HEAD_DIM), lambda b, i: (0, 0))],
        out_specs=pl.BlockSpec((Q_BLOCK, NSA_HEADS * NSA_HEAD_DIM), lambda b, i: (b * nq + i, 0)),
        out_shape=jax.ShapeDtypeStruct((t, NSA_HEADS * NSA_HEAD_DIM), BF16),
        compiler_params=_cparams(("arbitrary", "arbitrary")),
        name="nsa",
    )(q, kcmp, vcmp_t, kk, kk, vsel_t, vwin_t, gates, ovt, expand, g_nsa)


def _ret_kernel(q_ref, k_ref, v_ref, gr_ref, dec_ref, xi_ref, zeta_ref, gch_ref, g_ref, o_ref, st_ref):
    @pl.when(pl.program_id(1) == 0)
    def _():
        st_ref[...] = jnp.zeros_like(st_ref)

    for cc in range(q_ref.shape[0] // RET_CHUNK):
        rs = slice(cc * RET_CHUNK, (cc + 1) * RET_CHUNK)
        for hh in range(RET_HEADS):
            sl = slice(hh * LANES, (hh + 1) * LANES)
            q, k, v = q_ref[rs, sl], k_ref[rs, sl], v_ref[rs, sl]
            sc = _dot_nt(q, k) * dec_ref[hh]
            inner = _dot(sc.astype(BF16), v)
            st = st_ref[hh]
            cross = _dot((q.astype(F32) * xi_ref[hh]).astype(BF16), st.astype(BF16))
            kz = (k.astype(F32) * zeta_ref[hh]).T.astype(BF16)
            st_ref[hh] = st * gch_ref[hh] + _dot(kz, v)
            o = inner + cross
            y = o * lax.rsqrt(jnp.mean(o * o, axis=-1, keepdims=True) + RMS_EPS) * g_ref[hh:hh + 1, :]
            o_ref[rs, sl] = (_silu(gr_ref[rs, sl]) * y).astype(BF16)


def _retention(qr, kr, vr, gr, g_ret, bsz, seq):
    t = bsz * seq
    c = RET_CHUNK
    n = seq // c
    log_g = jnp.log(1.0 - 2.0 ** (-5.0 - jnp.arange(RET_HEADS, dtype=F32)))
    i = jnp.arange(c, dtype=F32)
    diff = i[:, None] - i[None, :]
    causal = diff >= 0
    dec = jnp.where(causal, jnp.exp(log_g[:, None, None] * jnp.where(causal, diff, 0.0)), 0.0)
    xi = jnp.broadcast_to(jnp.exp(log_g[:, None] * (i + 1.0))[:, :, None], (RET_HEADS, c, LANES))
    zeta = jnp.broadcast_to(jnp.exp(log_g[:, None] * (c - 1.0 - i))[:, :, None], (RET_HEADS, c, LANES))
    gch = jnp.broadcast_to(jnp.exp(log_g * c)[:, None, None], (RET_HEADS, 1, LANES))
    w = RET_HEADS * RET_HEAD_DIM
    per_step = 4 if n % 4 == 0 else 1
    steps = n // per_step
    row = pl.BlockSpec((c * per_step, w), lambda b, j: (b * steps + j, 0))
    full = lambda a: pl.BlockSpec(a.shape, lambda b, j: (0,) * a.ndim)
    return pl.pallas_call(
        _ret_kernel,
        grid=(bsz, steps),
        in_specs=[row, row, row, row, full(dec), full(xi), full(zeta), full(gch), full(g_ret)],
        out_specs=row,
        out_shape=jax.ShapeDtypeStruct((t, w), BF16),
        scratch_shapes=[pltpu.VMEM((RET_HEADS, RET_HEAD_DIM, RET_HEAD_DIM), F32)],
        compiler_params=_cparams(("arbitrary", "arbitrary")),
        name="retention",
    )(qr, kr, vr, gr, dec, xi, zeta, gch, g_ret)


def _post_kernel(x_ref, onsa_ref, oret_ref, mod_ref, wo1_ref, wo2_ref, gffn_ref, wrt_ref, rb_ref,
                 wgs_ref, wus_ref, wds_ref, acc_ref, h2_ref, idx_ref, wt_ref):
    mix = _dot(onsa_ref[...], wo1_ref[...]) + _dot(oret_ref[...], wo2_ref[...])
    x1 = x_ref[...] + mod_ref[0, 2:3, :] * mix
    ms = jnp.mean(x1 * x1, axis=-1, keepdims=True)
    h2 = x1 * lax.rsqrt(ms + RMS_EPS) * gffn_ref[...] * (1.0 + mod_ref[0, 4:5, :]) + mod_ref[0, 3:4, :]
    _rows_to_tiles(h2_ref, h2)
    hb = h2.astype(BF16)
    hid = (_silu(_dot(hb, wgs_ref[...])) * _dot(hb, wus_ref[...])).astype(BF16)
    acc_ref[...] = x1 + mod_ref[0, 5:6, :] * _dot(hid, wds_ref[...])
    s = _sigmoid(_dot_nt(wrt_ref[...], h2, precision=HIGHEST))
    sb = s + rb_ref[...]
    per = N_EXPERTS // N_EXPERT_GROUPS
    ridx = lax.broadcasted_iota(I32, (per, 1), 0)
    blks, grp = [], []
    for gi in range(N_EXPERT_GROUPS):
        blk = sb[gi * per:(gi + 1) * per]
        m1 = jnp.max(blk, axis=0, keepdims=True)
        first = jnp.min(jnp.where(blk == m1, ridx, per), axis=0, keepdims=True)
        m2 = jnp.max(jnp.where(ridx == first, -jnp.inf, blk), axis=0, keepdims=True)
        blks.append(blk)
        grp.append(m1 + m2)
    masked = []
    for gi in range(N_EXPERT_GROUPS):
        rank = jnp.zeros_like(grp[gi])
        for gj in range(N_EXPERT_GROUPS):
            if gj < gi:
                rank = rank + (grp[gj] >= grp[gi]).astype(F32)
            elif gj > gi:
                rank = rank + (grp[gj] > grp[gi]).astype(F32)
        masked.append(jnp.where(rank < TOPK_GROUPS, blks[gi], NEG))
    val = jnp.concatenate(masked, axis=0)
    eidx = lax.broadcasted_iota(I32, (N_EXPERTS, 1), 0)
    ids, ws = [], []
    for _ in range(TOP_K):
        mx = jnp.max(val, axis=0, keepdims=True)
        emin = jnp.min(jnp.where(val == mx, eidx, N_EXPERTS), axis=0, keepdims=True)
        hit = eidx == emin
        ids.append(emin)
        ws.append(jnp.sum(jnp.where(hit, s, 0.0), axis=0, keepdims=True))
        val = jnp.where(hit, -jnp.inf, val)
    wsum = ws[0]
    for wk in ws[1:]:
        wsum = wsum + wk
    idx_ref[...] = jnp.concatenate(ids, axis=0)
    wt_ref[...] = jnp.concatenate(ws, axis=0) / wsum * ROUTED_SCALE


def _post(x2, onsa, oret, mod3, w_out, g_ffn, w_router, router_bias, wgs, wus, wds, seq):
    t, d = x2.shape
    tm = min(512, seq)
    hw = onsa.shape[1]
    wo1 = w_out[:hw].astype(BF16)
    wo2 = w_out[hw:].astype(BF16)
    wrt = w_router.T
    rb = jnp.broadcast_to(router_bias.reshape(N_EXPERTS, 1), (N_EXPERTS, tm))
    row = lambda n: pl.BlockSpec((tm, n), lambda i: (i, 0))
    full = lambda a: pl.BlockSpec(a.shape, lambda i: (0,) * a.ndim)
    col = pl.BlockSpec((TOP_K, tm), lambda i: (0, i))
    ops = (wo1, wo2, g_ffn, wrt, rb, wgs.astype(BF16), wus.astype(BF16), wds.astype(BF16))
    return pl.pallas_call(
        _post_kernel,
        grid=(t // tm,),
        in_specs=[row(d), row(hw), row(oret.shape[1]),
                  pl.BlockSpec((1, 6, d), lambda i: ((i * tm) // seq, 0, 0))] + [full(a) for a in ops],
        out_specs=[row(d), pl.BlockSpec((tm * SUBLANES, LANES), lambda i: (i, 0)), col, col],
        out_shape=[jax.ShapeDtypeStruct((t, d), F32), jax.ShapeDtypeStruct((t * SUBLANES, LANES), F32),
                   jax.ShapeDtypeStruct((TOP_K, t), I32), jax.ShapeDtypeStruct((TOP_K, t), F32)],
        compiler_params=_cparams(("arbitrary",)),
        name="post",
    )(x2, onsa, oret, mod3, *ops)


def _expert_select(idx_row, table, eidx):
    return jnp.sum(jnp.where(eidx == idx_row, table, 0.0), axis=0, keepdims=True)


def _rank_kernel(idx_ref, tri_ref, rank_ref, cnt_ref, carry):
    @pl.when(pl.program_id(0) == 0)
    def _():
        carry[...] = jnp.zeros_like(carry)

    idx = idx_ref[...]
    eidx = lax.broadcasted_iota(I32, (N_EXPERTS, 1), 0)
    member = jnp.zeros((N_EXPERTS, idx.shape[1]), F32)
    for k in range(TOP_K):
        member = member + (eidx == idx[k:k + 1, :]).astype(F32)
    before = _dot(member.astype(BF16), tri_ref[...]) + carry[:, 0:1]
    rank_ref[...] = jnp.concatenate(
        [_expert_select(idx[k:k + 1, :], before, eidx) for k in range(TOP_K)], axis=0).astype(I32)
    carry[...] = carry[...] + jnp.sum(member, axis=1, keepdims=True)
    cnt_ref[...] = carry[...]


def _dest_kernel(idx_ref, rank_ref, start_ref, pos_ref):
    idx = idx_ref[...]
    eidx = lax.broadcasted_iota(I32, (N_EXPERTS, 1), 0)
    start = start_ref[:, 0:1]
    base = jnp.concatenate([_expert_select(idx[k:k + 1, :], start, eidx) for k in range(TOP_K)], axis=0)
    pos_ref[...] = rank_ref[...] + base.astype(I32)


def _route_plan(idx_t, n_tok):
    tm = min(512, n_tok)
    tri = jnp.asarray(np.triu(np.ones((tm, tm), np.float32), 1), BF16)
    col = pl.BlockSpec((TOP_K, tm), lambda i: (0, i))
    rank, cnt = pl.pallas_call(
        _rank_kernel,
        grid=(n_tok // tm,),
        in_specs=[col, pl.BlockSpec((tm, tm), lambda i: (0, 0))],
        out_specs=[col, pl.BlockSpec((N_EXPERTS, LANES), lambda i: (0, 0))],
        out_shape=[jax.ShapeDtypeStruct((TOP_K, n_tok), I32), jax.ShapeDtypeStruct((N_EXPERTS, LANES), F32)],
        scratch_shapes=[pltpu.VMEM((N_EXPERTS, LANES), F32)],
        compiler_params=_cparams(("arbitrary",)),
        name="route_rank",
    )(idx_t, tri)
    counts = cnt[:, 0].astype(I32)
    padded = ((counts + MOE_TILE - 1) // MOE_TILE) * MOE_TILE
    ends = jnp.cumsum(padded)
    starts = ends - padded
    start_b = jnp.broadcast_to(starts.astype(F32)[:, None], (N_EXPERTS, LANES))
    pos = pl.pallas_call(
        _dest_kernel,
        grid=(n_tok // tm,),
        in_specs=[col, col, pl.BlockSpec((N_EXPERTS, LANES), lambda i: (0, 0))],
        out_specs=col,
        out_shape=jax.ShapeDtypeStruct((TOP_K, n_tok), I32),
        compiler_params=_cparams(("arbitrary",)),
        name="route_dest",
    )(idx_t, rank, start_b)
    n_tiles = TOP_K * n_tok // MOE_TILE + N_EXPERTS
    n_used = (ends[-1] // MOE_TILE).astype(I32)
    tile_start = jnp.minimum(jnp.arange(n_tiles, dtype=I32), n_used - 1) * MOE_TILE
    tile_e = jnp.minimum(jnp.sum((ends[None, :] <= tile_start[:, None]).astype(I32), axis=1), N_EXPERTS - 1)
    first = jnp.concatenate([jnp.ones((1,), I32), (tile_e[1:] != tile_e[:-1]).astype(I32)])
    eidx = jnp.arange(N_EXPERTS, dtype=I32)
    used = counts > 0
    rank = jnp.cumsum(used.astype(I32)) - 1

    def later(j):
        hit = used[None, :] & (rank[None, :] == rank[:, None] + j)
        return jnp.max(jnp.where(hit, eidx[None, :], -1), axis=1).astype(I32)

    plan = (tile_e, first, n_used.reshape(1), later(1), later(WEIGHT_SLOTS - 1), (rank % WEIGHT_SLOTS).astype(I32))
    slot = jnp.arange(MOE_TILE, dtype=I32)[None, :]
    rem = (counts % MOE_TILE)[:, None]
    n_rows = n_tiles * MOE_TILE
    pad_key = jnp.where((rem != 0) & (slot >= rem),
                        (starts + (counts // MOE_TILE) * MOE_TILE)[:, None] + slot, n_rows)
    keys = jnp.concatenate([pos.reshape(-1), pad_key.reshape(-1).astype(I32)]).astype(jnp.uint32)
    toks = jnp.concatenate([jnp.tile(jnp.arange(n_tok, dtype=jnp.uint32), TOP_K),
                            jnp.zeros((N_EXPERTS * MOE_TILE,), jnp.uint32)])
    assert (n_rows + 1) * n_tok <= 2 ** 32, "plan row and token id share one 32-bit sort word"
    packed = lax.sort(keys * jnp.uint32(n_tok) + toks)
    row_tok = (packed % jnp.uint32(n_tok)).astype(I32).reshape(n_tiles, 1, MOE_TILE)
    return pos, plan, row_tok


def _rows_to_tiles(ref, val):
    n = val.shape[0]
    for s in range(SUBLANES):
        ref[pl.ds(s, n, stride=SUBLANES), :] = val[:, s * LANES:(s + 1) * LANES]


def _tiles_to_rows(ref, n, *lead):
    return jnp.concatenate([ref[(*lead, pl.ds(s, n, stride=SUBLANES), slice(None))] for s in range(SUBLANES)],
                           axis=1)


def _tile_rows(ref, row, n=1):
    start = row * SUBLANES if isinstance(row, int) else pl.multiple_of(row * SUBLANES, SUBLANES)
    return ref.at[pl.ds(start, n * SUBLANES), :]


def _gather_row(hbuf, tok_ref, stage, j):
    t = tok_ref[0, 0, j]
    pair = hbuf[pl.ds(pl.multiple_of((t >> 1) * BF16_TILE_ROWS, BF16_TILE_ROWS), BF16_TILE_ROWS), :].astype(F32)
    start = j * SUBLANES if isinstance(j, int) else pl.multiple_of(j * SUBLANES, SUBLANES)
    stage[pl.ds(start, SUBLANES), :] = jnp.where((t & 1) == 1, pair[SUBLANES:], pair[:SUBLANES])


def _expert_kernel(tile_e, first, n_used, next_e, ahead_e, wslot, tokc_ref, tokn_ref, h2_hbm, wg_hbm, wu_hbm,
                   wd_hbm, y_ref, hbuf, conv, stage_a, stage_b, wgf, wuf, wdf, wgb, wub, wdb, hsem, sem):
    i = pl.program_id(0)
    n = n_used[0]

    def weight_copies(e):
        slot = wslot[e]
        return [pltpu.make_async_copy(src.at[e], dst.at[slot], sem.at[slot])
                for src, dst in ((wg_hbm, wgf), (wu_hbm, wuf), (wd_hbm, wdf))]

    @pl.when(i == 0)
    def _():
        e = tile_e[0]
        for _ in range(WEIGHT_SLOTS - 1):
            @pl.when(e >= 0)
            def _():
                for cp in weight_copies(e):
                    cp.start()
            e = jnp.where(e >= 0, next_e[jnp.maximum(e, 0)], -1)
        rows = conv.shape[1]
        n_conv = h2_hbm.shape[0] // rows

        def chunk_copy(c, slot):
            return pltpu.make_async_copy(h2_hbm.at[pl.ds(pl.multiple_of(c * rows, rows), rows), :], conv.at[slot],
                                         hsem.at[slot])

        chunk_copy(0, 0).start()

        def convert(c, carry):
            slot = c % 2
            chunk_copy(c, slot).wait()

            @pl.when(c + 1 < n_conv)
            def _():
                chunk_copy(c + 1, 1 - slot).start()

            hbuf[pl.ds(pl.multiple_of(c * rows, rows), rows), :] = conv[slot].astype(BF16)
            return carry
        lax.fori_loop(0, n_conv, convert, 0)

        def first_tile(jj, carry):
            for u in range(SUBLANES):
                _gather_row(hbuf, tokc_ref, stage_a, jj * SUBLANES + u)
            return carry
        lax.fori_loop(0, MOE_TILE // SUBLANES, first_tile, 0)

    @pl.when(i < n)
    def _():
        @pl.when(first[i] == 1)
        def _():
            e = tile_e[i]
            slot = wslot[e]
            for cp in weight_copies(e):
                cp.wait()

            @pl.when(ahead_e[e] >= 0)
            def _():
                for cp in weight_copies(ahead_e[e]):
                    cp.start()

            wgb[...] = wgf[slot].astype(BF16)
            wub[...] = wuf[slot].astype(BF16)
            wdb[...] = wdf[slot].astype(BF16)

        for par, (cur, nxt) in enumerate(((stage_a, stage_b), (stage_b, stage_a))):
            @pl.when(i % 2 == par)
            def _():
                for j in range(MOE_TILE):
                    _gather_row(hbuf, tokn_ref, nxt, j)
                xb = _tiles_to_rows(cur, MOE_TILE).astype(BF16)
                hid = (_silu(_dot(xb, wgb[...])) * _dot(xb, wub[...])).astype(BF16)
                _rows_to_tiles(y_ref, _dot(hid, wdb[...]))

    @pl.when(i >= n)
    def _():
        y_ref[...] = jnp.zeros_like(y_ref)


def _experts(h2, row_tok, plan, wg, wu, wd):
    n_tiles = row_tok.shape[0]
    d, de = wg.shape[1], wg.shape[2]
    blk = (MOE_TILE * SUBLANES, LANES)
    conv_rows = min(4096, h2.shape[0])
    assert h2.shape[0] % conv_rows == 0 and h2.shape[0] % BF16_TILE_ROWS == 0
    hbm = pl.BlockSpec(memory_space=pl.ANY)
    grid_spec = pltpu.PrefetchScalarGridSpec(
        num_scalar_prefetch=len(plan),
        grid=(n_tiles,),
        in_specs=[pl.BlockSpec((1, 1, MOE_TILE), lambda i, *p: (jnp.minimum(i, p[2][0] - 1), 0, 0),
                               memory_space=pltpu.SMEM),
                  pl.BlockSpec((1, 1, MOE_TILE), lambda i, *p: (jnp.minimum(i + 1, p[2][0] - 1), 0, 0),
                               memory_space=pltpu.SMEM), hbm, hbm, hbm, hbm],
        out_specs=pl.BlockSpec(blk, lambda i, *p: (i, 0)),
        scratch_shapes=[pltpu.VMEM(h2.shape, BF16), pltpu.VMEM((2, conv_rows, LANES), F32),
                        pltpu.VMEM(blk, F32), pltpu.VMEM(blk, F32),
                        pltpu.VMEM((WEIGHT_SLOTS, d, de), F32), pltpu.VMEM((WEIGHT_SLOTS, d, de), F32),
                        pltpu.VMEM((WEIGHT_SLOTS, de, d), F32),
                        pltpu.VMEM((d, de), BF16), pltpu.VMEM((d, de), BF16), pltpu.VMEM((de, d), BF16),
                        pltpu.SemaphoreType.DMA((2,)), pltpu.SemaphoreType.DMA((WEIGHT_SLOTS,))])
    return pl.pallas_call(
        _expert_kernel,
        grid_spec=grid_spec,
        out_shape=jax.ShapeDtypeStruct((n_tiles * MOE_TILE * SUBLANES, LANES), F32),
        compiler_params=pltpu.CompilerParams(dimension_semantics=("arbitrary",),
                                             vmem_limit_bytes=EXPERTS_VMEM_LIMIT),
        name="experts",
    )(*plan, row_tok, row_tok, h2, wg, wu, wd)


def _combine_kernel(posc_ref, posn_ref, ys_hbm, acc_ref, w_ref, mod_ref, g_ref, o_ref, buf, sem):
    i = pl.program_id(0)
    n = pl.num_programs(0)
    tc = acc_ref.shape[0]

    def start_row(pos_ref, slot, k, j, priority):
        pltpu.make_async_copy(_tile_rows(ys_hbm, pos_ref[0, k, j]), _tile_rows(buf.at[slot, k], j),
                              sem.at[slot]).start(priority=priority)

    @pl.when(i == 0)
    def _():
        for k in range(TOP_K):
            def body(jj, carry):
                for u in range(8):
                    start_row(posc_ref, 0, k, jj * 8 + u, u % 2)
                return carry
            lax.fori_loop(0, tc // 8, body, 0)

    @pl.when(i + 1 < n)
    def _():
        for k in range(TOP_K):
            for j in range(tc):
                start_row(posn_ref, (i + 1) % 2, k, j, j % 2)

    slot = i % 2
    for k in range(TOP_K):
        pltpu.make_async_copy(_tile_rows(ys_hbm, 0, tc), buf.at[slot, k], sem.at[slot]).wait()
    w = w_ref[...]
    routed = w[:, 0:1] * _tiles_to_rows(buf, tc, slot, 0)
    for k in range(1, TOP_K):
        routed = routed + w[:, k:k + 1] * _tiles_to_rows(buf, tc, slot, k)
    x2 = acc_ref[...] + mod_ref[0, 5:6, :] * routed
    o_ref[...] = x2 * lax.rsqrt(jnp.mean(x2 * x2, axis=-1, keepdims=True) + RMS_EPS) * g_ref[...]


def _tile_pos(pos_t, tc):
    return pos_t.reshape(TOP_K, pos_t.shape[1] // tc, tc).transpose(1, 0, 2)


def _combine(ys, pos3, acc0, w_tok, mod3, g_final, seq):
    t, d = acc0.shape
    tc = pos3.shape[2]
    nt = t // tc
    return pl.pallas_call(
        _combine_kernel,
        grid=(nt,),
        in_specs=[pl.BlockSpec((1, TOP_K, tc), lambda i: (i, 0, 0), memory_space=pltpu.SMEM),
                  pl.BlockSpec((1, TOP_K, tc), lambda i: (jnp.minimum(i + 1, nt - 1), 0, 0),
                               memory_space=pltpu.SMEM),
                  pl.BlockSpec(memory_space=pl.ANY),
                  pl.BlockSpec((tc, d), lambda i: (i, 0)),
                  pl.BlockSpec((tc, TOP_K), lambda i: (i, 0)),
                  pl.BlockSpec((1, 6, d), lambda i: ((i * tc) // seq, 0, 0)),
                  pl.BlockSpec((1, d), lambda i: (0, 0))],
        out_specs=pl.BlockSpec((tc, d), lambda i: (i, 0)),
        out_shape=jax.ShapeDtypeStruct((t, d), F32),
        scratch_shapes=[pltpu.VMEM((2, TOP_K, tc * SUBLANES, LANES), F32), pltpu.SemaphoreType.DMA((2,))],
        compiler_params=_cparams(("arbitrary",)),
        name="combine",
    )(pos3, pos3, ys, acc0, w_tok, mod3, g_final)


def kernel(x, c, positions, w_ada, b_ada, g_norm_mix, w_in, cmp_pos_k, cmp_w1_k, cmp_w2_k, cmp_pos_v,
           cmp_w1_v, cmp_w2_v, g_nsa_out, g_ret_out, w_out, g_norm_ffn, w_router, router_bias,
           w_gate_e, w_up_e, w_down_e, w_gate_s, w_up_s, w_down_s, g_norm_final):
    bsz, seq, d = x.shape
    assert d == SUBLANES * LANES, "MoE rows are moved as one (8, 128) tile each"
    t = bsz * seq
    x2 = x.reshape(t, d)
    cos_t, sin_t = _rope_tables(positions.reshape(t, 1).astype(F32))
    n_cmp_pad = seq // CMP_STRIDE

    def cmp_rows(tab):
        rows = tab[:, :LANES].reshape(bsz, seq, LANES)[:, CMP_BLOCK - 1::CMP_STRIDE]
        return jnp.pad(rows, ((0, 0), (0, n_cmp_pad - rows.shape[1]), (0, 0)))

    cos_c, sin_c = cmp_rows(cos_t), cmp_rows(sin_t)
    for l in range(w_in.shape[0]):
        mod3 = _ada(c, w_ada[l], b_ada[l]).reshape(bsz, 6, d)
        q, kc, vc, kk, gates, qr, kr, vr, gr, vsel_t, vwin_t = _inproj(
            x2, mod3, g_norm_mix[l].reshape(1, d), _arrange_w_in(w_in[l]), cos_t, sin_t, seq)
        kcmp, vcmp = _compress(kc, vc, cos_c, sin_c,
                               _arrange_cmp_weights(cmp_pos_k[l], cmp_w1_k[l], cmp_w2_k[l]),
                               _arrange_cmp_weights(cmp_pos_v[l], cmp_w1_v[l], cmp_w2_v[l]), bsz, seq)
        onsa = _nsa(q, kcmp, vcmp, kk, vsel_t, vwin_t, gates, g_nsa_out[l].reshape(1, -1), bsz, seq)
        oret = _retention(qr, kr, vr, gr, g_ret_out[l], bsz, seq)
        acc0, h2, idx_t, w_t = _post(x2, onsa, oret, mod3, w_out[l], g_norm_ffn[l].reshape(1, d),
                                     w_router[l], router_bias[l], w_gate_s[l], w_up_s[l], w_down_s[l], seq)
        pos_t, plan, row_tok = _route_plan(idx_t, t)
        ys = _experts(h2, row_tok, plan, w_gate_e[l], w_up_e[l], w_down_e[l])
        last = l == w_in.shape[0] - 1
        gfin = g_norm_final.reshape(1, d)
        x2 = _combine(ys, _tile_pos(pos_t, 128), acc0, w_t.T, mod3, gfin, seq)
        assert last, "final norm is fused into the combine stage; depth 1 only"
    return x2.reshape(bsz, seq, d)
```

```python
import functools
import math

import numpy as np
import jax
import jax.numpy as jnp
from jax import lax
from jax.experimental import pallas as pl
from jax.experimental.pallas import tpu as pltpu

F32 = jnp.float32
BF16 = jnp.bfloat16
I32 = jnp.int32
HIGHEST = lax.Precision.HIGHEST

LANES = 128
SUBLANES = 8
BF16_TILE_ROWS = 16
NSA_HEAD_DIM = 64
NSA_HEADS = 8
NSA_GROUPS = 2
NSA_Q_PER_KV = NSA_HEADS // NSA_GROUPS
CMP_BLOCK = 32
CMP_STRIDE = 16
CMP_HIDDEN = 128
SEL_BLOCK = 64
SEL_TOP = 8
WINDOW = 512
Q_BLOCK = 128
FORCE_BONUS = 1.0e4
RET_HEADS = 4
RET_HEAD_DIM = 128
RET_CHUNK = 128
ROPE_THETA = 10000.0
RMS_EPS = 1e-6
N_EXPERTS = 256
N_EXPERT_GROUPS = 8
TOPK_GROUPS = 4
TOP_K = 8
ROUTED_SCALE = 2.5
MOE_TILE = 256
WEIGHT_SLOTS = 3
SEL_CHUNK = 256
MASK_BIG = 2.0 ** 100
MASK_FLOOR = 2.0 ** 99
NEG = -1e30
VMEM_LIMIT = 48 * 1024 * 1024
EXPERTS_VMEM_LIMIT = 56 * 1024 * 1024


def _cparams(sem):
    return pltpu.CompilerParams(dimension_semantics=sem, vmem_limit_bytes=VMEM_LIMIT)


def _dot(a, b, **kw):
    return jnp.dot(a, b, preferred_element_type=F32, **kw)


def _dot_nt(a, b, **kw):
    return lax.dot_general(a, b, (((1,), (1,)), ((), ())), preferred_element_type=F32, **kw)


def _sigmoid(v):
    return 1.0 / (1.0 + jnp.exp(-v))


def _silu(v):
    return v * _sigmoid(v)


def _ada_kernel(c_ref, w_ref, b_ref, o_ref):
    o_ref[...] = _dot(_silu(c_ref[...]), w_ref[...], precision=HIGHEST) + b_ref[...]


def _ada(c, w, b):
    bsz, d = c.shape
    n = w.shape[1]
    tn = 1536
    cp = jnp.zeros((8, d), F32).at[:bsz].set(c)
    out = pl.pallas_call(
        _ada_kernel,
        grid=(n // tn,),
        in_specs=[pl.BlockSpec((8, d), lambda j: (0, 0)),
                  pl.BlockSpec((d, tn), lambda j: (0, j)),
                  pl.BlockSpec((1, tn), lambda j: (0, j))],
        out_specs=pl.BlockSpec((8, tn), lambda j: (0, j)),
        out_shape=jax.ShapeDtypeStruct((8, n), F32),
        compiler_params=_cparams(("arbitrary",)),
        name="ada",
    )(cp, w, b.reshape(1, n))
    return out[:bsz]


def _rope_table_kernel(pos_ref, inv_ref, sgn_ref, cos_ref, sin_ref):
    ang = pos_ref[...] * inv_ref[...]
    cos_ref[...] = jnp.cos(ang)
    sin_ref[...] = jnp.sin(ang) * sgn_ref[...]


def _rope_tables(posf):
    t = posf.shape[0]
    inv_n = ROPE_THETA ** (-jnp.arange(0, NSA_HEAD_DIM, 2, dtype=F32) / NSA_HEAD_DIM)
    inv_r = ROPE_THETA ** (-jnp.arange(0, RET_HEAD_DIM, 2, dtype=F32) / RET_HEAD_DIM)
    inv = jnp.concatenate([jnp.tile(inv_n, 4), jnp.tile(inv_r, 2)]).reshape(1, 2 * LANES)
    sgn_n = np.where((np.arange(LANES) % NSA_HEAD_DIM) < NSA_HEAD_DIM // 2, -1.0, 1.0)
    sgn_r = np.where(np.arange(LANES) < RET_HEAD_DIM // 2, -1.0, 1.0)
    sgn = jnp.asarray(np.concatenate([sgn_n, sgn_r]).reshape(1, 2 * LANES), F32)
    tm = min(t, 1024)
    return pl.pallas_call(
        _rope_table_kernel,
        grid=(t // tm,),
        in_specs=[pl.BlockSpec((tm, 1), lambda i: (i, 0)),
                  pl.BlockSpec((1, 2 * LANES), lambda i: (0, 0)),
                  pl.BlockSpec((1, 2 * LANES), lambda i: (0, 0))],
        out_specs=[pl.BlockSpec((tm, 2 * LANES), lambda i: (i, 0))] * 2,
        out_shape=[jax.ShapeDtypeStruct((t, 2 * LANES), F32)] * 2,
        compiler_params=_cparams(("arbitrary",)),
        name="rope_tables",
    )(posf, inv, sgn)


def _rope64(p, cos, sin, first_half):
    rot = jnp.where(first_half, pltpu.roll(p, 96, 1), pltpu.roll(p, 32, 1))
    return p * cos + rot * sin


def _rope128(p, cos, sin):
    return p * cos + pltpu.roll(p, 64, 1) * sin


_C_Q = 0
_C_KV = _C_Q + NSA_HEADS * LANES
_C_GATE = _C_KV + 6 * LANES
_C_RET = _C_GATE + LANES
_C_END = _C_RET + 4 * RET_HEADS * RET_HEAD_DIM


def _arrange_w_in(w_in):
    d = w_in.shape[0]
    nw = NSA_HEADS * NSA_HEAD_DIM
    q = w_in[:, :nw].reshape(d, NSA_HEADS, NSA_HEAD_DIM)
    z = jnp.zeros_like(q)
    grp = (jnp.arange(NSA_HEADS) // NSA_Q_PER_KV)[None, :, None]
    qpad = jnp.where(grp == 0, jnp.concatenate([q, z], -1), jnp.concatenate([z, q], -1))
    qpad = qpad.reshape(d, NSA_HEADS * LANES)
    kv = w_in[:, nw:nw + 6 * LANES]
    g0 = nw + 6 * LANES
    gate = jnp.pad(w_in[:, g0:g0 + 3 * NSA_HEADS], ((0, 0), (0, LANES - 3 * NSA_HEADS)))
    ret = w_in[:, g0 + 3 * NSA_HEADS:]
    return jnp.concatenate([qpad, kv, gate, ret], axis=1).astype(BF16)


def _inproj_kernel(x_ref, mod_ref, g_ref, w_ref, cos_ref, sin_ref,
                   q_ref, kc_ref, vc_ref, kk_ref, gate_ref, qr_ref, kr_ref, vr_ref, gr_ref, vst_ref, vwt_ref):
    x = x_ref[...]
    tm = x.shape[0]
    ms = jnp.mean(x * x, axis=-1, keepdims=True)
    y = x * lax.rsqrt(ms + RMS_EPS) * g_ref[...]
    h = y * (1.0 + mod_ref[0, 1:2, :]) + mod_ref[0, 0:1, :]
    hb = h.astype(BF16)
    cos_n, sin_n = cos_ref[:, 0:LANES], sin_ref[:, 0:LANES]
    cos_r, sin_r = cos_ref[:, LANES:], sin_ref[:, LANES:]
    lane = lax.broadcasted_iota(I32, (tm, LANES), 1)
    first_half = (lane % NSA_HEAD_DIM) < (NSA_HEAD_DIM // 2)
    scale_n = NSA_HEAD_DIM ** -0.5
    scale_r = RET_HEAD_DIM ** -0.5

    def proj(c0, n):
        return _dot(hb, w_ref[:, c0:c0 + n])

    for hh in range(NSA_HEADS):
        p = proj(_C_Q + hh * LANES, LANES)
        q_ref[:, hh * LANES:(hh + 1) * LANES] = (_rope64(p, cos_n, sin_n, first_half) * scale_n).astype(BF16)
    kv = proj(_C_KV, 6 * LANES)
    kc_ref[...] = kv[:, 0:LANES].astype(BF16)
    vc_ref[...] = kv[:, LANES:2 * LANES].astype(BF16)
    kk_ref[:, 0:LANES] = _rope64(kv[:, 2 * LANES:3 * LANES], cos_n, sin_n, first_half).astype(BF16)
    kk_ref[:, LANES:2 * LANES] = _rope64(kv[:, 4 * LANES:5 * LANES], cos_n, sin_n, first_half).astype(BF16)
    group0 = lane < NSA_HEAD_DIM
    for c0, vt_ref, chunk in ((3 * LANES, vst_ref, SEL_CHUNK), (5 * LANES, vwt_ref, Q_BLOCK)):
        v = kv[:, c0:c0 + LANES]
        for g, vg in enumerate((jnp.where(group0, v, 1.0), jnp.where(group0, 1.0, v))):
            for cc in range(tm // chunk):
                vt_ref[cc, g] = vg[cc * chunk:(cc + 1) * chunk].T.astype(BF16)
    gate_ref[...] = _sigmoid(proj(_C_GATE, LANES))
    rw = RET_HEADS * RET_HEAD_DIM
    for hh in range(RET_HEADS):
        sl = slice(hh * LANES, (hh + 1) * LANES)
        pq = proj(_C_RET + hh * LANES, LANES)
        qr_ref[:, sl] = _rope128(pq, cos_r, sin_r).astype(BF16)
        pk = proj(_C_RET + rw + hh * LANES, LANES)
        kr_ref[:, sl] = (_rope128(pk, cos_r, sin_r) * scale_r).astype(BF16)
    vr_ref[...] = proj(_C_RET + 2 * rw, rw).astype(BF16)
    gr_ref[...] = proj(_C_RET + 3 * rw, rw)


def _inproj(x2, mod3, g_mix, w_in_p, cos_t, sin_t, seq):
    t, d = x2.shape
    tm = min(512, seq)
    rw = RET_HEADS * RET_HEAD_DIM
    row = lambda n: pl.BlockSpec((tm, n), lambda i: (i, 0))
    outs = [(NSA_HEADS * LANES, BF16), (LANES, BF16), (LANES, BF16), (2 * LANES, BF16), (LANES, F32),
            (rw, BF16), (rw, BF16), (rw, BF16), (rw, F32)]
    vt_specs, vt_shapes = [], []
    for chunk in (SEL_CHUNK, Q_BLOCK):
        vt_specs.append(pl.BlockSpec((tm // chunk, NSA_GROUPS, LANES, chunk), lambda i: (i, 0, 0, 0)))
        vt_shapes.append(jax.ShapeDtypeStruct((t // chunk, NSA_GROUPS, LANES, chunk), BF16))
    return pl.pallas_call(
        _inproj_kernel,
        grid=(t // tm,),
        in_specs=[row(d),
                  pl.BlockSpec((1, 6, d), lambda i: ((i * tm) // seq, 0, 0)),
                  pl.BlockSpec((1, d), lambda i: (0, 0)),
                  pl.BlockSpec((d, _C_END), lambda i: (0, 0)),
                  row(2 * LANES), row(2 * LANES)],
        out_specs=[row(n) for n, _ in outs] + vt_specs,
        out_shape=[jax.ShapeDtypeStruct((t, n), dt) for n, dt in outs] + vt_shapes,
        compiler_params=_cparams(("arbitrary",)),
        name="inproj",
    )(x2, mod3, g_mix, w_in_p, cos_t, sin_t)


def _gelu_tanh(v):
    return v * (0.5 * (1.0 + jnp.tanh(math.sqrt(2.0 / math.pi) * (v + 0.044715 * (v * v * v)))))


def _compress_kernel(kc_ref, vc_ref, wkt_ref, wkb_ref, wk2_ref, pk_ref, wvt_ref, wvb_ref, wv2_ref, pv_ref,
                     cos_ref, sin_ref, ko_ref, vo_ref):
    def one(x_ref, wt_ref, wb_ref, w2_ref, p_ref):
        xx = x_ref[0]
        a = _dot(xx, wt_ref[...])
        b = _dot(xx, wb_ref[...])
        pb = p_ref[...].astype(BF16)
        bias = _dot(pb, wt_ref[...])[0:1] + _dot(pb, wb_ref[...])[1:2]
        hid = a + pltpu.roll(b, b.shape[0] - 1, 0) + bias
        return _dot(_gelu_tanh(hid).astype(BF16), w2_ref[...])

    k = one(kc_ref, wkt_ref, wkb_ref, wk2_ref, pk_ref)
    lane = lax.broadcasted_iota(I32, k.shape, 1)
    first_half = (lane % NSA_HEAD_DIM) < (NSA_HEAD_DIM // 2)
    ko_ref[0] = _rope64(k, cos_ref[0], sin_ref[0], first_half).astype(BF16)
    v = one(vc_ref, wvt_ref, wvb_ref, wv2_ref, pv_ref)
    group0 = lane < NSA_HEAD_DIM
    vo_ref[0, 0] = jnp.where(group0, v, 1.0).T.astype(BF16)
    vo_ref[0, 1] = jnp.where(group0, 1.0, v).T.astype(BF16)


def _arrange_cmp_weights(pos, w1, w2):
    half = CMP_BLOCK // 2
    dh = NSA_HEAD_DIM
    w1r = w1.reshape(CMP_BLOCK, dh, CMP_HIDDEN)

    def block(wpart):
        z = jnp.zeros_like(wpart)
        g0 = jnp.concatenate([wpart, z], axis=-1)
        g1 = jnp.concatenate([z, wpart], axis=-1)
        return jnp.stack([g0, g1], axis=1).reshape(half * 2 * dh, 2 * CMP_HIDDEN)

    wt, wb = block(w1r[:half]), block(w1r[half:])
    z2 = jnp.zeros_like(w2)
    w2b = jnp.concatenate([jnp.concatenate([w2, z2], 1), jnp.concatenate([z2, w2], 1)], 0)
    ptop = jnp.tile(pos[:half], (1, 2)).reshape(1, -1)
    pbot = jnp.tile(pos[half:], (1, 2)).reshape(1, -1)
    prow = jnp.concatenate([ptop, pbot, jnp.zeros((6, ptop.shape[1]), F32)], 0)
    return wt.astype(BF16), wb.astype(BF16), w2b.astype(BF16), prow


def _compress(kc, vc, cos_c, sin_c, kparams, vparams, bsz, seq):
    nchunk = seq // CMP_STRIDE
    width = CMP_STRIDE * LANES
    kc16 = kc.reshape(bsz, nchunk, width)
    vc16 = vc.reshape(bsz, nchunk, width)
    full = lambda a: pl.BlockSpec(a.shape, lambda b: (0,) * a.ndim)
    per_b = lambda n: pl.BlockSpec((1, nchunk, n), lambda b: (b, 0, 0))
    return pl.pallas_call(
        _compress_kernel,
        grid=(bsz,),
        in_specs=[per_b(width), per_b(width)] + [full(a) for a in kparams] + [full(a) for a in vparams]
                 + [per_b(LANES), per_b(LANES)],
        out_specs=[per_b(LANES), pl.BlockSpec((1, NSA_GROUPS, LANES, nchunk), lambda b: (b, 0, 0, 0))],
        out_shape=[jax.ShapeDtypeStruct((bsz, nchunk, LANES), BF16),
                   jax.ShapeDtypeStruct((bsz, NSA_GROUPS, LANES, nchunk), BF16)],
        compiler_params=_cparams(("arbitrary",)),
        name="compress",
    )(kc16, vc16, *kparams, *vparams, cos_c, sin_c)


def _softmax_chunk(m, acc, s, vt):
    m_new = jnp.maximum(m, jnp.max(s, axis=0, keepdims=True))
    e = jnp.exp((s - m_new).astype(BF16))
    acc = jnp.exp(m - m_new) * acc + _dot(vt, e)
    return m_new, acc


def _nsa_kernel(q_ref, kc_ref, vct_ref, ks_ref, kw_ref, vst_ref, vwt_ref, gate_ref,
                ovt_ref, exp_ref, g_ref, o_ref, *, seq):
    qb = pl.program_id(1)
    t0 = qb * Q_BLOCK
    cols = NSA_Q_PER_KV * Q_BLOCK
    n_cmp_pad = kc_ref.shape[1]
    n_sel = seq // SEL_BLOCK
    t_row = t0 + lax.broadcasted_iota(I32, (1, Q_BLOCK), 1)
    gates_t = gate_ref[...].T
    lane = lax.broadcasted_iota(I32, (Q_BLOCK, LANES), 1)
    key = lax.broadcasted_iota(I32, (Q_BLOCK, Q_BLOCK), 0)
    tok = lax.broadcasted_iota(I32, (Q_BLOCK, Q_BLOCK), 1)
    cend = lax.broadcasted_iota(I32, (n_cmp_pad, 1), 0) * CMP_STRIDE + (CMP_BLOCK - 1)
    bias_cmp = jnp.where(cend <= t_row, 0.0, -MASK_BIG)
    last_c = qb // (SEL_CHUNK // Q_BLOCK)
    kpos_last = last_c * SEL_CHUNK + lax.broadcasted_iota(I32, (SEL_CHUNK, 1), 0)
    bias_diag = jnp.where(kpos_last <= t_row, 0.0, -MASK_BIG)
    n_win = WINDOW // Q_BLOCK
    bias_win_first = jnp.where(key > tok, 0.0, -MASK_BIG)
    bias_win_last = jnp.where(key <= tok, 0.0, -MASK_BIG)
    init = (jnp.full((1, cols), -MASK_FLOOR, F32), jnp.zeros((LANES, cols), F32))
    tile_heads = lambda b: jnp.concatenate([b] * NSA_Q_PER_KV, axis=1)

    def finish(acc, g):
        ones_row = NSA_HEAD_DIM * (1 - g)
        return acc * (1.0 / jnp.maximum(acc[ones_row:ones_row + 1, :], 1e-20))

    q4s, o_cs, sel_bs = [], [], []
    for g in range(NSA_GROUPS):
        q4 = jnp.concatenate([q_ref[:, (NSA_Q_PER_KV * g + r) * LANES:(NSA_Q_PER_KV * g + r + 1) * LANES]
                              for r in range(NSA_Q_PER_KV)], axis=0)
        q4s.append(q4)
        s_c = _dot_nt(kc_ref[0], q4) + tile_heads(bias_cmp)
        e_c = jnp.exp(s_c - jnp.maximum(jnp.max(s_c, axis=0, keepdims=True), -MASK_FLOOR))
        p_c = e_c * (1.0 / jnp.maximum(jnp.sum(e_c, axis=0, keepdims=True), 1e-20))
        psum = p_c[:, 0:Q_BLOCK]
        for r in range(1, NSA_Q_PER_KV):
            psum = psum + p_c[:, r * Q_BLOCK:(r + 1) * Q_BLOCK]
        o_cs.append(finish(_dot(vct_ref[0, g], e_c.astype(BF16)), g))
        imp_t = _dot(ovt_ref[...], psum, precision=HIGHEST)
        nsp = imp_t.shape[0]
        jrow = lax.broadcasted_iota(I32, (nsp, 1), 0)
        cur = t_row // SEL_BLOCK
        valid = (jrow * SEL_BLOCK <= t_row) & (jrow < n_sel)
        forced = (jrow == 0) | (jrow == cur) | (jrow == cur - 1)
        val = jnp.where(valid, imp_t + jnp.where(forced, FORCE_BONUS, 0.0), -1.0)
        val = jnp.where(jrow < n_sel, val, -jnp.inf)
        sel_t = jnp.zeros((nsp, Q_BLOCK), F32)
        for _ in range(min(SEL_TOP, n_sel)):
            mx = jnp.max(val, axis=0, keepdims=True)
            jmin = jnp.min(jnp.where(val == mx, jrow, nsp), axis=0, keepdims=True)
            hit = jrow == jmin
            sel_t = jnp.where(hit, 1.0, sel_t)
            val = jnp.where(hit, -jnp.inf, val)
        sel_bs.append(jnp.concatenate([sel_t[:n_sel], jnp.ones((SUBLANES, Q_BLOCK), F32),
                                       jnp.zeros((LANES - n_sel - SUBLANES, Q_BLOCK), F32)], axis=0).astype(BF16))

    def sel_scores(c):
        keys = ks_ref[pl.ds(pl.multiple_of(c * SEL_CHUNK, SEL_CHUNK), SEL_CHUNK), :]
        expand = exp_ref[c]
        return tuple(_dot_nt(keys, q4s[g]) + tile_heads(_dot(expand, sel_bs[g])) for g in range(NSA_GROUPS))

    def sel_reduce(c, state, scores):
        out = []
        for g in range(NSA_GROUPS):
            out.extend(_softmax_chunk(state[2 * g], state[2 * g + 1], scores[g], vst_ref[c, g]))
        return tuple(out)

    def sel_step(c, carry):
        state, scores = carry
        return sel_reduce(c, state, scores), sel_scores(c + 1)

    state, scores = lax.fori_loop(0, last_c, sel_step, (init * NSA_GROUPS, sel_scores(0)))
    diag = tile_heads(bias_diag)
    state = sel_reduce(last_c, state, tuple(s + diag for s in scores))
    o_ss = [finish(state[2 * g + 1], g) for g in range(NSA_GROUPS)]
    s_ws = [[] for _ in range(NSA_GROUPS)]
    kbs = []
    for w in range(n_win + 1):
        wb = qb - n_win + w
        kb = jnp.maximum(wb, 0)
        kbs.append(kb)
        keys = kw_ref[pl.ds(pl.multiple_of(kb * Q_BLOCK, Q_BLOCK), Q_BLOCK), :]
        if w == n_win:
            bias = bias_win_last
        else:
            before_start = jnp.where(wb >= 0, 0.0, -MASK_BIG)
            bias = (bias_win_first + before_start) if w == 0 else jnp.full((Q_BLOCK, Q_BLOCK), before_start)
        for g in range(NSA_GROUPS):
            s_ws[g].append(_dot_nt(keys, q4s[g]) + tile_heads(bias))
    o_ws = []
    for g in range(NSA_GROUPS):
        m_w = jnp.full((1, cols), -MASK_FLOOR, F32)
        for s in s_ws[g]:
            m_w = jnp.maximum(m_w, jnp.max(s, axis=0, keepdims=True))
        acc = None
        for kb, s in zip(kbs, s_ws[g]):
            pv = _dot(vwt_ref[kb, g], jnp.exp((s - m_w).astype(BF16)))
            acc = pv if acc is None else acc + pv
        o_ws.append(finish(acc, g))
    heads_out = []
    for g in range(NSA_GROUPS):
        o_c, o_s, o_w = o_cs[g], o_ss[g], o_ws[g]
        for r in range(NSA_Q_PER_KV):
            hh = NSA_Q_PER_KV * g + r
            cs = slice(r * Q_BLOCK, (r + 1) * Q_BLOCK)
            o = (gates_t[3 * hh:3 * hh + 1, :] * o_c[:, cs] + gates_t[3 * hh + 1:3 * hh + 2, :] * o_s[:, cs]
                 + gates_t[3 * hh + 2:3 * hh + 3, :] * o_w[:, cs]).T
            if (hh % 2) != g:
                o = pltpu.roll(o, NSA_HEAD_DIM, 1)
            heads_out.append(o)
    blocks = [jnp.where(lane < NSA_HEAD_DIM, heads_out[2 * i], heads_out[2 * i + 1])
              for i in range(NSA_HEADS // 2)]
    ss = sum(jnp.sum(b * b, axis=-1, keepdims=True) for b in blocks)
    inv = lax.rsqrt(ss / (NSA_HEADS * NSA_HEAD_DIM) + RMS_EPS)
    for i, b in enumerate(blocks):
        sl = slice(i * LANES, (i + 1) * LANES)
        o_ref[:, sl] = (b * inv * g_ref[:, sl]).astype(BF16)


def _nsa(q, kcmp, vcmp_t, kk, vsel_t, vwin_t, gates, g_nsa, bsz, seq):
    t = bsz * seq
    nq = seq // Q_BLOCK
    n_cmp_pad = seq // CMP_STRIDE
    n_cmp = (seq - CMP_BLOCK) // CMP_STRIDE + 1
    n_sel = seq // SEL_BLOCK
    assert n_sel % SUBLANES == 0 and n_sel + SUBLANES <= LANES, "selection mask needs a spare expansion row"
    nsp = n_sel
    cs = np.arange(n_cmp_pad) * CMP_STRIDE
    ss = np.arange(nsp) * SEL_BLOCK
    ov = ((cs[None, :] < ss[:, None] + SEL_BLOCK) & (cs[None, :] + CMP_BLOCK > ss[:, None])
          & (np.arange(n_cmp_pad)[None, :] < n_cmp) & (np.arange(nsp)[:, None] < n_sel))
    ovt = jnp.asarray(ov, F32)
    nch = seq // SEL_CHUNK
    kp = np.arange(seq).reshape(nch, SEL_CHUNK, 1)
    col = np.arange(LANES).reshape(1, 1, LANES)
    expand = jnp.asarray(np.where(kp // SEL_BLOCK == col, MASK_BIG, 0.0) + np.where(col == n_sel, -MASK_BIG, 0.0),
                         BF16)
    seqcol = lambda c: pl.BlockSpec((seq, LANES), lambda b, i: (b, c))
    per_b = lambda a: pl.BlockSpec((a.shape[0] // bsz,) + a.shape[1:], lambda b, i: (b,) + (0,) * (a.ndim - 1))
    return pl.pallas_call(
        functools.partial(_nsa_kernel, seq=seq),
        grid=(bsz, nq),
        in_specs=[pl.BlockSpec((Q_BLOCK, NSA_HEADS * LANES), lambda b, i: (b * nq + i, 0)),
                  per_b(kcmp), per_b(vcmp_t), seqcol(0), seqcol(1), per_b(vsel_t), per_b(vwin_t),
                  pl.BlockSpec((Q_BLOCK, LANES), lambda b, i: (b * nq + i, 0)),
                  pl.BlockSpec(ovt.shape, lambda b, i: (0, 0)),
                  pl.BlockSpec(expand.shape, lambda b, i: (0, 0, 0)),
                  pl.BlockSpec((1, NSA_HEADS * NSA_HEAD_DIM), lambda b, i: (0, 0))],
        out_specs=pl.BlockSpec((Q_BLOCK, NSA_HEADS * NSA_HEAD_DIM), lambda b, i: (b * nq + i, 0)),
        out_shape=jax.ShapeDtypeStruct((t, NSA_HEADS * NSA_HEAD_DIM), BF16),
        compiler_params=_cparams(("arbitrary", "arbitrary")),
        name="nsa",
    )(q, kcmp, vcmp_t, kk, kk, vsel_t, vwin_t, gates, ovt, expand, g_nsa)


def _ret_kernel(q_ref, k_ref, v_ref, gr_ref, dec_ref, xi_ref, zeta_ref, gch_ref, g_ref, o_ref, st_ref):
    @pl.when(pl.program_id(1) == 0)
    def _():
        st_ref[...] = jnp.zeros_like(st_ref)

    for cc in range(q_ref.shape[0] // RET_CHUNK):
        rs = slice(cc * RET_CHUNK, (cc + 1) * RET_CHUNK)
        for hh in range(RET_HEADS):
            sl = slice(hh * LANES, (hh + 1) * LANES)
            q, k, v = q_ref[rs, sl], k_ref[rs, sl], v_ref[rs, sl]
            sc = _dot_nt(q, k) * dec_ref[hh]
            inner = _dot(sc.astype(BF16), v)
            st = st_ref[hh]
            cross = _dot((q.astype(F32) * xi_ref[hh]).astype(BF16), st.astype(BF16))
            kz = (k.astype(F32) * zeta_ref[hh]).T.astype(BF16)
            st_ref[hh] = st * gch_ref[hh] + _dot(kz, v)
            o = inner + cross
            y = o * lax.rsqrt(jnp.mean(o * o, axis=-1, keepdims=True) + RMS_EPS) * g_ref[hh:hh + 1, :]
            o_ref[rs, sl] = (_silu(gr_ref[rs, sl]) * y).astype(BF16)


def _retention(qr, kr, vr, gr, g_ret, bsz, seq):
    t = bsz * seq
    c = RET_CHUNK
    n = seq // c
    log_g = jnp.log(1.0 - 2.0 ** (-5.0 - jnp.arange(RET_HEADS, dtype=F32)))
    i = jnp.arange(c, dtype=F32)
    diff = i[:, None] - i[None, :]
    causal = diff >= 0
    dec = jnp.where(causal, jnp.exp(log_g[:, None, None] * jnp.where(causal, diff, 0.0)), 0.0)
    xi = jnp.broadcast_to(jnp.exp(log_g[:, None] * (i + 1.0))[:, :, None], (RET_HEADS, c, LANES))
    zeta = jnp.broadcast_to(jnp.exp(log_g[:, None] * (c - 1.0 - i))[:, :, None], (RET_HEADS, c, LANES))
    gch = jnp.broadcast_to(jnp.exp(log_g * c)[:, None, None], (RET_HEADS, 1, LANES))
    w = RET_HEADS * RET_HEAD_DIM
    per_step = 4 if n % 4 == 0 else 1
    steps = n // per_step
    row = pl.BlockSpec((c * per_step, w), lambda b, j: (b * steps + j, 0))
    full = lambda a: pl.BlockSpec(a.shape, lambda b, j: (0,) * a.ndim)
    return pl.pallas_call(
        _ret_kernel,
        grid=(bsz, steps),
        in_specs=[row, row, row, row, full(dec), full(xi), full(zeta), full(gch), full(g_ret)],
        out_specs=row,
        out_shape=jax.ShapeDtypeStruct((t, w), BF16),
        scratch_shapes=[pltpu.VMEM((RET_HEADS, RET_HEAD_DIM, RET_HEAD_DIM), F32)],
        compiler_params=_cparams(("arbitrary", "arbitrary")),
        name="retention",
    )(qr, kr, vr, gr, dec, xi, zeta, gch, g_ret)


def _post_kernel(x_ref, onsa_ref, oret_ref, mod_ref, wo1_ref, wo2_ref, gffn_ref, wrt_ref, rb_ref,
                 wgs_ref, wus_ref, wds_ref, acc_ref, h2_ref, idx_ref, wt_ref):
    mix = _dot(onsa_ref[...], wo1_ref[...]) + _dot(oret_ref[...], wo2_ref[...])
    x1 = x_ref[...] + mod_ref[0, 2:3, :] * mix
    ms = jnp.mean(x1 * x1, axis=-1, keepdims=True)
    h2 = x1 * lax.rsqrt(ms + RMS_EPS) * gffn_ref[...] * (1.0 + mod_ref[0, 4:5, :]) + mod_ref[0, 3:4, :]
    _rows_to_tiles(h2_ref, h2)
    hb = h2.astype(BF16)
    hid = (_silu(_dot(hb, wgs_ref[...])) * _dot(hb, wus_ref[...])).astype(BF16)
    acc_ref[...] = x1 + mod_ref[0, 5:6, :] * _dot(hid, wds_ref[...])
    s = _sigmoid(_dot_nt(wrt_ref[...], h2, precision=HIGHEST))
    sb = s + rb_ref[...]
    per = N_EXPERTS // N_EXPERT_GROUPS
    ridx = lax.broadcasted_iota(I32, (per, 1), 0)
    blks, grp = [], []
    for gi in range(N_EXPERT_GROUPS):
        blk = sb[gi * per:(gi + 1) * per]
        m1 = jnp.max(blk, axis=0, keepdims=True)
        first = jnp.min(jnp.where(blk == m1, ridx, per), axis=0, keepdims=True)
        m2 = jnp.max(jnp.where(ridx == first, -jnp.inf, blk), axis=0, keepdims=True)
        blks.append(blk)
        grp.append(m1 + m2)
    masked = []
    for gi in range(N_EXPERT_GROUPS):
        rank = jnp.zeros_like(grp[gi])
        for gj in range(N_EXPERT_GROUPS):
            if gj < gi:
                rank = rank + (grp[gj] >= grp[gi]).astype(F32)
            elif gj > gi:
                rank = rank + (grp[gj] > grp[gi]).astype(F32)
        masked.append(jnp.where(rank < TOPK_GROUPS, blks[gi], NEG))
    val = jnp.concatenate(masked, axis=0)
    eidx = lax.broadcasted_iota(I32, (N_EXPERTS, 1), 0)
    ids, ws = [], []
    for _ in range(TOP_K):
        mx = jnp.max(val, axis=0, keepdims=True)
        emin = jnp.min(jnp.where(val == mx, eidx, N_EXPERTS), axis=0, keepdims=True)
        hit = eidx == emin
        ids.append(emin)
        ws.append(jnp.sum(jnp.where(hit, s, 0.0), axis=0, keepdims=True))
        val = jnp.where(hit, -jnp.inf, val)
    wsum = ws[0]
    for wk in ws[1:]:
        wsum = wsum + wk
    idx_ref[...] = jnp.concatenate(ids, axis=0)
    wt_ref[...] = jnp.concatenate(ws, axis=0) / wsum * ROUTED_SCALE


def _post(x2, onsa, oret, mod3, w_out, g_ffn, w_router, router_bias, wgs, wus, wds, seq):
    t, d = x2.shape
    tm = min(512, seq)
    hw = onsa.shape[1]
    wo1 = w_out[:hw].astype(BF16)
    wo2 = w_out[hw:].astype(BF16)
    wrt = w_router.T
    rb = jnp.broadcast_to(router_bias.reshape(N_EXPERTS, 1), (N_EXPERTS, tm))
    row = lambda n: pl.BlockSpec((tm, n), lambda i: (i, 0))
    full = lambda a: pl.BlockSpec(a.shape, lambda i: (0,) * a.ndim)
    col = pl.BlockSpec((TOP_K, tm), lambda i: (0, i))
    ops = (wo1, wo2, g_ffn, wrt, rb, wgs.astype(BF16), wus.astype(BF16), wds.astype(BF16))
    return pl.pallas_call(
        _post_kernel,
        grid=(t // tm,),
        in_specs=[row(d), row(hw), row(oret.shape[1]),
                  pl.BlockSpec((1, 6, d), lambda i: ((i * tm) // seq, 0, 0))] + [full(a) for a in ops],
        out_specs=[row(d), pl.BlockSpec((tm * SUBLANES, LANES), lambda i: (i, 0)), col, col],
        out_shape=[jax.ShapeDtypeStruct((t, d), F32), jax.ShapeDtypeStruct((t * SUBLANES, LANES), F32),
                   jax.ShapeDtypeStruct((TOP_K, t), I32), jax.ShapeDtypeStruct((TOP_K, t), F32)],
        compiler_params=_cparams(("arbitrary",)),
        name="post",
    )(x2, onsa, oret, mod3, *ops)


def _expert_select(idx_row, table, eidx):
    return jnp.sum(jnp.where(eidx == idx_row, table, 0.0), axis=0, keepdims=True)


def _rank_kernel(idx_ref, tri_ref, rank_ref, cnt_ref, carry):
    @pl.when(pl.program_id(0) == 0)
    def _():
        carry[...] = jnp.zeros_like(carry)

    idx = idx_ref[...]
    eidx = lax.broadcasted_iota(I32, (N_EXPERTS, 1), 0)
    member = jnp.zeros((N_EXPERTS, idx.shape[1]), F32)
    for k in range(TOP_K):
        member = member + (eidx == idx[k:k + 1, :]).astype(F32)
    before = _dot(member.astype(BF16), tri_ref[...]) + carry[:, 0:1]
    rank_ref[...] = jnp.concatenate(
        [_expert_select(idx[k:k + 1, :], before, eidx) for k in range(TOP_K)], axis=0).astype(I32)
    carry[...] = carry[...] + jnp.sum(member, axis=1, keepdims=True)
    cnt_ref[...] = carry[...]


def _dest_kernel(idx_ref, rank_ref, start_ref, pos_ref):
    idx = idx_ref[...]
    eidx = lax.broadcasted_iota(I32, (N_EXPERTS, 1), 0)
    start = start_ref[:, 0:1]
    base = jnp.concatenate([_expert_select(idx[k:k + 1, :], start, eidx) for k in range(TOP_K)], axis=0)
    pos_ref[...] = rank_ref[...] + base.astype(I32)


def _route_plan(idx_t, n_tok):
    tm = min(512, n_tok)
    tri = jnp.asarray(np.triu(np.ones((tm, tm), np.float32), 1), BF16)
    col = pl.BlockSpec((TOP_K, tm), lambda i: (0, i))
    rank, cnt = pl.pallas_call(
        _rank_kernel,
        grid=(n_tok // tm,),
        in_specs=[col, pl.BlockSpec((tm, tm), lambda i: (0, 0))],
        out_specs=[col, pl.BlockSpec((N_EXPERTS, LANES), lambda i: (0, 0))],
        out_shape=[jax.ShapeDtypeStruct((TOP_K, n_tok), I32), jax.ShapeDtypeStruct((N_EXPERTS, LANES), F32)],
        scratch_shapes=[pltpu.VMEM((N_EXPERTS, LANES), F32)],
        compiler_params=_cparams(("arbitrary",)),
        name="route_rank",
    )(idx_t, tri)
    counts = cnt[:, 0].astype(I32)
    padded = ((counts + MOE_TILE - 1) // MOE_TILE) * MOE_TILE
    ends = jnp.cumsum(padded)
    starts = ends - padded
    start_b = jnp.broadcast_to(starts.astype(F32)[:, None], (N_EXPERTS, LANES))
    pos = pl.pallas_call(
        _dest_kernel,
        grid=(n_tok // tm,),
        in_specs=[col, col, pl.BlockSpec((N_EXPERTS, LANES), lambda i: (0, 0))],
        out_specs=col,
        out_shape=jax.ShapeDtypeStruct((TOP_K, n_tok), I32),
        compiler_params=_cparams(("arbitrary",)),
        name="route_dest",
    )(idx_t, rank, start_b)
    n_tiles = TOP_K * n_tok // MOE_TILE + N_EXPERTS
    n_used = (ends[-1] // MOE_TILE).astype(I32)
    tile_start = jnp.minimum(jnp.arange(n_tiles, dtype=I32), n_used - 1) * MOE_TILE
    tile_e = jnp.minimum(jnp.sum((ends[None, :] <= tile_start[:, None]).astype(I32), axis=1), N_EXPERTS - 1)
    first = jnp.concatenate([jnp.ones((1,), I32), (tile_e[1:] != tile_e[:-1]).astype(I32)])
    eidx = jnp.arange(N_EXPERTS, dtype=I32)
    used = counts > 0
    rank = jnp.cumsum(used.astype(I32)) - 1

    def later(j):
        hit = used[None, :] & (rank[None, :] == rank[:, None] + j)
        return jnp.max(jnp.where(hit, eidx[None, :], -1), axis=1).astype(I32)

    plan = (tile_e, first, n_used.reshape(1), later(1), later(WEIGHT_SLOTS - 1), (rank % WEIGHT_SLOTS).astype(I32))
    slot = jnp.arange(MOE_TILE, dtype=I32)[None, :]
    rem = (counts % MOE_TILE)[:, None]
    n_rows = n_tiles * MOE_TILE
    pad_key = jnp.where((rem != 0) & (slot >= rem),
                        (starts + (counts // MOE_TILE) * MOE_TILE)[:, None] + slot, n_rows)
    keys = jnp.concatenate([pos.reshape(-1), pad_key.reshape(-1).astype(I32)]).astype(jnp.uint32)
    toks = jnp.concatenate([jnp.tile(jnp.arange(n_tok, dtype=jnp.uint32), TOP_K),
                            jnp.zeros((N_EXPERTS * MOE_TILE,), jnp.uint32)])
    assert (n_rows + 1) * n_tok <= 2 ** 32, "plan row and token id share one 32-bit sort word"
    packed = lax.sort(keys * jnp.uint32(n_tok) + toks)
    row_tok = (packed % jnp.uint32(n_tok)).astype(I32).reshape(n_tiles, 1, MOE_TILE)
    return pos, plan, row_tok


def _rows_to_tiles(ref, val):
    n = val.shape[0]
    for s in range(SUBLANES):
        ref[pl.ds(s, n, stride=SUBLANES), :] = val[:, s * LANES:(s + 1) * LANES]


def _tiles_to_rows(ref, n, *lead):
    return jnp.concatenate([ref[(*lead, pl.ds(s, n, stride=SUBLANES), slice(None))] for s in range(SUBLANES)],
                           axis=1)


def _tile_rows(ref, row, n=1):
    start = row * SUBLANES if isinstance(row, int) else pl.multiple_of(row * SUBLANES, SUBLANES)
    return ref.at[pl.ds(start, n * SUBLANES), :]


def _gather_row(hbuf, tok_ref, stage, j):
    t = tok_ref[0, 0, j]
    pair = hbuf[pl.ds(pl.multiple_of((t >> 1) * BF16_TILE_ROWS, BF16_TILE_ROWS), BF16_TILE_ROWS), :].astype(F32)
    start = j * SUBLANES if isinstance(j, int) else pl.multiple_of(j * SUBLANES, SUBLANES)
    stage[pl.ds(start, SUBLANES), :] = jnp.where((t & 1) == 1, pair[SUBLANES:], pair[:SUBLANES])


def _expert_kernel(tile_e, first, n_used, next_e, ahead_e, wslot, tokc_ref, tokn_ref, h2_hbm, wg_hbm, wu_hbm,
                   wd_hbm, y_ref, hbuf, conv, stage_a, stage_b, wgf, wuf, wdf, wgb, wub, wdb, hsem, sem):
    i = pl.program_id(0)
    n = n_used[0]

    def weight_copies(e):
        slot = wslot[e]
        return [pltpu.make_async_copy(src.at[e], dst.at[slot], sem.at[slot])
                for src, dst in ((wg_hbm, wgf), (wu_hbm, wuf), (wd_hbm, wdf))]

    @pl.when(i == 0)
    def _():
        e = tile_e[0]
        for _ in range(WEIGHT_SLOTS - 1):
            @pl.when(e >= 0)
            def _():
                for cp in weight_copies(e):
                    cp.start()
            e = jnp.where(e >= 0, next_e[jnp.maximum(e, 0)], -1)
        rows = conv.shape[1]
        n_conv = h2_hbm.shape[0] // rows

        def chunk_copy(c, slot):
            return pltpu.make_async_copy(h2_hbm.at[pl.ds(pl.multiple_of(c * rows, rows), rows), :], conv.at[slot],
                                         hsem.at[slot])

        chunk_copy(0, 0).start()

        def convert(c, carry):
            slot = c % 2
            chunk_copy(c, slot).wait()

            @pl.when(c + 1 < n_conv)
            def _():
                chunk_copy(c + 1, 1 - slot).start()

            hbuf[pl.ds(pl.multiple_of(c * rows, rows), rows), :] = conv[slot].astype(BF16)
            return carry
        lax.fori_loop(0, n_conv, convert, 0)

        def first_tile(jj, carry):
            for u in range(SUBLANES):
                _gather_row(hbuf, tokc_ref, stage_a, jj * SUBLANES + u)
            return carry
        lax.fori_loop(0, MOE_TILE // SUBLANES, first_tile, 0)

    @pl.when(i < n)
    def _():
        @pl.when(first[i] == 1)
        def _():
            e = tile_e[i]
            slot = wslot[e]
            for cp in weight_copies(e):
                cp.wait()

            @pl.when(ahead_e[e] >= 0)
            def _():
                for cp in weight_copies(ahead_e[e]):
                    cp.start()

            wgb[...] = wgf[slot].astype(BF16)
            wub[...] = wuf[slot].astype(BF16)
            wdb[...] = wdf[slot].astype(BF16)

        for par, (cur, nxt) in enumerate(((stage_a, stage_b), (stage_b, stage_a))):
            @pl.when(i % 2 == par)
            def _():
                for j in range(MOE_TILE):
                    _gather_row(hbuf, tokn_ref, nxt, j)
                xb = _tiles_to_rows(cur, MOE_TILE).astype(BF16)
                hid = (_silu(_dot(xb, wgb[...])) * _dot(xb, wub[...])).astype(BF16)
                _rows_to_tiles(y_ref, _dot(hid, wdb[...]))

    @pl.when(i >= n)
    def _():
        y_ref[...] = jnp.zeros_like(y_ref)


def _experts(h2, row_tok, plan, wg, wu, wd):
    n_tiles = row_tok.shape[0]
    d, de = wg.shape[1], wg.shape[2]
    blk = (MOE_TILE * SUBLANES, LANES)
    conv_rows = min(4096, h2.shape[0])
    assert h2.shape[0] % conv_rows == 0 and h2.shape[0] % BF16_TILE_ROWS == 0
    hbm = pl.BlockSpec(memory_space=pl.ANY)
    grid_spec = pltpu.PrefetchScalarGridSpec(
        num_scalar_prefetch=len(plan),
        grid=(n_tiles,),
        in_specs=[pl.BlockSpec((1, 1, MOE_TILE), lambda i, *p: (jnp.minimum(i, p[2][0] - 1), 0, 0),
                               memory_space=pltpu.SMEM),
                  pl.BlockSpec((1, 1, MOE_TILE), lambda i, *p: (jnp.minimum(i + 1, p[2][0] - 1), 0, 0),
                               memory_space=pltpu.SMEM), hbm, hbm, hbm, hbm],
        out_specs=pl.BlockSpec(blk, lambda i, *p: (i, 0)),
        scratch_shapes=[pltpu.VMEM(h2.shape, BF16), pltpu.VMEM((2, conv_rows, LANES), F32),
                        pltpu.VMEM(blk, F32), pltpu.VMEM(blk, F32),
                        pltpu.VMEM((WEIGHT_SLOTS, d, de), F32), pltpu.VMEM((WEIGHT_SLOTS, d, de), F32),
                        pltpu.VMEM((WEIGHT_SLOTS, de, d), F32),
                        pltpu.VMEM((d, de), BF16), pltpu.VMEM((d, de), BF16), pltpu.VMEM((de, d), BF16),
                        pltpu.SemaphoreType.DMA((2,)), pltpu.SemaphoreType.DMA((WEIGHT_SLOTS,))])
    return pl.pallas_call(
        _expert_kernel,
        grid_spec=grid_spec,
        out_shape=jax.ShapeDtypeStruct((n_tiles * MOE_TILE * SUBLANES, LANES), F32),
        compiler_params=pltpu.CompilerParams(dimension_semantics=("arbitrary",),
                                             vmem_limit_bytes=EXPERTS_VMEM_LIMIT),
        name="experts",
    )(*plan, row_tok, row_tok, h2, wg, wu, wd)


def _combine_kernel(posc_ref, posn_ref, ys_hbm, acc_ref, w_ref, mod_ref, g_ref, o_ref, buf, sem):
    i = pl.program_id(0)
    n = pl.num_programs(0)
    tc = acc_ref.shape[0]

    def start_row(pos_ref, slot, k, j, priority):
        pltpu.make_async_copy(_tile_rows(ys_hbm, pos_ref[0, k, j]), _tile_rows(buf.at[slot, k], j),
                              sem.at[slot]).start(priority=priority)

    @pl.when(i == 0)
    def _():
        for k in range(TOP_K):
            def body(jj, carry):
                for u in range(8):
                    start_row(posc_ref, 0, k, jj * 8 + u, u % 2)
                return carry
            lax.fori_loop(0, tc // 8, body, 0)

    @pl.when(i + 1 < n)
    def _():
        for k in range(TOP_K):
            for j in range(tc):
                start_row(posn_ref, (i + 1) % 2, k, j, j % 2)

    slot = i % 2
    for k in range(TOP_K):
        pltpu.make_async_copy(_tile_rows(ys_hbm, 0, tc), buf.at[slot, k], sem.at[slot]).wait()
    w = w_ref[...]
    routed = w[:, 0:1] * _tiles_to_rows(buf, tc, slot, 0)
    for k in range(1, TOP_K):
        routed = routed + w[:, k:k + 1] * _tiles_to_rows(buf, tc, slot, k)
    x2 = acc_ref[...] + mod_ref[0, 5:6, :] * routed
    o_ref[...] = x2 * lax.rsqrt(jnp.mean(x2 * x2, axis=-1, keepdims=True) + RMS_EPS) * g_ref[...]


def _tile_pos(pos_t, tc):
    return pos_t.reshape(TOP_K, pos_t.shape[1] // tc, tc).transpose(1, 0, 2)


def _combine(ys, pos3, acc0, w_tok, mod3, g_final, seq):
    t, d = acc0.shape
    tc = pos3.shape[2]
    nt = t // tc
    return pl.pallas_call(
        _combine_kernel,
        grid=(nt,),
        in_specs=[pl.BlockSpec((1, TOP_K, tc), lambda i: (i, 0, 0), memory_space=pltpu.SMEM),
                  pl.BlockSpec((1, TOP_K, tc), lambda i: (jnp.minimum(i + 1, nt - 1), 0, 0),
                               memory_space=pltpu.SMEM),
                  pl.BlockSpec(memory_space=pl.ANY),
                  pl.BlockSpec((tc, d), lambda i: (i, 0)),
                  pl.BlockSpec((tc, TOP_K), lambda i: (i, 0)),
                  pl.BlockSpec((1, 6, d), lambda i: ((i * tc) // seq, 0, 0)),
                  pl.BlockSpec((1, d), lambda i: (0, 0))],
        out_specs=pl.BlockSpec((tc, d), lambda i: (i, 0)),
        out_shape=jax.ShapeDtypeStruct((t, d), F32),
        scratch_shapes=[pltpu.VMEM((2, TOP_K, tc * SUBLANES, LANES), F32), pltpu.SemaphoreType.DMA((2,))],
        compiler_params=_cparams(("arbitrary",)),
        name="combine",
    )(pos3, pos3, ys, acc0, w_tok, mod3, g_final)


def kernel(x, c, positions, w_ada, b_ada, g_norm_mix, w_in, cmp_pos_k, cmp_w1_k, cmp_w2_k, cmp_pos_v,
           cmp_w1_v, cmp_w2_v, g_nsa_out, g_ret_out, w_out, g_norm_ffn, w_router, router_bias,
           w_gate_e, w_up_e, w_down_e, w_gate_s, w_up_s, w_down_s, g_norm_final):
    bsz, seq, d = x.shape
    assert d == SUBLANES * LANES, "MoE rows are moved as one (8, 128) tile each"
    t = bsz * seq
    x2 = x.reshape(t, d)
    cos_t, sin_t = _rope_tables(positions.reshape(t, 1).astype(F32))
    n_cmp_pad = seq // CMP_STRIDE

    def cmp_rows(tab):
        rows = tab[:, :LANES].reshape(bsz, seq, LANES)[:, CMP_BLOCK - 1::CMP_STRIDE]
        return jnp.pad(rows, ((0, 0), (0, n_cmp_pad - rows.shape[1]), (0, 0)))

    cos_c, sin_c = cmp_rows(cos_t), cmp_rows(sin_t)
    for l in range(w_in.shape[0]):
        mod3 = _ada(c, w_ada[l], b_ada[l]).reshape(bsz, 6, d)
        q, kc, vc, kk, gates, qr, kr, vr, gr, vsel_t, vwin_t = _inproj(
            x2, mod3, g_norm_mix[l].reshape(1, d), _arrange_w_in(w_in[l]), cos_t, sin_t, seq)
        kcmp, vcmp = _compress(kc, vc, cos_c, sin_c,
                               _arrange_cmp_weights(cmp_pos_k[l], cmp_w1_k[l], cmp_w2_k[l]),
                               _arrange_cmp_weights(cmp_pos_v[l], cmp_w1_v[l], cmp_w2_v[l]), bsz, seq)
        onsa = _nsa(q, kcmp, vcmp, kk, vsel_t, vwin_t, gates, g_nsa_out[l].reshape(1, -1), bsz, seq)
        oret = _retention(qr, kr, vr, gr, g_ret_out[l], bsz, seq)
        acc0, h2, idx_t, w_t = _post(x2, onsa, oret, mod3, w_out[l], g_norm_ffn[l].reshape(1, d),
                                     w_router[l], router_bias[l], w_gate_s[l], w_up_s[l], w_down_s[l], seq)
        pos_t, plan, row_tok = _route_plan(idx_t, t)
        ys = _experts(h2, row_tok, plan, w_gate_e[l], w_up_e[l], w_down_e[l])
        last = l == w_in.shape[0] - 1
        gfin = g_norm_final.reshape(1, d)
        x2 = _combine(ys, _tile_pos(pos_t, 128), acc0, w_t.T, mod3, gfin, seq)
        assert last, "final norm is fused into the combine stage; depth 1 only"
    return x2.reshape(bsz, seq, d)
```
